```python
import math
import jax, jax.numpy as jnp
from jax import lax
import numpy as np

D_MODEL = 4096
BATCH = 8
SEQ = 2048
DEPTH = 2
DEC_BATCH = 8
DEC_SEQ = 16
PAST_LEN = 2048

CHUNK = 64
MIX_W = D_MODEL // 4
HEAD_DIM = 128
N_HEADS_B = MIX_W // HEAD_DIM
N_HEADS_C = MIX_W // HEAD_DIM
SSM_GROUP = 16
N_GROUPS = MIX_W // SSM_GROUP
SSM_STATE = 64
BAND_CHUNKS = 8
BAND_LEN = BAND_CHUNKS * CHUNK
MAX_REL = 128
SB_BLOCK = 128
N_BRANCH = 3
N_IN = 10 * MIX_W + N_BRANCH * D_MODEL
RMS_EPS = 1e-6
NEG_INF = -1e30

kernel_name = "hybrid_streaming_encoder_step"


def rmsnorm(x, g):
    x32 = x.astype(jnp.float32)
    r = lax.rsqrt(jnp.mean(x32 * x32, axis=-1, keepdims=True) + RMS_EPS)
    return (x32 * r * g.astype(jnp.float32)).astype(x.dtype)


def _cmul_combine(e1, e2):
    a1r, a1i, b1r, b1i = e1
    a2r, a2i, b2r, b2i = e2
    ar = a2r * a1r - a2i * a1i
    ai = a2r * a1i + a2i * a1r
    br = a2r * b1r - a2i * b1i + b2r
    bi = a2r * b1i + a2i * b1r + b2i
    return (ar, ai, br, bi)


def s5_scan(u, h0_re, h0_im, a_re, a_im, log_dt, b_re, b_im, c_re, c_im, d_skip):
    f32 = jnp.float32
    bsz, s, _ = u.shape
    u32 = u.astype(f32)
    ug = u32.reshape(bsz, s, N_GROUPS, SSM_GROUP)
    a_re = a_re.astype(f32)
    a_im = a_im.astype(f32)
    dt = jnp.exp(log_dt.astype(f32))[:, None]
    mag = jnp.exp(a_re * dt)
    ab_re = mag * jnp.cos(a_im * dt)
    ab_im = mag * jnp.sin(a_im * dt)
    den = a_re * a_re + a_im * a_im
    nr = ab_re - 1.0
    k_re = (nr * a_re + ab_im * a_im) / den
    k_im = (ab_im * a_re - nr * a_im) / den
    b_re = b_re.astype(f32)
    b_im = b_im.astype(f32)
    bb_re = k_re[..., None] * b_re - k_im[..., None] * b_im
    bb_im = k_re[..., None] * b_im + k_im[..., None] * b_re
    bu_re = jnp.einsum("bsgp,gnp->bsgn", ug, bb_re)
    bu_im = jnp.einsum("bsgp,gnp->bsgn", ug, bb_im)
    h0_re = h0_re.astype(f32)
    h0_im = h0_im.astype(f32)
    bu_re = bu_re.at[:, 0].add(ab_re * h0_re - ab_im * h0_im)
    bu_im = bu_im.at[:, 0].add(ab_re * h0_im + ab_im * h0_re)
    ar = jnp.broadcast_to(ab_re, bu_re.shape)
    ai = jnp.broadcast_to(ab_im, bu_re.shape)
    _, _, h_re, h_im = lax.associative_scan(_cmul_combine, (ar, ai, bu_re, bu_im), axis=1)
    y = (jnp.einsum("bsgn,gpn->bsgp", h_re, c_re.astype(f32))
         - jnp.einsum("bsgn,gpn->bsgp", h_im, c_im.astype(f32)))
    y = y.reshape(bsz, s, MIX_W) + d_skip.astype(f32) * u32
    return y, h_re[:, -1], h_im[:, -1]


def rel_bias_lookup(rel_bias, qpos, kpos):
    rel = jnp.clip(qpos[:, None] - kpos[None, :], -MAX_REL, MAX_REL) + MAX_REL
    return rel_bias[:, rel].astype(jnp.float32)


def band_attention_prompt(q, k, v, rel_bias):
    bsz, s, h, dh = q.shape
    nc = s // CHUNK
    band = (BAND_CHUNKS + 1) * CHUNK
    qc = q.reshape(bsz, nc, CHUNK, h, dh)
    kc = k.reshape(bsz, nc, CHUNK, h, dh)
    vc = v.reshape(bsz, nc, CHUNK, h, dh)
    cidx = jnp.arange(nc)[:, None] + jnp.arange(BAND_CHUNKS + 1)[None, :] - BAND_CHUNKS
    valid = jnp.repeat(cidx >= 0, CHUNK, axis=1)
    cidx = jnp.maximum(cidx, 0)
    kb = kc[:, cidx].reshape(bsz, nc, band, h, dh)
    vb = vc[:, cidx].reshape(bsz, nc, band, h, dh)
    bias = rel_bias_lookup(rel_bias, jnp.arange(CHUNK) + BAND_CHUNKS * CHUNK, jnp.arange(band))
    sc = jnp.einsum("bcqhd,bckhd->bchqk", qc, kb).astype(jnp.float32) * (1.0 / math.sqrt(dh)) + bias
    sc = jnp.where(valid[None, :, None, None, :], sc, NEG_INF)
    p = jax.nn.softmax(sc, axis=-1)
    o = jnp.einsum("bchqk,bckhd->bcqhd", p.astype(vb.dtype), vb)
    return o.reshape(bsz, s, h * dh)


def band_attention_sample(q, k, v, cache_k, cache_v, rel_bias):
    bsz, n, h, dh = q.shape
    lc = cache_k.shape[1]
    kk = jnp.concatenate([cache_k.astype(k.dtype), k], axis=1)
    vv = jnp.concatenate([cache_v.astype(v.dtype), v], axis=1)
    bias = rel_bias_lookup(rel_bias, jnp.arange(n) + lc, jnp.arange(lc + n))
    sc = jnp.einsum("bqhd,bkhd->bhqk", q, kk).astype(jnp.float32) * (1.0 / math.sqrt(dh)) + bias
    p = jax.nn.softmax(sc, axis=-1)
    o = jnp.einsum("bhqk,bkhd->bqhd", p.astype(vv.dtype), vv)
    return o.reshape(bsz, n, h * dh)


def stick_breaking_block(q, qpos, k, v, kpos):
    dh = q.shape[-1]
    z = jnp.einsum("bqhd,bkhd->bhqk", q, k).astype(jnp.float32) * (1.0 / math.sqrt(dh))
    causal = kpos[None, :] < qpos[:, None]
    log_beta = jax.nn.log_sigmoid(z)
    log_fail = jnp.where(causal, jax.nn.log_sigmoid(-z), 0.0)
    later = lax.cumsum(log_fail, axis=3, reverse=True) - log_fail
    w = jnp.where(causal, jnp.exp(log_beta + later), 0.0)
    return jnp.einsum("bhqk,bkhd->bqhd", w.astype(v.dtype), v)


def stick_breaking_prompt(q, k, v):
    bsz, s, h, dh = q.shape
    nb = s // SB_BLOCK
    qb = q.reshape(bsz, nb, SB_BLOCK, h, dh).transpose(1, 0, 2, 3, 4)
    qpos = jnp.arange(s).reshape(nb, SB_BLOCK)
    kpos = jnp.arange(s)
    o = lax.map(lambda a: stick_breaking_block(a[0], a[1], k, v, kpos), (qb, qpos))
    return o.transpose(1, 0, 2, 3, 4).reshape(bsz, s, h * dh)


def stick_breaking_sample(q, k, v, cache_k, cache_v):
    bsz, n, h, dh = q.shape
    pl = cache_k.shape[1]
    kk = jnp.concatenate([cache_k.astype(k.dtype), k], axis=1)
    vv = jnp.concatenate([cache_v.astype(v.dtype), v], axis=1)
    o = stick_breaking_block(q, pl + jnp.arange(n), kk, vv, jnp.arange(pl + n))
    return o.reshape(bsz, n, h * dh)


def hybrid_layer(x, w, cache):
    (norm_g, w_in, a_re, a_im, log_dt, b_re, b_im, c_re, c_im, d_skip, w_glu, b_glu,
     qn_g, kn_g, rel_bias, w_br_a, w_br_b, w_br_c, gate_b, w_out) = w
    f32 = jnp.float32
    bsz, s, _ = x.shape
    hx = rmsnorm(x, norm_g)
    proj = hx @ w_in
    (u_a, z_a, q_b, k_b, v_b, z_b, q_c, k_c, v_c, z_c, g_in) = jnp.split(
        proj, [MIX_W * i for i in range(1, 11)], axis=-1)
    split_heads = lambda t: t.reshape(bsz, s, -1, HEAD_DIM)
    if cache is None:
        h0_re = jnp.zeros((bsz, N_GROUPS, SSM_STATE), f32)
        h0_im = jnp.zeros((bsz, N_GROUPS, SSM_STATE), f32)
    else:
        sb_k_c, sb_v_c, band_k_c, band_v_c, h0_re, h0_im = cache
    y_a, hT_re, hT_im = s5_scan(u_a, h0_re, h0_im, a_re, a_im, log_dt, b_re, b_im, c_re, c_im, d_skip)
    y_a = jax.nn.gelu(y_a)
    y_a = (y_a * jax.nn.sigmoid(y_a @ w_glu.astype(f32) + b_glu.astype(f32))).astype(x.dtype)
    qb = rmsnorm(split_heads(q_b), qn_g)
    kb = rmsnorm(split_heads(k_b), kn_g)
    vb = split_heads(v_b)
    qc, kc, vc = split_heads(q_c), split_heads(k_c), split_heads(v_c)
    if cache is None:
        y_b = band_attention_prompt(qb, kb, vb, rel_bias)
        y_c = stick_breaking_prompt(qc, kc, vc)
        keep = min(BAND_LEN, s)
        band_k_new, band_v_new = kb[:, s - keep:], vb[:, s - keep:]
    else:
        y_b = band_attention_sample(qb, kb, vb, band_k_c, band_v_c, rel_bias)
        y_c = stick_breaking_sample(qc, kc, vc, sb_k_c, sb_v_c)
        band_k_new, band_v_new = kb, vb
    o_a = (y_a * jax.nn.silu(z_a)) @ w_br_a
    o_b = (y_b * jax.nn.silu(z_b)) @ w_br_b
    o_c = (y_c * jax.nn.silu(z_c)) @ w_br_c
    g = jax.nn.sigmoid((g_in + gate_b).astype(f32)).astype(x.dtype).reshape(bsz, s, N_BRANCH, D_MODEL)
    mixed = g[:, :, 0] * o_a + g[:, :, 1] * o_b + g[:, :, 2] * o_c
    y = x + mixed @ w_out
    return y, (kc, vc, band_k_new, band_v_new, hT_re, hT_im)


def setup_inputs(seed: int = 0) -> dict:
    key = jax.random.key(seed)
    ks = jax.random.split(key, 32)
    f32 = jnp.float32
    nrm = lambda k, shape, scale: jax.random.normal(k, shape, f32) * scale
    band_keep = min(BAND_LEN, PAST_LEN)
    n_idx = jnp.arange(SSM_STATE, dtype=f32)
    return {
        "x_prompt": nrm(ks[0], (BATCH, SEQ, D_MODEL), 1.0),
        "x_sample": nrm(ks[1], (DEC_BATCH, DEC_SEQ, D_MODEL), 1.0),
        "cache_sb_k": nrm(ks[2], (DEPTH, DEC_BATCH, PAST_LEN, N_HEADS_C, HEAD_DIM), 1.0),
        "cache_sb_v": nrm(ks[3], (DEPTH, DEC_BATCH, PAST_LEN, N_HEADS_C, HEAD_DIM), 1.0),
        "cache_band_k": nrm(ks[4], (DEPTH, DEC_BATCH, band_keep, N_HEADS_B, HEAD_DIM), 1.0),
        "cache_band_v": nrm(ks[5], (DEPTH, DEC_BATCH, band_keep, N_HEADS_B, HEAD_DIM), 1.0),
        "state_ssm_re": nrm(ks[6], (DEPTH, DEC_BATCH, N_GROUPS, SSM_STATE), 0.3),
        "state_ssm_im": nrm(ks[7], (DEPTH, DEC_BATCH, N_GROUPS, SSM_STATE), 0.3),
        "norm_g": 1.0 + nrm(ks[8], (DEPTH, D_MODEL), 0.05),
        "w_in": nrm(ks[9], (DEPTH, D_MODEL, N_IN), D_MODEL ** -0.5),
        "ssm_a_re": -0.5 + nrm(ks[10], (DEPTH, N_GROUPS, SSM_STATE), 0.01),
        "ssm_a_im": math.pi * n_idx + nrm(ks[11], (DEPTH, N_GROUPS, SSM_STATE), 0.01),
        "ssm_log_dt": jax.random.uniform(ks[12], (DEPTH, N_GROUPS), f32, math.log(1e-3), math.log(1e-1)),
        "ssm_b_re": nrm(ks[13], (DEPTH, N_GROUPS, SSM_STATE, SSM_GROUP), (2.0 * SSM_GROUP) ** -0.5),
        "ssm_b_im": nrm(ks[14], (DEPTH, N_GROUPS, SSM_STATE, SSM_GROUP), (2.0 * SSM_GROUP) ** -0.5),
        "ssm_c_re": nrm(ks[15], (DEPTH, N_GROUPS, SSM_GROUP, SSM_STATE), SSM_STATE ** -0.5),
        "ssm_c_im": nrm(ks[16], (DEPTH, N_GROUPS, SSM_GROUP, SSM_STATE), SSM_STATE ** -0.5),
        "ssm_d": nrm(ks[17], (DEPTH, MIX_W), 0.5),
        "w_glu": nrm(ks[18], (DEPTH, MIX_W, MIX_W), MIX_W ** -0.5),
        "b_glu": nrm(ks[19], (DEPTH, MIX_W), 0.02),
        "q_norm_g": 1.0 + nrm(ks[20], (DEPTH, HEAD_DIM), 0.05),
        "k_norm_g": 1.0 + nrm(ks[21], (DEPTH, HEAD_DIM), 0.05),
        "rel_bias": nrm(ks[22], (DEPTH, N_HEADS_B, 2 * MAX_REL + 1), 0.1),
        "w_br_a": nrm(ks[23], (DEPTH, MIX_W, D_MODEL), MIX_W ** -0.5),
        "w_br_b": nrm(ks[24], (DEPTH, MIX_W, D_MODEL), MIX_W ** -0.5),
        "w_br_c": nrm(ks[25], (DEPTH, MIX_W, D_MODEL), MIX_W ** -0.5),
        "gate_b": nrm(ks[26], (DEPTH, N_BRANCH * D_MODEL), 0.02),
        "w_out": nrm(ks[27], (DEPTH, D_MODEL, D_MODEL), D_MODEL ** -0.5),
    }


def reference(x_prompt, x_sample, cache_sb_k, cache_sb_v, cache_band_k, cache_band_v,
              state_ssm_re, state_ssm_im, norm_g, w_in, ssm_a_re, ssm_a_im, ssm_log_dt,
              ssm_b_re, ssm_b_im, ssm_c_re, ssm_c_im, ssm_d, w_glu, b_glu, q_norm_g, k_norm_g,
              rel_bias, w_br_a, w_br_b, w_br_c, gate_b, w_out):
    yp = x_prompt
    ys = x_sample
    p_states = []
    s_states = []
    for l in range(DEPTH):
        w = (norm_g[l], w_in[l], ssm_a_re[l], ssm_a_im[l], ssm_log_dt[l], ssm_b_re[l], ssm_b_im[l],
             ssm_c_re[l], ssm_c_im[l], ssm_d[l], w_glu[l], b_glu[l], q_norm_g[l], k_norm_g[l],
             rel_bias[l], w_br_a[l], w_br_b[l], w_br_c[l], gate_b[l], w_out[l])
        yp, sp = hybrid_layer(yp, w, None)
        ys, ss = hybrid_layer(ys, w, (cache_sb_k[l], cache_sb_v[l], cache_band_k[l], cache_band_v[l],
                                      state_ssm_re[l], state_ssm_im[l]))
        p_states.append(sp)
        s_states.append(ss)
    stk = lambda states, i: jnp.stack([st[i] for st in states], axis=0)
    return (yp, ys,
            stk(p_states, 0), stk(p_states, 1), stk(p_states, 2), stk(p_states, 3), stk(p_states, 4), stk(p_states, 5),
            stk(s_states, 0), stk(s_states, 1), stk(s_states, 2), stk(s_states, 3), stk(s_states, 4), stk(s_states, 5))
```

```python
import functools
import math

import jax
import jax.numpy as jnp
import numpy as np
from jax import lax
from jax.experimental import pallas as pl
from jax.experimental.pallas import tpu as pltpu

F32 = jnp.float32
BF16 = jnp.bfloat16

CHUNK = 64
HEAD_DIM = 128
SSM_GROUP = 16
SSM_STATE = 64
BAND_CHUNKS = 8
BAND_LEN = BAND_CHUNKS * CHUNK
MAX_REL = 128
N_BRANCH = 3
RMS_EPS = 1e-6
NEG_INF = -1e30

LANES = 128
SUBLANES = 8
VMEM_LIMIT_BYTES = 56 * 1024 * 1024
MM_TILE = 1024
SSM_CH_BLOCK = 128
SSM_GROUPS_PER_BLOCK = SSM_CH_BLOCK // SSM_GROUP
SSM_LANES = SSM_GROUPS_PER_BLOCK * SSM_STATE
SB_Q_BLOCK = 256
SB_K_BLOCK = 128
BAND_Q_BLOCK = BAND_LEN


def _params(*sem):
    return pltpu.CompilerParams(dimension_semantics=sem, vmem_limit_bytes=VMEM_LIMIT_BYTES)


def _row_tile(m, target, align=SUBLANES):
    if m <= target:
        return m
    for t in range(target - target % align, 0, -align):
        if m % t == 0:
            return t
    raise ValueError((m, target, align))


def _rmsnorm_kernel(x_ref, g_ref, o_ref):
    x = x_ref[...]
    r = lax.rsqrt(jnp.mean(x * x, axis=-1, keepdims=True) + RMS_EPS)
    o_ref[...] = (x * r * g_ref[...]).astype(o_ref.dtype)


def _rmsnorm(x, g):
    m, d = x.shape
    tm = _row_tile(m, 512)
    return pl.pallas_call(
        _rmsnorm_kernel,
        grid=(m // tm,),
        in_specs=[pl.BlockSpec((tm, d), lambda i: (i, 0)),
                  pl.BlockSpec((1, d), lambda i: (0, 0))],
        out_specs=pl.BlockSpec((tm, d), lambda i: (i, 0)),
        out_shape=jax.ShapeDtypeStruct((m, d), BF16),
        compiler_params=_params("parallel"),
        name="rmsnorm",
    )(x, g.reshape(1, d))


def _mm_kernel(x_ref, w_ref, o_ref):
    o_ref[...] = jnp.dot(x_ref[...], w_ref[...], preferred_element_type=F32).astype(o_ref.dtype)


def _matmul(x, w, out_dtype=F32):
    m, k = x.shape
    _, n = w.shape
    tm = _row_tile(m, MM_TILE)
    tn = _row_tile(n, MM_TILE, LANES)
    return pl.pallas_call(
        _mm_kernel,
        grid=(m // tm, n // tn),
        in_specs=[pl.BlockSpec((tm, k), lambda i, j: (i, 0)),
                  pl.BlockSpec((k, tn), lambda i, j: (0, j))],
        out_specs=pl.BlockSpec((tm, tn), lambda i, j: (i, j)),
        out_shape=jax.ShapeDtypeStruct((m, n), out_dtype),
        compiler_params=_params("parallel", "arbitrary"),
        name="in_proj",
    )(x, w)


def _mm_res_kernel(x_ref, w_ref, r_ref, o_ref):
    o_ref[...] = r_ref[...] + jnp.dot(x_ref[...], w_ref[...], preferred_element_type=F32)


def _matmul_residual(x, w, res):
    m, k = x.shape
    _, n = w.shape
    tm = _row_tile(m, MM_TILE)
    tn = _row_tile(n, MM_TILE, LANES)
    return pl.pallas_call(
        _mm_res_kernel,
        grid=(m // tm, n // tn),
        in_specs=[pl.BlockSpec((tm, k), lambda i, j: (i, 0)),
                  pl.BlockSpec((k, tn), lambda i, j: (0, j)),
                  pl.BlockSpec((tm, tn), lambda i, j: (i, j))],
        out_specs=pl.BlockSpec((tm, tn), lambda i, j: (i, j)),
        out_shape=jax.ShapeDtypeStruct((m, n), F32),
        compiler_params=_params("parallel", "arbitrary"),
        name="out_proj",
    )(x, w, res)


def _branch_kernel(la_ref, lb_ref, lc_ref, wa_ref, wb_ref, wc_ref,
                   ga_ref, gb_ref, gc_ref, ba_ref, bb_ref, bc_ref, o_ref):
    def one(l_ref, w_ref, g_ref, b_ref):
        o = jnp.dot(l_ref[...], w_ref[...], preferred_element_type=F32)
        return jax.nn.sigmoid(g_ref[...] + b_ref[...]) * o
    mixed = one(la_ref, wa_ref, ga_ref, ba_ref) + one(lb_ref, wb_ref, gb_ref, bb_ref)
    mixed = mixed + one(lc_ref, wc_ref, gc_ref, bc_ref)
    o_ref[...] = mixed.astype(o_ref.dtype)


def _branch_merge(lhs_a, lhs_b, lhs_c, w_a, w_b, w_c, proj, gate_b, g_col0):
    m, kw = lhs_a.shape
    d = w_a.shape[1]
    tm = _row_tile(m, 512)
    tn = _row_tile(math.gcd(d, g_col0), MM_TILE, LANES)
    assert g_col0 % tn == 0 and d % tn == 0
    gblk = g_col0 // tn
    nblk = d // tn
    lspec = pl.BlockSpec((tm, kw), lambda j, i: (i, 0))
    wspec = pl.BlockSpec((kw, tn), lambda j, i: (0, j))
    gspecs = [pl.BlockSpec((tm, tn), functools.partial(lambda j, i, br: (i, gblk + br * nblk + j), br=br))
              for br in range(N_BRANCH)]
    bspecs = [pl.BlockSpec((1, tn), functools.partial(lambda j, i, br: (0, br * nblk + j), br=br))
              for br in range(N_BRANCH)]
    gb = gate_b.reshape(1, N_BRANCH * d)
    return pl.pallas_call(
        _branch_kernel,
        grid=(d // tn, m // tm),
        in_specs=[lspec, lspec, lspec, wspec, wspec, wspec, *gspecs, *bspecs],
        out_specs=pl.BlockSpec((tm, tn), lambda j, i: (i, j)),
        out_shape=jax.ShapeDtypeStruct((m, d), BF16),
        compiler_params=_params("parallel", "arbitrary"),
        name="branch_merge",
    )(lhs_a, lhs_b, lhs_c, w_a, w_b, w_c, proj, proj, proj, gb, gb, gb)


def _discretize_kernel(are_ref, aim_ref, ldt_ref, bre_ref, bim_ref,
                       abre_ref, abim_ref, bbre_ref, bbim_ref):
    a_re = are_ref[...]
    a_im = aim_ref[...]
    dt = jnp.exp(ldt_ref[...])
    mag = jnp.exp(a_re * dt)
    ab_re = mag * jnp.cos(a_im * dt)
    ab_im = mag * jnp.sin(a_im * dt)
    den = a_re * a_re + a_im * a_im
    nr = ab_re - 1.0
    k_re = (nr * a_re + ab_im * a_im) / den
    k_im = (ab_im * a_re - nr * a_im) / den
    abre_ref[...] = ab_re
    abim_ref[...] = ab_im
    b_re = bre_ref[...]
    b_im = bim_ref[...]
    bbre_ref[...] = k_re[:, None, :] * b_re - k_im[:, None, :] * b_im
    bbim_ref[...] = k_re[:, None, :] * b_im + k_im[:, None, :] * b_re


def _discretize(a_re, a_im, log_dt, b_re, b_im):
    g, n = a_re.shape
    p = b_re.shape[-1]
    gn = jax.ShapeDtypeStruct((g, n), F32)
    gpn = jax.ShapeDtypeStruct((g, p, n), F32)
    return pl.pallas_call(
        _discretize_kernel,
        out_shape=(gn, gn, gpn, gpn),
        name="s5_discretize",
    )(a_re, a_im, log_dt.reshape(g, 1), b_re.transpose(0, 2, 1), b_im.transpose(0, 2, 1))


def _block_diag(x):
    ncb, gl, r, c = x.shape
    eye = jnp.eye(gl, dtype=x.dtype)
    out = x[:, :, :, None, :] * eye[None, :, None, :, None]
    return out.reshape(ncb, gl * r, gl * c)


def _s5_kernel(u_ref, h0_ref, a_ref, bblk_ref, cblk_ref, d_ref, y_ref, ht_ref, hb_ref, h_ref, *, t_steps):
    t_idx = pl.program_id(1)

    @pl.when(t_idx == 0)
    def _():
        h_ref[...] = h0_ref[0]

    rows = t_steps * SUBLANES
    u = u_ref[...].reshape(rows, SSM_CH_BLOCK)
    hb_ref[...] = jnp.dot(u.astype(BF16), bblk_ref[0], preferred_element_type=F32)
    a_re = a_ref[0, :, :SSM_LANES]
    a_im = a_ref[0, :, SSM_LANES:]

    def step(t, carry):
        h_re, h_im = carry
        row = pl.multiple_of(t * SUBLANES, SUBLANES)
        b_re = hb_ref[pl.ds(row, SUBLANES), :SSM_LANES]
        b_im = hb_ref[pl.ds(row, SUBLANES), SSM_LANES:]
        n_re = a_re * h_re - a_im * h_im + b_re
        n_im = a_re * h_im + a_im * h_re + b_im
        hb_ref[pl.ds(row, SUBLANES), :SSM_LANES] = n_re
        hb_ref[pl.ds(row, SUBLANES), SSM_LANES:] = n_im
        return n_re, n_im

    h_re, h_im = lax.fori_loop(0, t_steps, step, (h_ref[:, :SSM_LANES], h_ref[:, SSM_LANES:]),
                               unroll=min(8, t_steps))
    h_ref[:, :SSM_LANES] = h_re
    h_ref[:, SSM_LANES:] = h_im
    ht_ref[0] = h_ref[...]
    y = jnp.dot(hb_ref[...].astype(BF16), cblk_ref[0], preferred_element_type=F32)
    y = y + d_ref[0] * u
    y_ref[...] = jax.nn.gelu(y).reshape(t_steps, SUBLANES, SSM_CH_BLOCK)


def _s5_branch(u_tm, h0_re, h0_im, ab_re, ab_im, bbt_re, bbt_im, c_re, c_im, d_skip):
    s, b, w = u_tm.shape
    assert b == SUBLANES, "the scan keeps one stream per sublane"
    g, n = ab_re.shape
    assert n == SSM_STATE and w == g * SSM_GROUP and w % SSM_CH_BLOCK == 0
    ncb = w // SSM_CH_BLOCK
    gl = SSM_GROUPS_PER_BLOCK
    t_steps = _row_tile(s, 128)

    def lanes(x):
        r = x.shape[0]
        return x.reshape(r, ncb, gl * n).transpose(1, 0, 2)

    h0 = jnp.concatenate([lanes(h0_re), lanes(h0_im)], axis=-1)
    a_tab = jnp.concatenate([lanes(jnp.broadcast_to(ab_re[None], (SUBLANES, g, n))),
                             lanes(jnp.broadcast_to(ab_im[None], (SUBLANES, g, n)))], axis=-1)
    bblk = jnp.concatenate([_block_diag(bbt_re.reshape(ncb, gl, SSM_GROUP, n)),
                            _block_diag(bbt_im.reshape(ncb, gl, SSM_GROUP, n))], axis=-1).astype(BF16)
    cblk = jnp.concatenate([_block_diag(c_re.reshape(ncb, gl, SSM_GROUP, n).transpose(0, 1, 3, 2)),
                            _block_diag(-c_im.reshape(ncb, gl, SSM_GROUP, n).transpose(0, 1, 3, 2))],
                           axis=1).astype(BF16)
    d_tab = d_skip.reshape(ncb, 1, SSM_CH_BLOCK)
    st_spec = pl.BlockSpec((1, SUBLANES, 2 * SSM_LANES), lambda c, t: (c, 0, 0))
    y_tm, ht = pl.pallas_call(
        functools.partial(_s5_kernel, t_steps=t_steps),
        grid=(ncb, s // t_steps),
        in_specs=[pl.BlockSpec((t_steps, SUBLANES, SSM_CH_BLOCK), lambda c, t: (t, 0, c)),
                  st_spec, st_spec,
                  pl.BlockSpec((1, SSM_CH_BLOCK, 2 * SSM_LANES), lambda c, t: (c, 0, 0)),
                  pl.BlockSpec((1, 2 * SSM_LANES, SSM_CH_BLOCK), lambda c, t: (c, 0, 0)),
                  pl.BlockSpec((1, 1, SSM_CH_BLOCK), lambda c, t: (c, 0, 0))],
        out_specs=[pl.BlockSpec((t_steps, SUBLANES, SSM_CH_BLOCK), lambda c, t: (t, 0, c)), st_spec],
        out_shape=(jax.ShapeDtypeStruct((s, b, w), F32),
                   jax.ShapeDtypeStruct((ncb, SUBLANES, 2 * SSM_LANES), F32)),
        scratch_shapes=[pltpu.VMEM((t_steps * SUBLANES, 2 * SSM_LANES), F32),
                        pltpu.VMEM((SUBLANES, 2 * SSM_LANES), F32)],
        compiler_params=_params("parallel", "arbitrary"),
        name="s5_scan",
    )(u_tm, h0, a_tab, bblk, cblk, d_tab)

    def unlanes(x):
        return x.transpose(1, 0, 2).reshape(b, g, n)

    return y_tm, unlanes(ht[..., :SSM_LANES]), unlanes(ht[..., SSM_LANES:])


def _glu_kernel(y_ref, w_ref, b_ref, z_ref, o_ref):
    y = y_ref[...]
    gate = jnp.dot(y.astype(BF16), w_ref[...], preferred_element_type=F32) + b_ref[...]
    o_ref[...] = (y * jax.nn.sigmoid(gate) * jax.nn.silu(z_ref[...])).astype(o_ref.dtype)


def _glu_gate(y_tm, w_glu, b_glu, proj, z_col0, bsz, s):
    w = w_glu.shape[0]
    tm = _row_tile(s, 512)
    nt = s // tm
    zblk = z_col0 // w
    return pl.pallas_call(
        _glu_kernel,
        grid=(bsz, nt),
        in_specs=[pl.BlockSpec((tm, w), lambda b, t: (t, b)),
                  pl.BlockSpec((w, w), lambda b, t: (0, 0)),
                  pl.BlockSpec((1, w), lambda b, t: (0, 0)),
                  pl.BlockSpec((tm, w), lambda b, t: (b * nt + t, zblk))],
        out_specs=pl.BlockSpec((tm, w), lambda b, t: (b * nt + t, 0)),
        out_shape=jax.ShapeDtypeStruct((bsz * s, w), BF16),
        compiler_params=_params("parallel", "arbitrary"),
        name="glu_gate",
    )(y_tm.reshape(s, bsz * w), w_glu, b_glu.reshape(1, w), proj)


def _head_rms(x, g):
    r = lax.rsqrt(jnp.mean(x * x, axis=-1, keepdims=True) + RMS_EPS)
    return x * r * g


def _qk(q, k):
    return lax.dot_general(q, k, (((1,), (1,)), ((), ())), preferred_element_type=F32)


def _band_prompt_kernel(q_ref, kp_ref, kc_ref, vp_ref, vc_ref, z_ref, bias_ref, qg_ref, kg_ref,
                        o_ref, kn_ref):
    i = pl.program_id(2)
    scale = 1.0 / math.sqrt(HEAD_DIM)
    qn = _head_rms(q_ref[0], qg_ref[...]).astype(BF16)
    kpn = _head_rms(kp_ref[0], kg_ref[...])
    kcn = _head_rms(kc_ref[0], kg_ref[...])
    kn_ref[0] = kcn
    s_p = _qk(qn, kpn.astype(BF16)) * scale + bias_ref[0, :, :BAND_Q_BLOCK]
    s_p = jnp.where(i > 0, s_p, NEG_INF)
    s_c = _qk(qn, kcn.astype(BF16)) * scale + bias_ref[0, :, BAND_Q_BLOCK:]
    m = jnp.maximum(jnp.max(s_p, axis=-1, keepdims=True), jnp.max(s_c, axis=-1, keepdims=True))
    p_p = jnp.exp(s_p - m)
    p_c = jnp.exp(s_c - m)
    l = jnp.sum(p_p, axis=-1, keepdims=True) + jnp.sum(p_c, axis=-1, keepdims=True)
    o = jnp.dot(p_p.astype(BF16), vp_ref[0].astype(BF16), preferred_element_type=F32)
    o = o + jnp.dot(p_c.astype(BF16), vc_ref[0].astype(BF16), preferred_element_type=F32)
    o_ref[0] = ((o / l) * jax.nn.silu(z_ref[0])).astype(o_ref.dtype)


def _band_bias_prompt(rel_bias):
    r = np.arange(BAND_Q_BLOCK)
    c = np.arange(2 * BAND_Q_BLOCK)
    jq = r // CHUNK
    i = r % CHUNK
    jb = c[None, :] - (jq * CHUNK)[:, None]
    inside = (jb >= 0) & (jb < (BAND_CHUNKS + 1) * CHUNK)
    rel = np.clip(i[:, None] + BAND_CHUNKS * CHUNK - jb, -MAX_REL, MAX_REL) + MAX_REL
    bias = jnp.take(rel_bias.astype(F32), jnp.asarray(rel, jnp.int32), axis=1)
    return jnp.where(jnp.asarray(inside)[None], bias, NEG_INF)


def _band_prompt(proj3, q_col0, k_col0, v_col0, z_col0, rel_bias, qn_g, kn_g):
    bsz, s, _ = proj3.shape
    nh = rel_bias.shape[0]
    tq = BAND_Q_BLOCK
    assert s % tq == 0 and s >= tq
    nq = s // tq
    qb, kb, vb, zb = (c // HEAD_DIM for c in (q_col0, k_col0, v_col0, z_col0))
    bias = _band_bias_prompt(rel_bias)
    blk = (1, tq, HEAD_DIM)
    gspec = pl.BlockSpec((1, HEAD_DIM), lambda h, b, i: (0, 0))
    return pl.pallas_call(
        _band_prompt_kernel,
        grid=(nh, bsz, nq),
        in_specs=[pl.BlockSpec(blk, lambda h, b, i: (b, i, qb + h)),
                  pl.BlockSpec(blk, lambda h, b, i: (b, jnp.maximum(i - 1, 0), kb + h)),
                  pl.BlockSpec(blk, lambda h, b, i: (b, i, kb + h)),
                  pl.BlockSpec(blk, lambda h, b, i: (b, jnp.maximum(i - 1, 0), vb + h)),
                  pl.BlockSpec(blk, lambda h, b, i: (b, i, vb + h)),
                  pl.BlockSpec(blk, lambda h, b, i: (b, i, zb + h)),
                  pl.BlockSpec((1, tq, 2 * tq), lambda h, b, i: (h, 0, 0)),
                  gspec, gspec],
        out_specs=[pl.BlockSpec(blk, lambda h, b, i: (b, i, h)),
                   pl.BlockSpec(blk, lambda h, b, i: (b, 0, h))],
        out_shape=(jax.ShapeDtypeStruct((bsz, s, nh * HEAD_DIM), BF16),
                   jax.ShapeDtypeStruct((bsz, tq, nh * HEAD_DIM), F32)),
        compiler_params=_params("parallel", "parallel", "arbitrary"),
        name="band_prompt",
    )(proj3, proj3, proj3, proj3, proj3, proj3, bias, qn_g.reshape(1, HEAD_DIM), kn_g.reshape(1, HEAD_DIM))


def _band_sample_kernel(q_ref, k_ref, v_ref, ck_ref, cv_ref, z_ref, bias_ref, qg_ref, kg_ref,
                        o_ref, kn_ref, *, lc):
    scale = 1.0 / math.sqrt(HEAD_DIM)
    n = q_ref.shape[1]
    qn = _head_rms(q_ref[0], qg_ref[...]).astype(BF16)
    kn = _head_rms(k_ref[0], kg_ref[...])
    kn_ref[0] = kn
    zpad = jnp.zeros((LANES - n, HEAD_DIM), F32)
    kn = jnp.concatenate([kn, zpad], axis=0)
    v_new = jnp.concatenate([v_ref[0], zpad], axis=0)
    s_p = _qk(qn, ck_ref[0].astype(BF16)) * scale + bias_ref[0, :, :lc]
    s_c = _qk(qn, kn.astype(BF16)) * scale + bias_ref[0, :, lc:]
    m = jnp.maximum(jnp.max(s_p, axis=-1, keepdims=True), jnp.max(s_c, axis=-1, keepdims=True))
    p_p = jnp.exp(s_p - m)
    p_c = jnp.exp(s_c - m)
    l = jnp.sum(p_p, axis=-1, keepdims=True) + jnp.sum(p_c, axis=-1, keepdims=True)
    o = jnp.dot(p_p.astype(BF16), cv_ref[0].astype(BF16), preferred_element_type=F32)
    o = o + jnp.dot(p_c.astype(BF16), v_new.astype(BF16), preferred_element_type=F32)
    o_ref[0] = ((o / l) * jax.nn.silu(z_ref[0])).astype(o_ref.dtype)


def _band_sample(proj3, q_col0, k_col0, v_col0, z_col0, cache_k, cache_v, rel_bias, qn_g, kn_g):
    bsz, n, _ = proj3.shape
    lc = cache_k.shape[1]
    nh = rel_bias.shape[0]
    qb, kb, vb, zb = (c // HEAD_DIM for c in (q_col0, k_col0, v_col0, z_col0))
    rel = np.clip((np.arange(n) + lc)[:, None] - np.arange(lc + n)[None, :], -MAX_REL, MAX_REL) + MAX_REL
    bias = jnp.take(rel_bias.astype(F32), jnp.asarray(rel, jnp.int32), axis=1)
    assert n <= LANES
    bias = jnp.pad(bias, ((0, 0), (0, 0), (0, LANES - n)), constant_values=NEG_INF)
    blk = (1, n, HEAD_DIM)
    cblk = (1, lc, HEAD_DIM)
    gspec = pl.BlockSpec((1, HEAD_DIM), lambda b, h: (0, 0))
    return pl.pallas_call(
        functools.partial(_band_sample_kernel, lc=lc),
        grid=(bsz, nh),
        in_specs=[pl.BlockSpec(blk, lambda b, h: (b, 0, qb + h)),
                  pl.BlockSpec(blk, lambda b, h: (b, 0, kb + h)),
                  pl.BlockSpec(blk, lambda b, h: (b, 0, vb + h)),
                  pl.BlockSpec(cblk, lambda b, h: (b, 0, h)),
                  pl.BlockSpec(cblk, lambda b, h: (b, 0, h)),
                  pl.BlockSpec(blk, lambda b, h: (b, 0, zb + h)),
                  pl.BlockSpec((1, n, lc + LANES), lambda b, h: (h, 0, 0)),
                  gspec, gspec],
        out_specs=[pl.BlockSpec(blk, lambda b, h: (b, 0, h)),
                   pl.BlockSpec(blk, lambda b, h: (b, 0, h))],
        out_shape=(jax.ShapeDtypeStruct((bsz, n, nh * HEAD_DIM), BF16),
                   jax.ShapeDtypeStruct((bsz, n, nh * HEAD_DIM), F32)),
        compiler_params=_params("parallel", "parallel"),
        name="band_sample",
    )(proj3, proj3, proj3, cache_k, cache_v, proj3, bias, qn_g.reshape(1, HEAD_DIM), kn_g.reshape(1, HEAD_DIM))


def _suffix_matrix():
    j = lax.broadcasted_iota(jnp.int32, (SB_K_BLOCK, 2 * SB_K_BLOCK), 0)
    c = lax.broadcasted_iota(jnp.int32, (SB_K_BLOCK, 2 * SB_K_BLOCK), 1)
    return jnp.where((c >= SB_K_BLOCK) | (j > c), 1.0, 0.0).astype(BF16)


def _sb_block(q, k, v, carry, acc, umat, causal):
    scale = 1.0 / math.sqrt(HEAD_DIM)
    z = _qk(q, k.astype(BF16)) * scale
    log_fail = -(jnp.maximum(z, 0.0) + jnp.log1p(jnp.exp(-jnp.abs(z))))
    if causal is not None:
        log_fail = jnp.where(causal, log_fail, 0.0)
    hi = log_fail.astype(BF16)
    lo = (log_fail - hi.astype(F32)).astype(BF16)
    sums = jnp.dot(hi, umat, preferred_element_type=F32) + jnp.dot(lo, umat, preferred_element_type=F32)
    later = sums[:, :SB_K_BLOCK] + carry
    w = jnp.exp(log_fail + z + later)
    if causal is not None:
        w = jnp.where(causal, w, 0.0)
    acc = acc + jnp.dot(w.astype(BF16), v.astype(BF16), preferred_element_type=F32)
    return carry + sums[:, SB_K_BLOCK:], acc


def _sb_prompt_kernel(q_ref, k_ref, v_ref, z_ref, o_ref):
    qi = pl.program_id(2)
    tq = SB_Q_BLOCK
    q = q_ref[0].astype(BF16)
    umat = _suffix_matrix()
    q0 = qi * tq
    qpos = q0 + lax.broadcasted_iota(jnp.int32, (tq, SB_K_BLOCK), 0)
    col = lax.broadcasted_iota(jnp.int32, (tq, SB_K_BLOCK), 1)
    carry = jnp.zeros((tq, SB_K_BLOCK), F32)
    acc = jnp.zeros((tq, HEAD_DIM), F32)
    n_diag = tq // SB_K_BLOCK
    for d in reversed(range(n_diag)):
        k0 = pl.multiple_of(q0 + d * SB_K_BLOCK, SB_K_BLOCK)
        causal = (k0 + col) < qpos
        carry, acc = _sb_block(q, k_ref[0, pl.ds(k0, SB_K_BLOCK), :], v_ref[0, pl.ds(k0, SB_K_BLOCK), :],
                               carry, acc, umat, causal)

    def body(j, ca):
        k0 = pl.multiple_of(q0 - (j + 1) * SB_K_BLOCK, SB_K_BLOCK)
        return _sb_block(q, k_ref[0, pl.ds(k0, SB_K_BLOCK), :], v_ref[0, pl.ds(k0, SB_K_BLOCK), :],
                         ca[0], ca[1], umat, None)

    carry, acc = lax.fori_loop(0, qi * n_diag, body, (carry, acc))
    o_ref[0] = (acc * jax.nn.silu(z_ref[0])).astype(o_ref.dtype)


def _sb_prompt(proj3, q_col0, k_col0, v_col0, z_col0, nh):
    bsz, s, _ = proj3.shape
    tq = _row_tile(s, SB_Q_BLOCK)
    assert tq == SB_Q_BLOCK
    qb, kb, vb, zb = (c // HEAD_DIM for c in (q_col0, k_col0, v_col0, z_col0))
    blk = (1, tq, HEAD_DIM)
    full = (1, s, HEAD_DIM)
    return pl.pallas_call(
        _sb_prompt_kernel,
        grid=(bsz, nh, s // tq),
        in_specs=[pl.BlockSpec(blk, lambda b, h, i: (b, i, qb + h)),
                  pl.BlockSpec(full, lambda b, h, i: (b, 0, kb + h)),
                  pl.BlockSpec(full, lambda b, h, i: (b, 0, vb + h)),
                  pl.BlockSpec(blk, lambda b, h, i: (b, i, zb + h))],
        out_specs=pl.BlockSpec(blk, lambda b, h, i: (b, i, h)),
        out_shape=jax.ShapeDtypeStruct((bsz, s, nh * HEAD_DIM), BF16),
        compiler_params=_params("parallel", "parallel", "arbitrary"),
        name="sb_prompt",
    )(proj3, proj3, proj3, proj3)


def _sb_sample_kernel(q_ref, k_ref, v_ref, ck_ref, cv_ref, z_ref, o_ref, *, past_len):
    n = q_ref.shape[1]
    q = q_ref[0].astype(BF16)
    umat = _suffix_matrix()
    row = lax.broadcasted_iota(jnp.int32, (n, SB_K_BLOCK), 0)
    col = lax.broadcasted_iota(jnp.int32, (n, SB_K_BLOCK), 1)
    carry = jnp.zeros((n, SB_K_BLOCK), F32)
    acc = jnp.zeros((n, HEAD_DIM), F32)
    carry, acc = _sb_block(q, k_ref[0], v_ref[0], carry, acc, umat, col < row)

    def body(j, ca):
        k0 = pl.multiple_of(past_len - (j + 1) * SB_K_BLOCK, SB_K_BLOCK)
        return _sb_block(q, ck_ref[0, pl.ds(k0, SB_K_BLOCK), :], cv_ref[0, pl.ds(k0, SB_K_BLOCK), :],
                         ca[0], ca[1], umat, None)

    carry, acc = lax.fori_loop(0, past_len // SB_K_BLOCK, body, (carry, acc))
    o_ref[0] = (acc * jax.nn.silu(z_ref[0])).astype(o_ref.dtype)


def _sb_sample(proj3, q_col0, z_col0, k_new, v_new, cache_k, cache_v, nh):
    bsz, n, _ = proj3.shape
    past_len = cache_k.shape[1]
    assert past_len % SB_K_BLOCK == 0 and n <= SB_K_BLOCK
    qb, zb = q_col0 // HEAD_DIM, z_col0 // HEAD_DIM
    pad = ((0, 0), (0, SB_K_BLOCK - n), (0, 0))
    k_pad = jnp.pad(k_new, pad)
    v_pad = jnp.pad(v_new, pad)
    blk = (1, n, HEAD_DIM)
    own = (1, SB_K_BLOCK, HEAD_DIM)
    past = (1, past_len, HEAD_DIM)
    return pl.pallas_call(
        functools.partial(_sb_sample_kernel, past_len=past_len),
        grid=(bsz, nh),
        in_specs=[pl.BlockSpec(blk, lambda b, h: (b, 0, qb + h)),
                  pl.BlockSpec(own, lambda b, h: (b, 0, h)),
                  pl.BlockSpec(own, lambda b, h: (b, 0, h)),
                  pl.BlockSpec(past, lambda b, h: (b, 0, h)),
                  pl.BlockSpec(past, lambda b, h: (b, 0, h)),
                  pl.BlockSpec(blk, lambda b, h: (b, 0, zb + h))],
        out_specs=pl.BlockSpec(blk, lambda b, h: (b, 0, h)),
        out_shape=jax.ShapeDtypeStruct((bsz, n, nh * HEAD_DIM), BF16),
        compiler_params=_params("parallel", "parallel"),
        name="sb_sample",
    )(proj3, k_pad, v_pad, cache_k, cache_v, proj3)


def _layer(x, w, cache):
    (norm_g, w_in, ssm, w_glu, b_glu, qn_g, kn_g, rel_bias, w_br_a, w_br_b, w_br_c, gate_b, w_out) = w
    ab_re, ab_im, bbt_re, bbt_im, c_re, c_im, d_skip = ssm
    bsz, s, d = x.shape
    mw = w_glu.shape[0]
    nh = mw // HEAD_DIM
    g = ab_re.shape[0]
    m = bsz * s
    x2 = x.reshape(m, d)
    hx = _rmsnorm(x2, norm_g)
    proj = _matmul(hx, w_in)
    n_in = proj.shape[1]
    proj3 = proj.reshape(bsz, s, n_in)
    col = lambda i: i * mw
    u_tm = proj3[:, :, :mw].transpose(1, 0, 2)
    if cache is None:
        h0_re = jnp.zeros((bsz, g, SSM_STATE), F32)
        h0_im = h0_re
    else:
        h0_re, h0_im = cache[4], cache[5]
    y_tm, ht_re, ht_im = _s5_branch(u_tm, h0_re, h0_im, ab_re, ab_im, bbt_re, bbt_im, c_re, c_im, d_skip)
    lhs_a = _glu_gate(y_tm, w_glu, b_glu, proj, col(1), bsz, s)
    k_c = proj3[:, :, col(7):col(8)]
    v_c = proj3[:, :, col(8):col(9)]
    v_b = proj3[:, :, col(4):col(5)]
    if cache is None:
        lhs_b, band_k = _band_prompt(proj3, col(2), col(3), col(4), col(5), rel_bias, qn_g, kn_g)
        keep = min(BAND_LEN, s)
        band_v = v_b[:, s - keep:]
        lhs_c = _sb_prompt(proj3, col(6), col(7), col(8), col(9), nh)
    else:
        sb_k, sb_v, bk, bv = (t.reshape(bsz, t.shape[1], mw) for t in cache[:4])
        lhs_b, band_k = _band_sample(proj3, col(2), col(3), col(4), col(5), bk, bv, rel_bias, qn_g, kn_g)
        band_v = v_b
        lhs_c = _sb_sample(proj3, col(6), col(9), k_c, v_c, sb_k, sb_v, nh)
    mixed = _branch_merge(lhs_a, lhs_b.reshape(m, mw), lhs_c.reshape(m, mw), w_br_a, w_br_b, w_br_c,
                          proj, gate_b, col(10))
    y = _matmul_residual(mixed, w_out, x2).reshape(bsz, s, d)
    heads = lambda t: t.reshape(bsz, t.shape[1], nh, HEAD_DIM)
    return y, (heads(k_c), heads(v_c), heads(band_k), heads(band_v), ht_re, ht_im)


def kernel(x_prompt, x_sample, cache_sb_k, cache_sb_v, cache_band_k, cache_band_v, state_ssm_re, state_ssm_im, norm_g, w_in, ssm_a_re, ssm_a_im, ssm_log_dt, ssm_b_re, ssm_b_im, ssm_c_re, ssm_c_im, ssm_d, w_glu, b_glu, q_norm_g, k_norm_g, rel_bias, w_br_a, w_br_b, w_br_c, gate_b, w_out):
    depth = w_in.shape[0]
    yp, ys = x_prompt, x_sample
    p_states, s_states = [], []
    for l in range(depth):
        ssm = _discretize(ssm_a_re[l], ssm_a_im[l], ssm_log_dt[l], ssm_b_re[l], ssm_b_im[l])
        w = (norm_g[l], w_in[l].astype(BF16), (*ssm, ssm_c_re[l], ssm_c_im[l], ssm_d[l]),
             w_glu[l].astype(BF16), b_glu[l], q_norm_g[l], k_norm_g[l], rel_bias[l],
             w_br_a[l].astype(BF16), w_br_b[l].astype(BF16), w_br_c[l].astype(BF16), gate_b[l],
             w_out[l].astype(BF16))
        yp, sp = _layer(yp, w, None)
        ys, ss = _layer(ys, w, (cache_sb_k[l], cache_sb_v[l], cache_band_k[l], cache_band_v[l],
                                state_ssm_re[l], state_ssm_im[l]))
        p_states.append(sp)
        s_states.append(ss)
    stk = lambda states, i: jnp.stack([st[i] for st in states], axis=0)
    return (yp, ys,
            *(stk(p_states, i) for i in range(6)),
            *(stk(s_states, i) for i in range(6)))
```

```python
import functools
import math

import jax
import jax.numpy as jnp
import numpy as np
from jax import lax
from jax.experimental import pallas as pl
from jax.experimental.pallas import tpu as pltpu

F32 = jnp.float32
BF16 = jnp.bfloat16

CHUNK = 64
HEAD_DIM = 128
SSM_GROUP = 16
SSM_STATE = 64
BAND_CHUNKS = 8
BAND_LEN = BAND_CHUNKS * CHUNK
MAX_REL = 128
N_BRANCH = 3
RMS_EPS = 1e-6
NEG_INF = -1e30

LANES = 128
SUBLANES = 8
VMEM_LIMIT_BYTES = 56 * 1024 * 1024
MM_TILE = 1024
SSM_CH_BLOCK = 128
SSM_GROUPS_PER_BLOCK = SSM_CH_BLOCK // SSM_GROUP
SSM_LANES = SSM_GROUPS_PER_BLOCK * SSM_STATE
SB_Q_BLOCK = 512
SB_K_BLOCK = 128
SB_GROUP = 2 * SB_K_BLOCK
SB_PROMPT_UNROLL = 2
SB_SAMPLE_UNROLL = 4
BAND_Q_BLOCK = BAND_LEN


def _params(*sem):
    return pltpu.CompilerParams(dimension_semantics=sem, vmem_limit_bytes=VMEM_LIMIT_BYTES)


def _row_tile(m, target, align=SUBLANES):
    if m <= target:
        return m
    for t in range(target - target % align, 0, -align):
        if m % t == 0:
            return t
    raise ValueError((m, target, align))


def _rmsnorm_kernel(x_ref, g_ref, o_ref):
    x = x_ref[...]
    r = lax.rsqrt(jnp.mean(x * x, axis=-1, keepdims=True) + RMS_EPS)
    o_ref[...] = (x * r * g_ref[...]).astype(o_ref.dtype)


def _rmsnorm(x, g):
    m, d = x.shape
    tm = _row_tile(m, 512)
    return pl.pallas_call(
        _rmsnorm_kernel,
        grid=(m // tm,),
        in_specs=[pl.BlockSpec((tm, d), lambda i: (i, 0)),
                  pl.BlockSpec((1, d), lambda i: (0, 0))],
        out_specs=pl.BlockSpec((tm, d), lambda i: (i, 0)),
        out_shape=jax.ShapeDtypeStruct((m, d), BF16),
        compiler_params=_params("parallel"),
        name="rmsnorm",
    )(x, g.reshape(1, d))


def _mm_kernel(x_ref, w_ref, o_ref):
    o_ref[...] = jnp.dot(x_ref[...], w_ref[...], preferred_element_type=F32).astype(o_ref.dtype)


def _matmul(x, w, out_dtype=F32):
    m, k = x.shape
    _, n = w.shape
    tm = _row_tile(m, MM_TILE)
    tn = _row_tile(n, MM_TILE, LANES)
    return pl.pallas_call(
        _mm_kernel,
        grid=(m // tm, n // tn),
        in_specs=[pl.BlockSpec((tm, k), lambda i, j: (i, 0)),
                  pl.BlockSpec((k, tn), lambda i, j: (0, j))],
        out_specs=pl.BlockSpec((tm, tn), lambda i, j: (i, j)),
        out_shape=jax.ShapeDtypeStruct((m, n), out_dtype),
        compiler_params=_params("parallel", "arbitrary"),
        name="in_proj",
    )(x, w)


def _mm_res_kernel(x_ref, w_ref, r_ref, o_ref):
    o_ref[...] = r_ref[...] + jnp.dot(x_ref[...], w_ref[...], preferred_element_type=F32)


def _matmul_residual(x, w, res):
    m, k = x.shape
    _, n = w.shape
    tm = _row_tile(m, MM_TILE)
    tn = _row_tile(n, MM_TILE, LANES)
    return pl.pallas_call(
        _mm_res_kernel,
        grid=(m // tm, n // tn),
        in_specs=[pl.BlockSpec((tm, k), lambda i, j: (i, 0)),
                  pl.BlockSpec((k, tn), lambda i, j: (0, j)),
                  pl.BlockSpec((tm, tn), lambda i, j: (i, j))],
        out_specs=pl.BlockSpec((tm, tn), lambda i, j: (i, j)),
        out_shape=jax.ShapeDtypeStruct((m, n), F32),
        compiler_params=_params("parallel", "arbitrary"),
        name="out_proj",
    )(x, w, res)


def _branch_kernel(la_ref, lb_ref, lc_ref, wa_ref, wb_ref, wc_ref,
                   ga_ref, gb_ref, gc_ref, ba_ref, bb_ref, bc_ref, o_ref):
    def one(l_ref, w_ref, g_ref, b_ref):
        o = jnp.dot(l_ref[...], w_ref[...], preferred_element_type=F32)
        return jax.nn.sigmoid(g_ref[...] + b_ref[...]) * o
    mixed = one(la_ref, wa_ref, ga_ref, ba_ref) + one(lb_ref, wb_ref, gb_ref, bb_ref)
    mixed = mixed + one(lc_ref, wc_ref, gc_ref, bc_ref)
    o_ref[...] = mixed.astype(o_ref.dtype)


def _branch_merge(lhs_a, lhs_b, lhs_c, w_a, w_b, w_c, proj, gate_b, g_col0):
    m, kw = lhs_a.shape
    d = w_a.shape[1]
    tm = _row_tile(m, 512)
    tn = _row_tile(math.gcd(d, g_col0), MM_TILE, LANES)
    assert g_col0 % tn == 0 and d % tn == 0
    gblk = g_col0 // tn
    nblk = d // tn
    lspec = pl.BlockSpec((tm, kw), lambda j, i: (i, 0))
    wspec = pl.BlockSpec((kw, tn), lambda j, i: (0, j))
    gspecs = [pl.BlockSpec((tm, tn), functools.partial(lambda j, i, br: (i, gblk + br * nblk + j), br=br))
              for br in range(N_BRANCH)]
    bspecs = [pl.BlockSpec((1, tn), functools.partial(lambda j, i, br: (0, br * nblk + j), br=br))
              for br in range(N_BRANCH)]
    gb = gate_b.reshape(1, N_BRANCH * d)
    return pl.pallas_call(
        _branch_kernel,
        grid=(d // tn, m // tm),
        in_specs=[lspec, lspec, lspec, wspec, wspec, wspec, *gspecs, *bspecs],
        out_specs=pl.BlockSpec((tm, tn), lambda j, i: (i, j)),
        out_shape=jax.ShapeDtypeStruct((m, d), BF16),
        compiler_params=_params("parallel", "arbitrary"),
        name="branch_merge",
    )(lhs_a, lhs_b, lhs_c, w_a, w_b, w_c, proj, proj, proj, gb, gb, gb)


def _discretize_kernel(are_ref, aim_ref, ldt_ref, bre_ref, bim_ref,
                       abre_ref, abim_ref, bbre_ref, bbim_ref):
    a_re = are_ref[...]
    a_im = aim_ref[...]
    dt = jnp.exp(ldt_ref[...])
    mag = jnp.exp(a_re * dt)
    ab_re = mag * jnp.cos(a_im * dt)
    ab_im = mag * jnp.sin(a_im * dt)
    den = a_re * a_re + a_im * a_im
    nr = ab_re - 1.0
    k_re = (nr * a_re + ab_im * a_im) / den
    k_im = (ab_im * a_re - nr * a_im) / den
    abre_ref[...] = ab_re
    abim_ref[...] = ab_im
    b_re = bre_ref[...]
    b_im = bim_ref[...]
    bbre_ref[...] = k_re[:, None, :] * b_re - k_im[:, None, :] * b_im
    bbim_ref[...] = k_re[:, None, :] * b_im + k_im[:, None, :] * b_re


def _discretize(a_re, a_im, log_dt, b_re, b_im):
    g, n = a_re.shape
    p = b_re.shape[-1]
    gn = jax.ShapeDtypeStruct((g, n), F32)
    gpn = jax.ShapeDtypeStruct((g, p, n), F32)
    return pl.pallas_call(
        _discretize_kernel,
        out_shape=(gn, gn, gpn, gpn),
        name="s5_discretize",
    )(a_re, a_im, log_dt.reshape(g, 1), b_re.transpose(0, 2, 1), b_im.transpose(0, 2, 1))


def _block_diag(x):
    ncb, gl, r, c = x.shape
    eye = jnp.eye(gl, dtype=x.dtype)
    out = x[:, :, :, None, :] * eye[None, :, None, :, None]
    return out.reshape(ncb, gl * r, gl * c)


def _s5_kernel(u_ref, h0_ref, a_ref, bblk_ref, cblk_ref, d_ref, y_ref, ht_ref, hb_ref, h_ref, *, t_steps):
    t_idx = pl.program_id(1)

    @pl.when(t_idx == 0)
    def _():
        h_ref[...] = h0_ref[0]

    rows = t_steps * SUBLANES
    u = u_ref[...].reshape(rows, SSM_CH_BLOCK)
    hb_ref[...] = jnp.dot(u.astype(BF16), bblk_ref[0], preferred_element_type=F32)
    a_re = a_ref[0, :, :SSM_LANES]
    a_im = a_ref[0, :, SSM_LANES:]

    def step(t, carry):
        h_re, h_im = carry
        row = pl.multiple_of(t * SUBLANES, SUBLANES)
        b_re = hb_ref[pl.ds(row, SUBLANES), :SSM_LANES]
        b_im = hb_ref[pl.ds(row, SUBLANES), SSM_LANES:]
        n_re = a_re * h_re - a_im * h_im + b_re
        n_im = a_re * h_im + a_im * h_re + b_im
        hb_ref[pl.ds(row, SUBLANES), :SSM_LANES] = n_re
        hb_ref[pl.ds(row, SUBLANES), SSM_LANES:] = n_im
        return n_re, n_im

    h_re, h_im = lax.fori_loop(0, t_steps, step, (h_ref[:, :SSM_LANES], h_ref[:, SSM_LANES:]),
                               unroll=min(8, t_steps))
    h_ref[:, :SSM_LANES] = h_re
    h_ref[:, SSM_LANES:] = h_im
    ht_ref[0] = h_ref[...]
    y = jnp.dot(hb_ref[...].astype(BF16), cblk_ref[0], preferred_element_type=F32)
    y = y + d_ref[0] * u
    y_ref[...] = jax.nn.gelu(y).reshape(t_steps, SUBLANES, SSM_CH_BLOCK)


def _s5_branch(u_tm, h0_re, h0_im, ab_re, ab_im, bbt_re, bbt_im, c_re, c_im, d_skip):
    s, b, w = u_tm.shape
    assert b == SUBLANES, "the scan keeps one stream per sublane"
    g, n = ab_re.shape
    assert n == SSM_STATE and w == g * SSM_GROUP and w % SSM_CH_BLOCK == 0
    ncb = w // SSM_CH_BLOCK
    gl = SSM_GROUPS_PER_BLOCK
    t_steps = _row_tile(s, 128)

    def lanes(x):
        r = x.shape[0]
        return x.reshape(r, ncb, gl * n).transpose(1, 0, 2)

    h0 = jnp.concatenate([lanes(h0_re), lanes(h0_im)], axis=-1)
    a_tab = jnp.concatenate([lanes(jnp.broadcast_to(ab_re[None], (SUBLANES, g, n))),
                             lanes(jnp.broadcast_to(ab_im[None], (SUBLANES, g, n)))], axis=-1)
    bblk = jnp.concatenate([_block_diag(bbt_re.reshape(ncb, gl, SSM_GROUP, n)),
                            _block_diag(bbt_im.reshape(ncb, gl, SSM_GROUP, n))], axis=-1).astype(BF16)
    cblk = jnp.concatenate([_block_diag(c_re.reshape(ncb, gl, SSM_GROUP, n).transpose(0, 1, 3, 2)),
                            _block_diag(-c_im.reshape(ncb, gl, SSM_GROUP, n).transpose(0, 1, 3, 2))],
                           axis=1).astype(BF16)
    d_tab = d_skip.reshape(ncb, 1, SSM_CH_BLOCK)
    st_spec = pl.BlockSpec((1, SUBLANES, 2 * SSM_LANES), lambda c, t: (c, 0, 0))
    y_tm, ht = pl.pallas_call(
        functools.partial(_s5_kernel, t_steps=t_steps),
        grid=(ncb, s // t_steps),
        in_specs=[pl.BlockSpec((t_steps, SUBLANES, SSM_CH_BLOCK), lambda c, t: (t, 0, c)),
                  st_spec, st_spec,
                  pl.BlockSpec((1, SSM_CH_BLOCK, 2 * SSM_LANES), lambda c, t: (c, 0, 0)),
                  pl.BlockSpec((1, 2 * SSM_LANES, SSM_CH_BLOCK), lambda c, t: (c, 0, 0)),
                  pl.BlockSpec((1, 1, SSM_CH_BLOCK), lambda c, t: (c, 0, 0))],
        out_specs=[pl.BlockSpec((t_steps, SUBLANES, SSM_CH_BLOCK), lambda c, t: (t, 0, c)), st_spec],
        out_shape=(jax.ShapeDtypeStruct((s, b, w), F32),
                   jax.ShapeDtypeStruct((ncb, SUBLANES, 2 * SSM_LANES), F32)),
        scratch_shapes=[pltpu.VMEM((t_steps * SUBLANES, 2 * SSM_LANES), F32),
                        pltpu.VMEM((SUBLANES, 2 * SSM_LANES), F32)],
        compiler_params=_params("parallel", "arbitrary"),
        name="s5_scan",
    )(u_tm, h0, a_tab, bblk, cblk, d_tab)

    def unlanes(x):
        return x.transpose(1, 0, 2).reshape(b, g, n)

    return y_tm, unlanes(ht[..., :SSM_LANES]), unlanes(ht[..., SSM_LANES:])


def _glu_kernel(y_ref, w_ref, b_ref, z_ref, o_ref):
    y = y_ref[...]
    gate = jnp.dot(y.astype(BF16), w_ref[...], preferred_element_type=F32) + b_ref[...]
    o_ref[...] = (y * jax.nn.sigmoid(gate) * jax.nn.silu(z_ref[...])).astype(o_ref.dtype)


def _glu_gate(y_tm, w_glu, b_glu, proj, z_col0, bsz, s):
    w = w_glu.shape[0]
    tm = _row_tile(s, 512)
    nt = s // tm
    zblk = z_col0 // w
    return pl.pallas_call(
        _glu_kernel,
        grid=(bsz, nt),
        in_specs=[pl.BlockSpec((tm, w), lambda b, t: (t, b)),
                  pl.BlockSpec((w, w), lambda b, t: (0, 0)),
                  pl.BlockSpec((1, w), lambda b, t: (0, 0)),
                  pl.BlockSpec((tm, w), lambda b, t: (b * nt + t, zblk))],
        out_specs=pl.BlockSpec((tm, w), lambda b, t: (b * nt + t, 0)),
        out_shape=jax.ShapeDtypeStruct((bsz * s, w), BF16),
        compiler_params=_params("parallel", "arbitrary"),
        name="glu_gate",
    )(y_tm.reshape(s, bsz * w), w_glu, b_glu.reshape(1, w), proj)


def _head_rms(x, g):
    r = lax.rsqrt(jnp.mean(x * x, axis=-1, keepdims=True) + RMS_EPS)
    return x * r * g


def _qk(q, k):
    return lax.dot_general(q, k, (((1,), (1,)), ((), ())), preferred_element_type=F32)


def _band_prompt_kernel(q_ref, kp_ref, kc_ref, vp_ref, vc_ref, z_ref, bias_ref, qg_ref, kg_ref,
                        o_ref, kn_ref):
    i = pl.program_id(2)
    scale = 1.0 / math.sqrt(HEAD_DIM)
    qn = _head_rms(q_ref[0], qg_ref[...]).astype(BF16)
    kpn = _head_rms(kp_ref[0], kg_ref[...])
    kcn = _head_rms(kc_ref[0], kg_ref[...])
    kn_ref[0] = kcn
    s_p = _qk(qn, kpn.astype(BF16)) * scale + bias_ref[0, :, :BAND_Q_BLOCK]
    s_p = jnp.where(i > 0, s_p, NEG_INF)
    s_c = _qk(qn, kcn.astype(BF16)) * scale + bias_ref[0, :, BAND_Q_BLOCK:]
    m = jnp.maximum(jnp.max(s_p, axis=-1, keepdims=True), jnp.max(s_c, axis=-1, keepdims=True))
    p_p = jnp.exp(s_p - m)
    p_c = jnp.exp(s_c - m)
    l = jnp.sum(p_p, axis=-1, keepdims=True) + jnp.sum(p_c, axis=-1, keepdims=True)
    o = jnp.dot(p_p.astype(BF16), vp_ref[0].astype(BF16), preferred_element_type=F32)
    o = o + jnp.dot(p_c.astype(BF16), vc_ref[0].astype(BF16), preferred_element_type=F32)
    o_ref[0] = ((o / l) * jax.nn.silu(z_ref[0])).astype(o_ref.dtype)


def _band_bias_prompt(rel_bias):
    tq = BAND_Q_BLOCK
    nh = rel_bias.shape[0]
    r = np.arange(tq)
    c = np.arange(2 * tq)
    jb = c[None, :] - (r // CHUNK * CHUNK)[:, None]
    inside = (jb >= 0) & (jb < (BAND_CHUNKS + 1) * CHUNK)
    period = 3 * tq
    m = np.arange(period)
    c_minus_r = np.where(m < 2 * tq, m, m - period)
    rel = np.clip(tq - c_minus_r, -MAX_REL, MAX_REL) + MAX_REL
    vec = rel_bias.astype(F32)[:, rel]
    toep = jnp.tile(vec, (1, tq))[:, :tq * (period - 1)].reshape(nh, tq, period - 1)[:, :, :2 * tq]
    return jnp.where(jnp.asarray(inside)[None], toep, NEG_INF)


def _band_prompt(proj3, q_col0, k_col0, v_col0, z_col0, rel_bias, qn_g, kn_g):
    bsz, s, _ = proj3.shape
    nh = rel_bias.shape[0]
    tq = BAND_Q_BLOCK
    assert s % tq == 0 and s >= tq
    nq = s // tq
    qb, kb, vb, zb = (c // HEAD_DIM for c in (q_col0, k_col0, v_col0, z_col0))
    bias = _band_bias_prompt(rel_bias)
    blk = (1, tq, HEAD_DIM)
    gspec = pl.BlockSpec((1, HEAD_DIM), lambda h, b, i: (0, 0))
    return pl.pallas_call(
        _band_prompt_kernel,
        grid=(nh, bsz, nq),
        in_specs=[pl.BlockSpec(blk, lambda h, b, i: (b, i, qb + h)),
                  pl.BlockSpec(blk, lambda h, b, i: (b, jnp.maximum(i - 1, 0), kb + h)),
                  pl.BlockSpec(blk, lambda h, b, i: (b, i, kb + h)),
                  pl.BlockSpec(blk, lambda h, b, i: (b, jnp.maximum(i - 1, 0), vb + h)),
                  pl.BlockSpec(blk, lambda h, b, i: (b, i, vb + h)),
                  pl.BlockSpec(blk, lambda h, b, i: (b, i, zb + h)),
                  pl.BlockSpec((1, tq, 2 * tq), lambda h, b, i: (h, 0, 0)),
                  gspec, gspec],
        out_specs=[pl.BlockSpec(blk, lambda h, b, i: (b, i, h)),
                   pl.BlockSpec(blk, lambda h, b, i: (b, 0, h))],
        out_shape=(jax.ShapeDtypeStruct((bsz, s, nh * HEAD_DIM), BF16),
                   jax.ShapeDtypeStruct((bsz, tq, nh * HEAD_DIM), F32)),
        compiler_params=_params("parallel", "parallel", "arbitrary"),
        name="band_prompt",
    )(proj3, proj3, proj3, proj3, proj3, proj3, bias, qn_g.reshape(1, HEAD_DIM), kn_g.reshape(1, HEAD_DIM))


def _band_sample_kernel(q_ref, k_ref, v_ref, ck_ref, cv_ref, z_ref, bias_ref, qg_ref, kg_ref,
                        o_ref, kn_ref, *, lc):
    scale = 1.0 / math.sqrt(HEAD_DIM)
    n = q_ref.shape[1]
    qn = _head_rms(q_ref[0], qg_ref[...]).astype(BF16)
    kn = _head_rms(k_ref[0], kg_ref[...])
    kn_ref[0] = kn
    zpad = jnp.zeros((LANES - n, HEAD_DIM), F32)
    kn = jnp.concatenate([kn, zpad], axis=0)
    v_new = jnp.concatenate([v_ref[0], zpad], axis=0)
    s_p = _qk(qn, ck_ref[0].astype(BF16)) * scale + bias_ref[0, :, :lc]
    s_c = _qk(qn, kn.astype(BF16)) * scale + bias_ref[0, :, lc:]
    m = jnp.maximum(jnp.max(s_p, axis=-1, keepdims=True), jnp.max(s_c, axis=-1, keepdims=True))
    p_p = jnp.exp(s_p - m)
    p_c = jnp.exp(s_c - m)
    l = jnp.sum(p_p, axis=-1, keepdims=True) + jnp.sum(p_c, axis=-1, keepdims=True)
    o = jnp.dot(p_p.astype(BF16), cv_ref[0].astype(BF16), preferred_element_type=F32)
    o = o + jnp.dot(p_c.astype(BF16), v_new.astype(BF16), preferred_element_type=F32)
    o_ref[0] = ((o / l) * jax.nn.silu(z_ref[0])).astype(o_ref.dtype)


def _band_sample(proj3, q_col0, k_col0, v_col0, z_col0, cache_k, cache_v, rel_bias, qn_g, kn_g):
    bsz, n, _ = proj3.shape
    lc = cache_k.shape[1]
    nh = rel_bias.shape[0]
    qb, kb, vb, zb = (c // HEAD_DIM for c in (q_col0, k_col0, v_col0, z_col0))
    rel = np.clip((np.arange(n) + lc)[:, None] - np.arange(lc + n)[None, :], -MAX_REL, MAX_REL) + MAX_REL
    bias = jnp.take(rel_bias.astype(F32), jnp.asarray(rel, jnp.int32), axis=1)
    assert n <= LANES
    bias = jnp.pad(bias, ((0, 0), (0, 0), (0, LANES - n)), constant_values=NEG_INF)
    blk = (1, n, HEAD_DIM)
    cblk = (1, lc, HEAD_DIM)
    gspec = pl.BlockSpec((1, HEAD_DIM), lambda b, h: (0, 0))
    return pl.pallas_call(
        functools.partial(_band_sample_kernel, lc=lc),
        grid=(bsz, nh),
        in_specs=[pl.BlockSpec(blk, lambda b, h: (b, 0, qb + h)),
                  pl.BlockSpec(blk, lambda b, h: (b, 0, kb + h)),
                  pl.BlockSpec(blk, lambda b, h: (b, 0, vb + h)),
                  pl.BlockSpec(cblk, lambda b, h: (b, 0, h)),
                  pl.BlockSpec(cblk, lambda b, h: (b, 0, h)),
                  pl.BlockSpec(blk, lambda b, h: (b, 0, zb + h)),
                  pl.BlockSpec((1, n, lc + LANES), lambda b, h: (h, 0, 0)),
                  gspec, gspec],
        out_specs=[pl.BlockSpec(blk, lambda b, h: (b, 0, h)),
                   pl.BlockSpec(blk, lambda b, h: (b, 0, h))],
        out_shape=(jax.ShapeDtypeStruct((bsz, n, nh * HEAD_DIM), BF16),
                   jax.ShapeDtypeStruct((bsz, n, nh * HEAD_DIM), F32)),
        compiler_params=_params("parallel", "parallel"),
        name="band_sample",
    )(proj3, proj3, proj3, cache_k, cache_v, proj3, bias, qn_g.reshape(1, HEAD_DIM), kn_g.reshape(1, HEAD_DIM))


def _suffix_matrix():
    kb = SB_K_BLOCK
    j = lax.broadcasted_iota(jnp.int32, (2 * kb, 2 * kb), 0) % kb
    c = lax.broadcasted_iota(jnp.int32, (2 * kb, 2 * kb), 1)
    return jnp.where((c >= kb) | (j >= c), -1.0, 0.0).astype(BF16)


def _sb_groups(q, groups, carry, acc, umat):
    scale = 1.0 / math.sqrt(HEAD_DIM)
    kb = SB_K_BLOCK
    staged = []
    for k, v, causal in groups:
        z = _qk(q, k.astype(BF16)) * scale
        fail = jnp.maximum(z, 0.0) + jnp.log(1.0 + jnp.exp(-jnp.abs(z)))
        if causal is not None:
            fail = jnp.where(causal, fail, 0.0)
        hi = fail.astype(BF16)
        lo = (fail - hi.astype(F32)).astype(BF16)
        parts = []
        for b in reversed(range(SB_GROUP // kb)):
            cols = slice(b * kb, (b + 1) * kb)
            sums = jnp.dot(jnp.concatenate([hi[:, cols], lo[:, cols]], axis=1), umat, preferred_element_type=F32)
            parts.append((b, z[:, cols] + sums[:, :kb], sums[:, kb:]))
        staged.append((parts, causal))
    ws = []
    for parts, causal in staged:
        log_w = [None] * len(parts)
        for b, part, total in parts:
            log_w[b] = part + carry
            carry = carry + total
        w = jnp.exp(jnp.concatenate(log_w, axis=1))
        if causal is not None:
            w = jnp.where(causal, w, 0.0)
        ws.append(w.astype(BF16))
    vs = [v.astype(BF16) for _, v, _ in groups]
    w_all = ws[0] if len(ws) == 1 else jnp.concatenate(ws, axis=1)
    v_all = vs[0] if len(vs) == 1 else jnp.concatenate(vs, axis=0)
    return carry, acc + jnp.dot(w_all, v_all, preferred_element_type=F32)


def _sb_prompt_kernel(q_ref, k_ref, v_ref, z_ref, o_ref):
    qi = pl.program_id(2)
    tq = SB_Q_BLOCK
    q = q_ref[0].astype(BF16)
    umat = _suffix_matrix()
    q0 = qi * tq
    qpos = q0 + lax.broadcasted_iota(jnp.int32, (tq, SB_GROUP), 0)
    col = lax.broadcasted_iota(jnp.int32, (tq, SB_GROUP), 1)
    carry = jnp.zeros((tq, SB_K_BLOCK), F32)
    acc = jnp.zeros((tq, HEAD_DIM), F32)

    def load(k0, causal):
        k0 = pl.multiple_of(k0, SB_GROUP)
        return k_ref[0, pl.ds(k0, SB_GROUP), :], v_ref[0, pl.ds(k0, SB_GROUP), :], causal

    n_diag = tq // SB_GROUP
    assert n_diag % SB_PROMPT_UNROLL == 0
    diag = [load(q0 + d * SB_GROUP, (q0 + d * SB_GROUP + col) < qpos) for d in reversed(range(n_diag))]
    carry, acc = _sb_groups(q, diag, carry, acc, umat)

    def body(j, ca):
        groups = [load(q0 - (j * SB_PROMPT_UNROLL + u + 1) * SB_GROUP, None) for u in range(SB_PROMPT_UNROLL)]
        return _sb_groups(q, groups, ca[0], ca[1], umat)

    carry, acc = lax.fori_loop(0, qi * (n_diag // SB_PROMPT_UNROLL), body, (carry, acc))
    o_ref[0] = (acc * jax.nn.silu(z_ref[0])).astype(o_ref.dtype)


def _sb_prompt(proj3, q_col0, k_col0, v_col0, z_col0, nh):
    bsz, s, _ = proj3.shape
    tq = _row_tile(s, SB_Q_BLOCK)
    assert tq == SB_Q_BLOCK
    qb, kb, vb, zb = (c // HEAD_DIM for c in (q_col0, k_col0, v_col0, z_col0))
    blk = (1, tq, HEAD_DIM)
    full = (1, s, HEAD_DIM)
    return pl.pallas_call(
        _sb_prompt_kernel,
        grid=(bsz, nh, s // tq),
        in_specs=[pl.BlockSpec(blk, lambda b, h, i: (b, i, qb + h)),
                  pl.BlockSpec(full, lambda b, h, i: (b, 0, kb + h)),
                  pl.BlockSpec(full, lambda b, h, i: (b, 0, vb + h)),
                  pl.BlockSpec(blk, lambda b, h, i: (b, i, zb + h))],
        out_specs=pl.BlockSpec(blk, lambda b, h, i: (b, i, h)),
        out_shape=jax.ShapeDtypeStruct((bsz, s, nh * HEAD_DIM), BF16),
        compiler_params=_params("parallel", "parallel", "arbitrary"),
        name="sb_prompt",
    )(proj3, proj3, proj3, proj3)


def _sb_sample_kernel(q_ref, k_ref, v_ref, ck_ref, cv_ref, z_ref, o_ref, *, past_len):
    n = q_ref.shape[1]
    q = q_ref[0].astype(BF16)
    umat = _suffix_matrix()
    row = lax.broadcasted_iota(jnp.int32, (n, SB_GROUP), 0)
    col = lax.broadcasted_iota(jnp.int32, (n, SB_GROUP), 1)
    carry = jnp.zeros((n, SB_K_BLOCK), F32)
    acc = jnp.zeros((n, HEAD_DIM), F32)
    carry, acc = _sb_groups(q, [(k_ref[0], v_ref[0], col < row)], carry, acc, umat)
    n_past = past_len // SB_GROUP
    unroll = math.gcd(n_past, SB_SAMPLE_UNROLL)

    def body(j, ca):
        groups = []
        for u in range(unroll):
            k0 = pl.multiple_of(past_len - (j * unroll + u + 1) * SB_GROUP, SB_GROUP)
            groups.append((ck_ref[0, pl.ds(k0, SB_GROUP), :], cv_ref[0, pl.ds(k0, SB_GROUP), :], None))
        return _sb_groups(q, groups, ca[0], ca[1], umat)

    carry, acc = lax.fori_loop(0, n_past // unroll, body, (carry, acc))
    o_ref[0] = (acc * jax.nn.silu(z_ref[0])).astype(o_ref.dtype)


def _sb_sample(proj3, q_col0, z_col0, k_new, v_new, cache_k, cache_v, nh):
    bsz, n, _ = proj3.shape
    past_len = cache_k.shape[1]
    assert past_len % SB_GROUP == 0 and n <= SB_GROUP
    qb, zb = q_col0 // HEAD_DIM, z_col0 // HEAD_DIM
    pad = ((0, 0), (0, SB_GROUP - n), (0, 0))
    k_pad = jnp.pad(k_new, pad)
    v_pad = jnp.pad(v_new, pad)
    blk = (1, n, HEAD_DIM)
    own = (1, SB_GROUP, HEAD_DIM)
    past = (1, past_len, HEAD_DIM)
    return pl.pallas_call(
        functools.partial(_sb_sample_kernel, past_len=past_len),
        grid=(bsz, nh),
        in_specs=[pl.BlockSpec(blk, lambda b, h: (b, 0, qb + h)),
                  pl.BlockSpec(own, lambda b, h: (b, 0, h)),
                  pl.BlockSpec(own, lambda b, h: (b, 0, h)),
                  pl.BlockSpec(past, lambda b, h: (b, 0, h)),
                  pl.BlockSpec(past, lambda b, h: (b, 0, h)),
                  pl.BlockSpec(blk, lambda b, h: (b, 0, zb + h))],
        out_specs=pl.BlockSpec(blk, lambda b, h: (b, 0, h)),
        out_shape=jax.ShapeDtypeStruct((bsz, n, nh * HEAD_DIM), BF16),
        compiler_params=_params("parallel", "parallel"),
        name="sb_sample",
    )(proj3, k_pad, v_pad, cache_k, cache_v, proj3)


def _layer(x, w, cache):
    (norm_g, w_in, ssm, w_glu, b_glu, qn_g, kn_g, rel_bias, w_br_a, w_br_b, w_br_c, gate_b, w_out) = w
    ab_re, ab_im, bbt_re, bbt_im, c_re, c_im, d_skip = ssm
    bsz, s, d = x.shape
    mw = w_glu.shape[0]
    nh = mw // HEAD_DIM
    g = ab_re.shape[0]
    m = bsz * s
    x2 = x.reshape(m, d)
    hx = _rmsnorm(x2, norm_g)
    proj = _matmul(hx, w_in)
    n_in = proj.shape[1]
    proj3 = proj.reshape(bsz, s, n_in)
    col = lambda i: i * mw
    u_tm = proj3[:, :, :mw].transpose(1, 0, 2)
    if cache is None:
        h0_re = jnp.zeros((bsz, g, SSM_STATE), F32)
        h0_im = h0_re
    else:
        h0_re, h0_im = cache[4], cache[5]
    y_tm, ht_re, ht_im = _s5_branch(u_tm, h0_re, h0_im, ab_re, ab_im, bbt_re, bbt_im, c_re, c_im, d_skip)
    lhs_a = _glu_gate(y_tm, w_glu, b_glu, proj, col(1), bsz, s)
    k_c = proj3[:, :, col(7):col(8)]
    v_c = proj3[:, :, col(8):col(9)]
    v_b = proj3[:, :, col(4):col(5)]
    if cache is None:
        lhs_b, band_k = _band_prompt(proj3, col(2), col(3), col(4), col(5), rel_bias, qn_g, kn_g)
        keep = min(BAND_LEN, s)
        band_v = v_b[:, s - keep:]
        lhs_c = _sb_prompt(proj3, col(6), col(7), col(8), col(9), nh)
    else:
        sb_k, sb_v, bk, bv = (t.reshape(bsz, t.shape[1], mw) for t in cache[:4])
        lhs_b, band_k = _band_sample(proj3, col(2), col(3), col(4), col(5), bk, bv, rel_bias, qn_g, kn_g)
        band_v = v_b
        lhs_c = _sb_sample(proj3, col(6), col(9), k_c, v_c, sb_k, sb_v, nh)
    mixed = _branch_merge(lhs_a, lhs_b.reshape(m, mw), lhs_c.reshape(m, mw), w_br_a, w_br_b, w_br_c,
                          proj, gate_b, col(10))
    y = _matmul_residual(mixed, w_out, x2).reshape(bsz, s, d)
    heads = lambda t: t.reshape(bsz, t.shape[1], nh, HEAD_DIM)
    return y, (heads(k_c), heads(v_c), heads(band_k), heads(band_v), ht_re, ht_im)


def kernel(x_prompt, x_sample, cache_sb_k, cache_sb_v, cache_band_k, cache_band_v, state_ssm_re, state_ssm_im, norm_g, w_in, ssm_a_re, ssm_a_im, ssm_log_dt, ssm_b_re, ssm_b_im, ssm_c_re, ssm_c_im, ssm_d, w_glu, b_glu, q_norm_g, k_norm_g, rel_bias, w_br_a, w_br_b, w_br_c, gate_b, w_out):
    depth = w_in.shape[0]
    yp, ys = x_prompt, x_sample
    p_states, s_states = [], []
    for l in range(depth):
        ssm = _discretize(ssm_a_re[l], ssm_a_im[l], ssm_log_dt[l], ssm_b_re[l], ssm_b_im[l])
        w = (norm_g[l], w_in[l].astype(BF16), (*ssm, ssm_c_re[l], ssm_c_im[l], ssm_d[l]),
             w_glu[l].astype(BF16), b_glu[l], q_norm_g[l], k_norm_g[l], rel_bias[l],
             w_br_a[l].astype(BF16), w_br_b[l].astype(BF16), w_br_c[l].astype(BF16), gate_b[l],
             w_out[l].astype(BF16))
        yp, sp = _layer(yp, w, None)
        ys, ss = _layer(ys, w, (cache_sb_k[l], cache_sb_v[l], cache_band_k[l], cache_band_v[l],
                                state_ssm_re[l], state_ssm_im[l]))
        p_states.append(sp)
        s_states.append(ss)
    stk = lambda states, i: jnp.stack([st[i] for st in states], axis=0)
    return (yp, ys,
            *(stk(p_states, i) for i in range(6)),
            *(stk(s_states, i) for i in range(6)))
```

```python
import functools
import math

import jax
import jax.numpy as jnp
import numpy as np
from jax import lax
from jax.experimental import pallas as pl
from jax.experimental.pallas import tpu as pltpu

F32 = jnp.float32
BF16 = jnp.bfloat16

CHUNK = 64
HEAD_DIM = 128
SSM_GROUP = 16
SSM_STATE = 64
BAND_CHUNKS = 8
BAND_LEN = BAND_CHUNKS * CHUNK
MAX_REL = 128
N_BRANCH = 3
RMS_EPS = 1e-6
NEG_INF = -1e30

LANES = 128
SUBLANES = 8
VMEM_LIMIT_BYTES = 56 * 1024 * 1024
MM_TILE = 1024
SSM_CH_BLOCK = 128
SSM_GROUPS_PER_BLOCK = SSM_CH_BLOCK // SSM_GROUP
SSM_LANES = SSM_GROUPS_PER_BLOCK * SSM_STATE
SB_Q_BLOCK = 512
SB_K_BLOCK = 128
SB_GROUP = 2 * SB_K_BLOCK
SB_PROMPT_UNROLL = 2
SB_SAMPLE_UNROLL = 4
BAND_Q_BLOCK = 2 * BAND_LEN
BAND_Q_SUB = 2 * CHUNK
BAND_WINDOW = BAND_LEN + BAND_Q_SUB


def _params(*sem):
    return pltpu.CompilerParams(dimension_semantics=sem, vmem_limit_bytes=VMEM_LIMIT_BYTES)


def _row_tile(m, target, align=SUBLANES):
    if m <= target:
        return m
    for t in range(target - target % align, 0, -align):
        if m % t == 0:
            return t
    raise ValueError((m, target, align))


def _rmsnorm_kernel(x_ref, g_ref, o_ref):
    x = x_ref[...]
    r = lax.rsqrt(jnp.mean(x * x, axis=-1, keepdims=True) + RMS_EPS)
    o_ref[...] = (x * r * g_ref[...]).astype(o_ref.dtype)


def _rmsnorm(x, g):
    m, d = x.shape
    tm = _row_tile(m, 512)
    return pl.pallas_call(
        _rmsnorm_kernel,
        grid=(m // tm,),
        in_specs=[pl.BlockSpec((tm, d), lambda i: (i, 0)),
                  pl.BlockSpec((1, d), lambda i: (0, 0))],
        out_specs=pl.BlockSpec((tm, d), lambda i: (i, 0)),
        out_shape=jax.ShapeDtypeStruct((m, d), BF16),
        compiler_params=_params("parallel"),
        name="rmsnorm",
    )(x, g.reshape(1, d))


def _mm_kernel(x_ref, w_ref, o_ref):
    o_ref[...] = jnp.dot(x_ref[...], w_ref[...], preferred_element_type=F32).astype(o_ref.dtype)


def _matmul(x, w, l, out_dtype=F32):
    m, k = x.shape
    _, _, n = w.shape
    tm = _row_tile(m, MM_TILE)
    tn = _row_tile(n, MM_TILE, LANES)
    return pl.pallas_call(
        _mm_kernel,
        grid=(m // tm, n // tn),
        in_specs=[pl.BlockSpec((tm, k), lambda i, j: (i, 0)),
                  pl.BlockSpec((None, k, tn), lambda i, j: (l, 0, j))],
        out_specs=pl.BlockSpec((tm, tn), lambda i, j: (i, j)),
        out_shape=jax.ShapeDtypeStruct((m, n), out_dtype),
        compiler_params=_params("parallel", "arbitrary"),
        name="in_proj",
    )(x, w)


def _mm_res_kernel(x_ref, w_ref, r_ref, o_ref):
    o_ref[...] = r_ref[...] + jnp.dot(x_ref[...], w_ref[...], preferred_element_type=F32)


def _matmul_residual(x, w, l, res):
    m, k = x.shape
    _, _, n = w.shape
    tm = _row_tile(m, MM_TILE)
    tn = _row_tile(n, MM_TILE, LANES)
    return pl.pallas_call(
        _mm_res_kernel,
        grid=(m // tm, n // tn),
        in_specs=[pl.BlockSpec((tm, k), lambda i, j: (i, 0)),
                  pl.BlockSpec((None, k, tn), lambda i, j: (l, 0, j)),
                  pl.BlockSpec((tm, tn), lambda i, j: (i, j))],
        out_specs=pl.BlockSpec((tm, tn), lambda i, j: (i, j)),
        out_shape=jax.ShapeDtypeStruct((m, n), F32),
        compiler_params=_params("parallel", "arbitrary"),
        name="out_proj",
    )(x, w, res)


def _branch_kernel(la_ref, lb_ref, lc_ref, wa_ref, wb_ref, wc_ref,
                   ga_ref, gb_ref, gc_ref, ba_ref, bb_ref, bc_ref, o_ref):
    def one(l_ref, w_ref, g_ref, b_ref):
        o = jnp.dot(l_ref[...], w_ref[...], preferred_element_type=F32)
        return jax.nn.sigmoid(g_ref[...] + b_ref[...]) * o
    mixed = one(la_ref, wa_ref, ga_ref, ba_ref) + one(lb_ref, wb_ref, gb_ref, bb_ref)
    mixed = mixed + one(lc_ref, wc_ref, gc_ref, bc_ref)
    o_ref[...] = mixed.astype(o_ref.dtype)


def _branch_merge(lhs_a, lhs_b, lhs_c, w_a, w_b, w_c, l, proj, gate_b, g_col0):
    m, kw = lhs_a.shape
    d = w_a.shape[2]
    tm = _row_tile(m, 512)
    tn = _row_tile(math.gcd(d, g_col0), MM_TILE, LANES)
    assert g_col0 % tn == 0 and d % tn == 0
    gblk = g_col0 // tn
    nblk = d // tn
    lspec = pl.BlockSpec((tm, kw), lambda j, i: (i, 0))
    wspec = pl.BlockSpec((None, kw, tn), lambda j, i: (l, 0, j))
    gspecs = [pl.BlockSpec((tm, tn), functools.partial(lambda j, i, br: (i, gblk + br * nblk + j), br=br))
              for br in range(N_BRANCH)]
    bspecs = [pl.BlockSpec((1, tn), functools.partial(lambda j, i, br: (0, br * nblk + j), br=br))
              for br in range(N_BRANCH)]
    gb = gate_b.reshape(1, N_BRANCH * d)
    return pl.pallas_call(
        _branch_kernel,
        grid=(d // tn, m // tm),
        in_specs=[lspec, lspec, lspec, wspec, wspec, wspec, *gspecs, *bspecs],
        out_specs=pl.BlockSpec((tm, tn), lambda j, i: (i, j)),
        out_shape=jax.ShapeDtypeStruct((m, d), BF16),
        compiler_params=_params("parallel", "arbitrary"),
        name="branch_merge",
    )(lhs_a, lhs_b, lhs_c, w_a, w_b, w_c, proj, proj, proj, gb, gb, gb)


def _discretize_kernel(are_ref, aim_ref, ldt_ref, bre_ref, bim_ref,
                       abre_ref, abim_ref, bbre_ref, bbim_ref):
    a_re = are_ref[...]
    a_im = aim_ref[...]
    dt = jnp.exp(ldt_ref[...])
    mag = jnp.exp(a_re * dt)
    ab_re = mag * jnp.cos(a_im * dt)
    ab_im = mag * jnp.sin(a_im * dt)
    den = a_re * a_re + a_im * a_im
    nr = ab_re - 1.0
    k_re = (nr * a_re + ab_im * a_im) / den
    k_im = (ab_im * a_re - nr * a_im) / den
    abre_ref[...] = ab_re
    abim_ref[...] = ab_im
    b_re = bre_ref[...]
    b_im = bim_ref[...]
    bbre_ref[...] = k_re[:, None, :] * b_re - k_im[:, None, :] * b_im
    bbim_ref[...] = k_re[:, None, :] * b_im + k_im[:, None, :] * b_re


def _discretize(a_re, a_im, log_dt, b_re, b_im):
    g, n = a_re.shape
    p = b_re.shape[-1]
    gn = jax.ShapeDtypeStruct((g, n), F32)
    gpn = jax.ShapeDtypeStruct((g, p, n), F32)
    return pl.pallas_call(
        _discretize_kernel,
        out_shape=(gn, gn, gpn, gpn),
        name="s5_discretize",
    )(a_re, a_im, log_dt.reshape(g, 1), b_re.transpose(0, 2, 1), b_im.transpose(0, 2, 1))


def _block_diag(x):
    ncb, gl, r, c = x.shape
    eye = jnp.eye(gl, dtype=x.dtype)
    out = x[:, :, :, None, :] * eye[None, :, None, :, None]
    return out.reshape(ncb, gl * r, gl * c)


def _s5_kernel(u_ref, h0_ref, a_ref, bblk_ref, cblk_ref, d_ref, y_ref, ht_ref, hb_ref, h_ref, *, t_steps):
    t_idx = pl.program_id(1)

    @pl.when(t_idx == 0)
    def _():
        h_ref[...] = h0_ref[0]

    rows = t_steps * SUBLANES
    u = u_ref[...].reshape(rows, SSM_CH_BLOCK)
    hb_ref[...] = jnp.dot(u.astype(BF16), bblk_ref[0], preferred_element_type=F32)
    a_re = a_ref[0, :, :SSM_LANES]
    a_im = a_ref[0, :, SSM_LANES:]

    def step(t, carry):
        h_re, h_im = carry
        row = pl.multiple_of(t * SUBLANES, SUBLANES)
        b_re = hb_ref[pl.ds(row, SUBLANES), :SSM_LANES]
        b_im = hb_ref[pl.ds(row, SUBLANES), SSM_LANES:]
        n_re = a_re * h_re - a_im * h_im + b_re
        n_im = a_re * h_im + a_im * h_re + b_im
        hb_ref[pl.ds(row, SUBLANES), :SSM_LANES] = n_re
        hb_ref[pl.ds(row, SUBLANES), SSM_LANES:] = n_im
        return n_re, n_im

    h_re, h_im = lax.fori_loop(0, t_steps, step, (h_ref[:, :SSM_LANES], h_ref[:, SSM_LANES:]),
                               unroll=min(8, t_steps))
    h_ref[:, :SSM_LANES] = h_re
    h_ref[:, SSM_LANES:] = h_im
    ht_ref[0] = h_ref[...]
    y = jnp.dot(hb_ref[...].astype(BF16), cblk_ref[0], preferred_element_type=F32)
    y = y + d_ref[0] * u
    y_ref[...] = jax.nn.gelu(y).reshape(t_steps, SUBLANES, SSM_CH_BLOCK)


def _s5_branch(u_tm, h0_re, h0_im, ab_re, ab_im, bbt_re, bbt_im, c_re, c_im, d_skip):
    s, b, w = u_tm.shape
    assert b == SUBLANES, "the scan keeps one stream per sublane"
    g, n = ab_re.shape
    assert n == SSM_STATE and w == g * SSM_GROUP and w % SSM_CH_BLOCK == 0
    ncb = w // SSM_CH_BLOCK
    gl = SSM_GROUPS_PER_BLOCK
    t_steps = _row_tile(s, 128)

    def lanes(x):
        r = x.shape[0]
        return x.reshape(r, ncb, gl * n).transpose(1, 0, 2)

    h0 = jnp.concatenate([lanes(h0_re), lanes(h0_im)], axis=-1)
    a_tab = jnp.concatenate([lanes(jnp.broadcast_to(ab_re[None], (SUBLANES, g, n))),
                             lanes(jnp.broadcast_to(ab_im[None], (SUBLANES, g, n)))], axis=-1)
    bblk = jnp.concatenate([_block_diag(bbt_re.reshape(ncb, gl, SSM_GROUP, n)),
                            _block_diag(bbt_im.reshape(ncb, gl, SSM_GROUP, n))], axis=-1).astype(BF16)
    cblk = jnp.concatenate([_block_diag(c_re.reshape(ncb, gl, SSM_GROUP, n).transpose(0, 1, 3, 2)),
                            _block_diag(-c_im.reshape(ncb, gl, SSM_GROUP, n).transpose(0, 1, 3, 2))],
                           axis=1).astype(BF16)
    d_tab = d_skip.reshape(ncb, 1, SSM_CH_BLOCK)
    st_spec = pl.BlockSpec((1, SUBLANES, 2 * SSM_LANES), lambda c, t: (c, 0, 0))
    y_tm, ht = pl.pallas_call(
        functools.partial(_s5_kernel, t_steps=t_steps),
        grid=(ncb, s // t_steps),
        in_specs=[pl.BlockSpec((t_steps, SUBLANES, SSM_CH_BLOCK), lambda c, t: (t, 0, c)),
                  st_spec, st_spec,
                  pl.BlockSpec((1, SSM_CH_BLOCK, 2 * SSM_LANES), lambda c, t: (c, 0, 0)),
                  pl.BlockSpec((1, 2 * SSM_LANES, SSM_CH_BLOCK), lambda c, t: (c, 0, 0)),
                  pl.BlockSpec((1, 1, SSM_CH_BLOCK), lambda c, t: (c, 0, 0))],
        out_specs=[pl.BlockSpec((t_steps, SUBLANES, SSM_CH_BLOCK), lambda c, t: (t, 0, c)), st_spec],
        out_shape=(jax.ShapeDtypeStruct((s, b, w), F32),
                   jax.ShapeDtypeStruct((ncb, SUBLANES, 2 * SSM_LANES), F32)),
        scratch_shapes=[pltpu.VMEM((t_steps * SUBLANES, 2 * SSM_LANES), F32),
                        pltpu.VMEM((SUBLANES, 2 * SSM_LANES), F32)],
        compiler_params=_params("parallel", "arbitrary"),
        name="s5_scan",
    )(u_tm, h0, a_tab, bblk, cblk, d_tab)

    def unlanes(x):
        return x.transpose(1, 0, 2).reshape(b, g, n)

    return y_tm, unlanes(ht[..., :SSM_LANES]), unlanes(ht[..., SSM_LANES:])


def _glu_kernel(y_ref, w_ref, b_ref, z_ref, o_ref):
    y = y_ref[...]
    gate = jnp.dot(y.astype(BF16), w_ref[...], preferred_element_type=F32) + b_ref[...]
    o_ref[...] = (y * jax.nn.sigmoid(gate) * jax.nn.silu(z_ref[...])).astype(o_ref.dtype)


def _glu_gate(y_tm, w_glu, l, b_glu, proj, z_col0, bsz, s):
    w = w_glu.shape[1]
    tm = _row_tile(s, 512)
    nt = s // tm
    zblk = z_col0 // w
    return pl.pallas_call(
        _glu_kernel,
        grid=(bsz, nt),
        in_specs=[pl.BlockSpec((tm, w), lambda b, t: (t, b)),
                  pl.BlockSpec((None, w, w), lambda b, t: (l, 0, 0)),
                  pl.BlockSpec((1, w), lambda b, t: (0, 0)),
                  pl.BlockSpec((tm, w), lambda b, t: (b * nt + t, zblk))],
        out_specs=pl.BlockSpec((tm, w), lambda b, t: (b * nt + t, 0)),
        out_shape=jax.ShapeDtypeStruct((bsz * s, w), BF16),
        compiler_params=_params("parallel", "arbitrary"),
        name="glu_gate",
    )(y_tm.reshape(s, bsz * w), w_glu, b_glu.reshape(1, w), proj)


def _head_rms(x, g):
    r = lax.rsqrt(jnp.mean(x * x, axis=-1, keepdims=True) + RMS_EPS)
    return x * r * g


def _qk(q, k):
    return lax.dot_general(q, k, (((1,), (1,)), ((), ())), preferred_element_type=F32)


def _band_prompt_kernel(q_ref, kp_ref, kc_ref, vp_ref, vc_ref, z_ref, bias_ref, qg_ref, kg_ref,
                        o_ref, kn_ref):
    i = pl.program_id(2)
    scale = 1.0 / math.sqrt(HEAD_DIM)
    qn = _head_rms(q_ref[0], qg_ref[...]).astype(BF16)
    kpn = _head_rms(kp_ref[0], kg_ref[...])
    kcn = _head_rms(kc_ref[0], kg_ref[...])
    tq = q_ref.shape[1]
    kn_ref[0] = kcn[tq - BAND_LEN:]
    k_all = jnp.concatenate([kpn.astype(BF16), kcn.astype(BF16)], axis=0)
    v_all = jnp.concatenate([vp_ref[0].astype(BF16), vc_ref[0].astype(BF16)], axis=0)
    bias = bias_ref[0]
    col = lax.broadcasted_iota(jnp.int32, (BAND_Q_SUB, BAND_WINDOW), 1)
    z = z_ref[0]
    for p in range(tq // BAND_Q_SUB):
        rows = slice(p * BAND_Q_SUB, (p + 1) * BAND_Q_SUB)
        win = slice(p * BAND_Q_SUB, p * BAND_Q_SUB + BAND_WINDOW)
        s = _qk(qn[rows], k_all[win]) * scale + bias
        if p * BAND_Q_SUB < BAND_LEN:
            n_missing = jnp.where(i == 0, BAND_LEN - p * BAND_Q_SUB, 0)
            s = jnp.where(col < n_missing, NEG_INF, s)
        m = jnp.max(s, axis=-1, keepdims=True)
        pr = jnp.exp(s - m)
        l = jnp.sum(pr, axis=-1, keepdims=True)
        o = jnp.dot(pr.astype(BF16), v_all[win], preferred_element_type=F32)
        o_ref[0, rows, :] = ((o / l) * jax.nn.silu(z[rows])).astype(o_ref.dtype)


def _static_take(tab, idx):
    nh = tab.shape[0]
    idx = np.asarray(idx)
    pieces = []
    i = 0
    while i < len(idx):
        step = int(idx[i + 1] - idx[i]) if i + 1 < len(idx) else 0
        if step not in (-1, 0, 1):
            step = 0
            j = i + 1
        else:
            j = i + 1
            while j < len(idx) and idx[j] - idx[j - 1] == step:
                j += 1
        n = j - i
        a = int(idx[i])
        if step == 0:
            pieces.append(jnp.broadcast_to(tab[:, a:a + 1], (nh, n)))
        elif step == 1:
            pieces.append(tab[:, a:a + n])
        else:
            pieces.append(tab[:, a - n + 1:a + 1][:, ::-1])
        i = j
    return jnp.concatenate(pieces, axis=1)


def _toeplitz_bias(rel_bias, rows, cols, offset):
    nh = rel_bias.shape[0]
    period = rows + cols
    m = np.arange(period)
    c_minus_r = np.where(m < cols, m, m - period)
    rel = np.clip(offset - c_minus_r, -MAX_REL, MAX_REL) + MAX_REL
    vec = _static_take(rel_bias.astype(F32), rel)
    return jnp.tile(vec, (1, rows))[:, :rows * (period - 1)].reshape(nh, rows, period - 1)[:, :, :cols]


def _band_bias_prompt(rel_bias):
    r = np.arange(BAND_Q_SUB)
    c = np.arange(BAND_WINDOW)
    jb = c[None, :] - (r // CHUNK * CHUNK)[:, None]
    inside = (jb >= 0) & (jb < (BAND_CHUNKS + 1) * CHUNK)
    toep = _toeplitz_bias(rel_bias, BAND_Q_SUB, BAND_WINDOW, BAND_LEN)
    return jnp.where(jnp.asarray(inside)[None], toep, NEG_INF)


def _band_prompt(proj3, q_col0, k_col0, v_col0, z_col0, rel_bias, qn_g, kn_g):
    bsz, s, _ = proj3.shape
    nh = rel_bias.shape[0]
    tq = _row_tile(s, BAND_Q_BLOCK, BAND_LEN)
    assert s % tq == 0 and tq % BAND_LEN == 0
    nq = s // tq
    per = tq // BAND_LEN
    qb, kb, vb, zb = (c // HEAD_DIM for c in (q_col0, k_col0, v_col0, z_col0))
    bias = _band_bias_prompt(rel_bias)
    blk = (1, tq, HEAD_DIM)
    prev = (1, BAND_LEN, HEAD_DIM)
    gspec = pl.BlockSpec((1, HEAD_DIM), lambda h, b, i: (0, 0))
    return pl.pallas_call(
        _band_prompt_kernel,
        grid=(nh, bsz, nq),
        in_specs=[pl.BlockSpec(blk, lambda h, b, i: (b, i, qb + h)),
                  pl.BlockSpec(prev, lambda h, b, i: (b, jnp.maximum(i * per - 1, 0), kb + h)),
                  pl.BlockSpec(blk, lambda h, b, i: (b, i, kb + h)),
                  pl.BlockSpec(prev, lambda h, b, i: (b, jnp.maximum(i * per - 1, 0), vb + h)),
                  pl.BlockSpec(blk, lambda h, b, i: (b, i, vb + h)),
                  pl.BlockSpec(blk, lambda h, b, i: (b, i, zb + h)),
                  pl.BlockSpec((1, BAND_Q_SUB, BAND_WINDOW), lambda h, b, i: (h, 0, 0)),
                  gspec, gspec],
        out_specs=[pl.BlockSpec(blk, lambda h, b, i: (b, i, h)),
                   pl.BlockSpec(prev, lambda h, b, i: (b, 0, h))],
        out_shape=(jax.ShapeDtypeStruct((bsz, s, nh * HEAD_DIM), BF16),
                   jax.ShapeDtypeStruct((bsz, BAND_LEN, nh * HEAD_DIM), F32)),
        compiler_params=_params("parallel", "parallel", "arbitrary"),
        name="band_prompt",
    )(proj3, proj3, proj3, proj3, proj3, proj3, bias, qn_g.reshape(1, HEAD_DIM), kn_g.reshape(1, HEAD_DIM))


def _band_sample_kernel(q_ref, k_ref, v_ref, ck_ref, cv_ref, z_ref, bias_ref, qg_ref, kg_ref,
                        o_ref, kn_ref, *, lc):
    scale = 1.0 / math.sqrt(HEAD_DIM)
    n = q_ref.shape[1]
    qn = _head_rms(q_ref[0], qg_ref[...]).astype(BF16)
    kn = _head_rms(k_ref[0], kg_ref[...])
    kn_ref[0] = kn
    zpad = jnp.zeros((LANES - n, HEAD_DIM), F32)
    kn = jnp.concatenate([kn, zpad], axis=0)
    v_new = jnp.concatenate([v_ref[0], zpad], axis=0)
    s_p = _qk(qn, ck_ref[0].astype(BF16)) * scale + bias_ref[0, :, :lc]
    s_c = _qk(qn, kn.astype(BF16)) * scale + bias_ref[0, :, lc:]
    m = jnp.maximum(jnp.max(s_p, axis=-1, keepdims=True), jnp.max(s_c, axis=-1, keepdims=True))
    p_p = jnp.exp(s_p - m)
    p_c = jnp.exp(s_c - m)
    l = jnp.sum(p_p, axis=-1, keepdims=True) + jnp.sum(p_c, axis=-1, keepdims=True)
    o = jnp.dot(p_p.astype(BF16), cv_ref[0].astype(BF16), preferred_element_type=F32)
    o = o + jnp.dot(p_c.astype(BF16), v_new.astype(BF16), preferred_element_type=F32)
    o_ref[0] = ((o / l) * jax.nn.silu(z_ref[0])).astype(o_ref.dtype)


def _band_sample(proj3, q_col0, k_col0, v_col0, z_col0, cache_k, cache_v, l, rel_bias, qn_g, kn_g):
    bsz, n, _ = proj3.shape
    lc = cache_k.shape[2]
    nh = rel_bias.shape[0]
    qb, kb, vb, zb = (c // HEAD_DIM for c in (q_col0, k_col0, v_col0, z_col0))
    bias = _toeplitz_bias(rel_bias, n, lc + n, lc)
    assert n <= LANES
    bias = jnp.pad(bias, ((0, 0), (0, 0), (0, LANES - n)), constant_values=NEG_INF)
    blk = (1, n, HEAD_DIM)
    cblk = (None, 1, lc, HEAD_DIM)
    gspec = pl.BlockSpec((1, HEAD_DIM), lambda b, h: (0, 0))
    return pl.pallas_call(
        functools.partial(_band_sample_kernel, lc=lc),
        grid=(bsz, nh),
        in_specs=[pl.BlockSpec(blk, lambda b, h: (b, 0, qb + h)),
                  pl.BlockSpec(blk, lambda b, h: (b, 0, kb + h)),
                  pl.BlockSpec(blk, lambda b, h: (b, 0, vb + h)),
                  pl.BlockSpec(cblk, lambda b, h: (l, b, 0, h)),
                  pl.BlockSpec(cblk, lambda b, h: (l, b, 0, h)),
                  pl.BlockSpec(blk, lambda b, h: (b, 0, zb + h)),
                  pl.BlockSpec((1, n, lc + LANES), lambda b, h: (h, 0, 0)),
                  gspec, gspec],
        out_specs=[pl.BlockSpec(blk, lambda b, h: (b, 0, h)),
                   pl.BlockSpec(blk, lambda b, h: (b, 0, h))],
        out_shape=(jax.ShapeDtypeStruct((bsz, n, nh * HEAD_DIM), BF16),
                   jax.ShapeDtypeStruct((bsz, n, nh * HEAD_DIM), F32)),
        compiler_params=_params("parallel", "parallel"),
        name="band_sample",
    )(proj3, proj3, proj3, cache_k, cache_v, proj3, bias, qn_g.reshape(1, HEAD_DIM), kn_g.reshape(1, HEAD_DIM))


def _suffix_matrix():
    kb = SB_K_BLOCK
    j = lax.broadcasted_iota(jnp.int32, (2 * kb, 2 * kb), 0) % kb
    c = lax.broadcasted_iota(jnp.int32, (2 * kb, 2 * kb), 1)
    return jnp.where((c >= kb) | (j >= c), -1.0, 0.0).astype(BF16)


def _sb_groups(q, groups, carry, acc, umat):
    scale = 1.0 / math.sqrt(HEAD_DIM)
    kb = SB_K_BLOCK
    staged = []
    for k, v, causal in groups:
        z = _qk(q, k.astype(BF16)) * scale
        fail = jnp.maximum(z, 0.0) + jnp.log(1.0 + jnp.exp(-jnp.abs(z)))
        if causal is not None:
            fail = jnp.where(causal, fail, 0.0)
        hi = fail.astype(BF16)
        lo = (fail - hi.astype(F32)).astype(BF16)
        parts = []
        for b in reversed(range(SB_GROUP // kb)):
            cols = slice(b * kb, (b + 1) * kb)
            sums = jnp.dot(jnp.concatenate([hi[:, cols], lo[:, cols]], axis=1), umat, preferred_element_type=F32)
            parts.append((b, z[:, cols] + sums[:, :kb], sums[:, kb:]))
        staged.append((parts, causal))
    ws = []
    for parts, causal in staged:
        log_w = [None] * len(parts)
        for b, part, total in parts:
            log_w[b] = part + carry
            carry = carry + total
        w = jnp.exp(jnp.concatenate(log_w, axis=1))
        if causal is not None:
            w = jnp.where(causal, w, 0.0)
        ws.append(w.astype(BF16))
    vs = [v.astype(BF16) for _, v, _ in groups]
    w_all = ws[0] if len(ws) == 1 else jnp.concatenate(ws, axis=1)
    v_all = vs[0] if len(vs) == 1 else jnp.concatenate(vs, axis=0)
    return carry, acc + jnp.dot(w_all, v_all, preferred_element_type=F32)


def _sb_prompt_kernel(q_ref, k_ref, v_ref, z_ref, o_ref):
    qi = pl.program_id(2)
    tq = SB_Q_BLOCK
    q = q_ref[0].astype(BF16)
    umat = _suffix_matrix()
    q0 = qi * tq
    carry = jnp.zeros((tq, SB_K_BLOCK), F32)
    acc = jnp.zeros((tq, HEAD_DIM), F32)

    def load(k0, causal):
        k0 = pl.multiple_of(k0, SB_GROUP)
        return k_ref[0, pl.ds(k0, SB_GROUP), :], v_ref[0, pl.ds(k0, SB_GROUP), :], causal

    n_diag = tq // SB_GROUP
    assert n_diag % SB_PROMPT_UNROLL == 0
    for d in reversed(range(n_diag)):
        r0 = d * SB_GROUP
        k, v, _ = load(q0 + r0, None)
        causal = (lax.broadcasted_iota(jnp.int32, (tq - r0, SB_GROUP), 1)
                  < lax.broadcasted_iota(jnp.int32, (tq - r0, SB_GROUP), 0))
        c_lo, a_lo = _sb_groups(q[r0:], [(k, v, causal)], carry[r0:], acc[r0:], umat)
        carry = c_lo if d == 0 else jnp.concatenate([carry[:r0], c_lo], axis=0)
        acc = a_lo if d == 0 else jnp.concatenate([acc[:r0], a_lo], axis=0)

    def body(j, ca):
        groups = [load(q0 - (j * SB_PROMPT_UNROLL + u + 1) * SB_GROUP, None) for u in range(SB_PROMPT_UNROLL)]
        return _sb_groups(q, groups, ca[0], ca[1], umat)

    carry, acc = lax.fori_loop(0, qi * (n_diag // SB_PROMPT_UNROLL), body, (carry, acc))
    o_ref[0] = (acc * jax.nn.silu(z_ref[0])).astype(o_ref.dtype)


def _sb_prompt(proj3, q_col0, k_col0, v_col0, z_col0, nh):
    bsz, s, _ = proj3.shape
    tq = _row_tile(s, SB_Q_BLOCK)
    assert tq == SB_Q_BLOCK
    qb, kb, vb, zb = (c // HEAD_DIM for c in (q_col0, k_col0, v_col0, z_col0))
    blk = (1, tq, HEAD_DIM)
    full = (1, s, HEAD_DIM)
    return pl.pallas_call(
        _sb_prompt_kernel,
        grid=(bsz, nh, s // tq),
        in_specs=[pl.BlockSpec(blk, lambda b, h, i: (b, i, qb + h)),
                  pl.BlockSpec(full, lambda b, h, i: (b, 0, kb + h)),
                  pl.BlockSpec(full, lambda b, h, i: (b, 0, vb + h)),
                  pl.BlockSpec(blk, lambda b, h, i: (b, i, zb + h))],
        out_specs=pl.BlockSpec(blk, lambda b, h, i: (b, i, h)),
        out_shape=jax.ShapeDtypeStruct((bsz, s, nh * HEAD_DIM), BF16),
        compiler_params=_params("parallel", "parallel", "arbitrary"),
        name="sb_prompt",
    )(proj3, proj3, proj3, proj3)


def _sb_sample_kernel(q_ref, k_ref, v_ref, ck_ref, cv_ref, z_ref, o_ref, *, past_len):
    n = q_ref.shape[1]
    q = q_ref[0].astype(BF16)
    umat = _suffix_matrix()
    row = lax.broadcasted_iota(jnp.int32, (n, SB_GROUP), 0)
    col = lax.broadcasted_iota(jnp.int32, (n, SB_GROUP), 1)
    carry = jnp.zeros((n, SB_K_BLOCK), F32)
    acc = jnp.zeros((n, HEAD_DIM), F32)
    carry, acc = _sb_groups(q, [(k_ref[0], v_ref[0], col < row)], carry, acc, umat)
    n_past = past_len // SB_GROUP
    unroll = math.gcd(n_past, SB_SAMPLE_UNROLL)

    def body(j, ca):
        groups = []
        for u in range(unroll):
            k0 = pl.multiple_of(past_len - (j * unroll + u + 1) * SB_GROUP, SB_GROUP)
            groups.append((ck_ref[0, pl.ds(k0, SB_GROUP), :], cv_ref[0, pl.ds(k0, SB_GROUP), :], None))
        return _sb_groups(q, groups, ca[0], ca[1], umat)

    carry, acc = lax.fori_loop(0, n_past // unroll, body, (carry, acc))
    o_ref[0] = (acc * jax.nn.silu(z_ref[0])).astype(o_ref.dtype)


def _sb_sample(proj3, q_col0, z_col0, k_new, v_new, cache_k, cache_v, l, nh):
    bsz, n, _ = proj3.shape
    past_len = cache_k.shape[2]
    assert past_len % SB_GROUP == 0 and n <= SB_GROUP
    qb, zb = q_col0 // HEAD_DIM, z_col0 // HEAD_DIM
    pad = ((0, 0), (0, SB_GROUP - n), (0, 0))
    k_pad = jnp.pad(k_new, pad)
    v_pad = jnp.pad(v_new, pad)
    blk = (1, n, HEAD_DIM)
    own = (1, SB_GROUP, HEAD_DIM)
    past = (None, 1, past_len, HEAD_DIM)
    return pl.pallas_call(
        functools.partial(_sb_sample_kernel, past_len=past_len),
        grid=(bsz, nh),
        in_specs=[pl.BlockSpec(blk, lambda b, h: (b, 0, qb + h)),
                  pl.BlockSpec(own, lambda b, h: (b, 0, h)),
                  pl.BlockSpec(own, lambda b, h: (b, 0, h)),
                  pl.BlockSpec(past, lambda b, h: (l, b, 0, h)),
                  pl.BlockSpec(past, lambda b, h: (l, b, 0, h)),
                  pl.BlockSpec(blk, lambda b, h: (b, 0, zb + h))],
        out_specs=pl.BlockSpec(blk, lambda b, h: (b, 0, h)),
        out_shape=jax.ShapeDtypeStruct((bsz, n, nh * HEAD_DIM), BF16),
        compiler_params=_params("parallel", "parallel"),
        name="sb_sample",
    )(proj3, k_pad, v_pad, cache_k, cache_v, proj3)


def _layer(x, l, w, cache):
    (norm_g, w_in, ssm, w_glu, b_glu, qn_g, kn_g, rel_bias, w_br_a, w_br_b, w_br_c, gate_b, w_out) = w
    ab_re, ab_im, bbt_re, bbt_im, c_re, c_im, d_skip = ssm
    bsz, s, d = x.shape
    mw = w_glu.shape[1]
    nh = mw // HEAD_DIM
    g = ab_re.shape[0]
    m = bsz * s
    x2 = x.reshape(m, d)
    hx = _rmsnorm(x2, norm_g)
    proj = _matmul(hx, w_in, l)
    n_in = proj.shape[1]
    proj3 = proj.reshape(bsz, s, n_in)
    col = lambda i: i * mw
    u_tm = proj3[:, :, :mw].transpose(1, 0, 2)
    if cache is None:
        h0_re = jnp.zeros((bsz, g, SSM_STATE), F32)
        h0_im = h0_re
    else:
        h0_re, h0_im = cache[4], cache[5]
    y_tm, ht_re, ht_im = _s5_branch(u_tm, h0_re, h0_im, ab_re, ab_im, bbt_re, bbt_im, c_re, c_im, d_skip)
    lhs_a = _glu_gate(y_tm, w_glu, l, b_glu, proj, col(1), bsz, s)
    k_c = proj3[:, :, col(7):col(8)]
    v_c = proj3[:, :, col(8):col(9)]
    v_b = proj3[:, :, col(4):col(5)]
    if cache is None:
        lhs_b, band_k = _band_prompt(proj3, col(2), col(3), col(4), col(5), rel_bias, qn_g, kn_g)
        keep = min(BAND_LEN, s)
        band_v = v_b[:, s - keep:]
        lhs_c = _sb_prompt(proj3, col(6), col(7), col(8), col(9), nh)
    else:
        sb_k, sb_v, bk, bv = cache[:4]
        lhs_b, band_k = _band_sample(proj3, col(2), col(3), col(4), col(5), bk, bv, l, rel_bias, qn_g, kn_g)
        band_v = v_b
        lhs_c = _sb_sample(proj3, col(6), col(9), k_c, v_c, sb_k, sb_v, l, nh)
    mixed = _branch_merge(lhs_a, lhs_b.reshape(m, mw), lhs_c.reshape(m, mw), w_br_a, w_br_b, w_br_c, l,
                          proj, gate_b, col(10))
    y = _matmul_residual(mixed, w_out, l, x2).reshape(bsz, s, d)
    heads = lambda t: t.reshape(bsz, t.shape[1], nh, HEAD_DIM)
    return y, (heads(k_c), heads(v_c), heads(band_k), heads(band_v), ht_re, ht_im)


def kernel(x_prompt, x_sample, cache_sb_k, cache_sb_v, cache_band_k, cache_band_v, state_ssm_re, state_ssm_im, norm_g, w_in, ssm_a_re, ssm_a_im, ssm_log_dt, ssm_b_re, ssm_b_im, ssm_c_re, ssm_c_im, ssm_d, w_glu, b_glu, q_norm_g, k_norm_g, rel_bias, w_br_a, w_br_b, w_br_c, gate_b, w_out):
    depth = w_in.shape[0]
    yp, ys = x_prompt, x_sample
    p_states, s_states = [], []
    w_in, w_glu, w_br_a, w_br_b, w_br_c, w_out = (t.astype(BF16) for t in (w_in, w_glu, w_br_a, w_br_b, w_br_c, w_out))
    caches = tuple(t.reshape(*t.shape[:3], -1) for t in (cache_sb_k, cache_sb_v, cache_band_k, cache_band_v))
    for l in range(depth):
        ssm = _discretize(ssm_a_re[l], ssm_a_im[l], ssm_log_dt[l], ssm_b_re[l], ssm_b_im[l])
        w = (norm_g[l], w_in, (*ssm, ssm_c_re[l], ssm_c_im[l], ssm_d[l]), w_glu, b_glu[l], q_norm_g[l], k_norm_g[l],
             rel_bias[l], w_br_a, w_br_b, w_br_c, gate_b[l], w_out)
        yp, sp = _layer(yp, l, w, None)
        ys, ss = _layer(ys, l, w, (*caches, state_ssm_re[l], state_ssm_im[l]))
        p_states.append(sp)
        s_states.append(ss)
    stk = lambda states, i: jnp.stack([st[i] for st in states], axis=0)
    return (yp, ys,
            *(stk(p_states, i) for i in range(6)),
            *(stk(s_states, i) for i in range(6)))
```

```python
import functools
import math

import jax
import jax.numpy as jnp
import numpy as np
from jax import lax
from jax.experimental import pallas as pl
from jax.experimental.pallas import tpu as pltpu

F32 = jnp.float32
BF16 = jnp.bfloat16

CHUNK = 64
HEAD_DIM = 128
SSM_GROUP = 16
SSM_STATE = 64
BAND_CHUNKS = 8
BAND_LEN = BAND_CHUNKS * CHUNK
MAX_REL = 128
N_BRANCH = 3
RMS_EPS = 1e-6
NEG_INF = -1e30

LANES = 128
SUBLANES = 8
VMEM_LIMIT_BYTES = 56 * 1024 * 1024
MM_TILE = 1024
SSM_CH_BLOCK = 128
SSM_GROUPS_PER_BLOCK = SSM_CH_BLOCK // SSM_GROUP
SSM_LANES = SSM_GROUPS_PER_BLOCK * SSM_STATE
S5_RING = 3
SB_Q_BLOCK = 512
SB_K_BLOCK = 128
SB_GROUP = 2 * SB_K_BLOCK
SB_PROMPT_UNROLL = 2
SB_SAMPLE_UNROLL = 4
BAND_Q_BLOCK = 2 * BAND_LEN
BAND_Q_SUB = 2 * CHUNK
BAND_WINDOW = BAND_LEN + BAND_Q_SUB


def _params(*sem):
    return pltpu.CompilerParams(dimension_semantics=sem, vmem_limit_bytes=VMEM_LIMIT_BYTES)


def _row_tile(m, target, align=SUBLANES):
    if m <= target:
        return m
    for t in range(target - target % align, 0, -align):
        if m % t == 0:
            return t
    raise ValueError((m, target, align))


def _rmsnorm_kernel(x_ref, g_ref, o_ref):
    x = x_ref[...]
    r = lax.rsqrt(jnp.mean(x * x, axis=-1, keepdims=True) + RMS_EPS)
    o_ref[...] = (x * r * g_ref[...]).astype(o_ref.dtype)


def _rmsnorm(x, g):
    m, d = x.shape
    tm = _row_tile(m, 512)
    return pl.pallas_call(
        _rmsnorm_kernel,
        grid=(m // tm,),
        in_specs=[pl.BlockSpec((tm, d), lambda i: (i, 0)),
                  pl.BlockSpec((1, d), lambda i: (0, 0))],
        out_specs=pl.BlockSpec((tm, d), lambda i: (i, 0)),
        out_shape=jax.ShapeDtypeStruct((m, d), BF16),
        compiler_params=_params("parallel"),
        name="rmsnorm",
    )(x, g.reshape(1, d))


def _mm_kernel(x_ref, w_ref, o_ref):
    o_ref[...] = jnp.dot(x_ref[...], w_ref[...], preferred_element_type=F32).astype(o_ref.dtype)


def _matmul(x, w, l, out_dtype=F32):
    m, k = x.shape
    _, _, n = w.shape
    tm = _row_tile(m, MM_TILE)
    tn = _row_tile(n, MM_TILE, LANES)
    return pl.pallas_call(
        _mm_kernel,
        grid=(m // tm, n // tn),
        in_specs=[pl.BlockSpec((tm, k), lambda i, j: (i, 0)),
                  pl.BlockSpec((None, k, tn), lambda i, j: (l, 0, j))],
        out_specs=pl.BlockSpec((tm, tn), lambda i, j: (i, j)),
        out_shape=jax.ShapeDtypeStruct((m, n), out_dtype),
        compiler_params=_params("parallel", "arbitrary"),
        name="in_proj",
    )(x, w)


def _mm_res_kernel(x_ref, w_ref, r_ref, o_ref):
    o_ref[...] = r_ref[...] + jnp.dot(x_ref[...], w_ref[...], preferred_element_type=F32)


def _matmul_residual(x, w, l, res):
    m, k = x.shape
    _, _, n = w.shape
    tm = _row_tile(m, MM_TILE)
    tn = _row_tile(n, MM_TILE, LANES)
    return pl.pallas_call(
        _mm_res_kernel,
        grid=(m // tm, n // tn),
        in_specs=[pl.BlockSpec((tm, k), lambda i, j: (i, 0)),
                  pl.BlockSpec((None, k, tn), lambda i, j: (l, 0, j)),
                  pl.BlockSpec((tm, tn), lambda i, j: (i, j))],
        out_specs=pl.BlockSpec((tm, tn), lambda i, j: (i, j)),
        out_shape=jax.ShapeDtypeStruct((m, n), F32),
        compiler_params=_params("parallel", "arbitrary"),
        name="out_proj",
    )(x, w, res)


def _branch_kernel(la_ref, lb_ref, lc_ref, wa_ref, wb_ref, wc_ref,
                   ga_ref, gb_ref, gc_ref, ba_ref, bb_ref, bc_ref, o_ref):
    def one(l_ref, w_ref, g_ref, b_ref):
        o = jnp.dot(l_ref[...], w_ref[...], preferred_element_type=F32)
        return jax.nn.sigmoid(g_ref[...] + b_ref[...]) * o
    mixed = one(la_ref, wa_ref, ga_ref, ba_ref) + one(lb_ref, wb_ref, gb_ref, bb_ref)
    mixed = mixed + one(lc_ref, wc_ref, gc_ref, bc_ref)
    o_ref[...] = mixed.astype(o_ref.dtype)


def _branch_merge(lhs_a, lhs_b, lhs_c, w_a, w_b, w_c, l, proj, gate_b, g_col0):
    m, kw = lhs_a.shape
    d = w_a.shape[2]
    tm = _row_tile(m, 512)
    tn = _row_tile(math.gcd(d, g_col0), MM_TILE, LANES)
    assert g_col0 % tn == 0 and d % tn == 0
    gblk = g_col0 // tn
    nblk = d // tn
    lspec = pl.BlockSpec((tm, kw), lambda j, i: (i, 0))
    wspec = pl.BlockSpec((None, kw, tn), lambda j, i: (l, 0, j))
    gspecs = [pl.BlockSpec((tm, tn), functools.partial(lambda j, i, br: (i, gblk + br * nblk + j), br=br))
              for br in range(N_BRANCH)]
    bspecs = [pl.BlockSpec((1, tn), functools.partial(lambda j, i, br: (0, br * nblk + j), br=br))
              for br in range(N_BRANCH)]
    gb = gate_b.reshape(1, N_BRANCH * d)
    return pl.pallas_call(
        _branch_kernel,
        grid=(d // tn, m // tm),
        in_specs=[lspec, lspec, lspec, wspec, wspec, wspec, *gspecs, *bspecs],
        out_specs=pl.BlockSpec((tm, tn), lambda j, i: (i, j)),
        out_shape=jax.ShapeDtypeStruct((m, d), BF16),
        compiler_params=_params("parallel", "arbitrary"),
        name="branch_merge",
    )(lhs_a, lhs_b, lhs_c, w_a, w_b, w_c, proj, proj, proj, gb, gb, gb)


def _discretize_kernel(are_ref, aim_ref, ldt_ref, bre_ref, bim_ref,
                       abre_ref, abim_ref, bbre_ref, bbim_ref):
    a_re = are_ref[...]
    a_im = aim_ref[...]
    dt = jnp.exp(ldt_ref[...])
    mag = jnp.exp(a_re * dt)
    ab_re = mag * jnp.cos(a_im * dt)
    ab_im = mag * jnp.sin(a_im * dt)
    den = a_re * a_re + a_im * a_im
    nr = ab_re - 1.0
    k_re = (nr * a_re + ab_im * a_im) / den
    k_im = (ab_im * a_re - nr * a_im) / den
    abre_ref[...] = ab_re
    abim_ref[...] = ab_im
    b_re = bre_ref[...]
    b_im = bim_ref[...]
    bbre_ref[...] = k_re[:, None, :] * b_re - k_im[:, None, :] * b_im
    bbim_ref[...] = k_re[:, None, :] * b_im + k_im[:, None, :] * b_re


def _discretize(a_re, a_im, log_dt, b_re, b_im):
    g, n = a_re.shape
    p = b_re.shape[-1]
    gn = jax.ShapeDtypeStruct((g, n), F32)
    gpn = jax.ShapeDtypeStruct((g, p, n), F32)
    return pl.pallas_call(
        _discretize_kernel,
        out_shape=(gn, gn, gpn, gpn),
        name="s5_discretize",
    )(a_re, a_im, log_dt.reshape(g, 1), b_re.transpose(0, 2, 1), b_im.transpose(0, 2, 1))


def _block_diag(x):
    ncb, gl, r, c = x.shape
    eye = jnp.eye(gl, dtype=x.dtype)
    out = x[:, :, :, None, :] * eye[None, :, None, :, None]
    return out.reshape(ncb, gl * r, gl * c)


def _s5_kernel(u_ref, up_ref, h0_ref, a_ref, bblk_ref, cblk_ref, d_ref, y_ref, ht_ref, *scratch,
               t_steps, n_chunks):
    ring, h_ref = scratch[:S5_RING], scratch[S5_RING]
    t_idx = pl.program_id(1)

    @pl.when(t_idx == 0)
    def _():
        h_ref[...] = h0_ref[0]
        for buf in ring:
            buf[...] = jnp.zeros(buf.shape, F32)

    rows = t_steps * SUBLANES

    def stages(in_buf, scan_buf, out_buf):
        u = u_ref[...].reshape(rows, SSM_CH_BLOCK)
        in_buf[...] = jnp.dot(u.astype(BF16), bblk_ref[0], preferred_element_type=F32)

        u_out = up_ref[...].reshape(rows, SSM_CH_BLOCK)
        y = jnp.dot(out_buf[...].astype(BF16), cblk_ref[0], preferred_element_type=F32)
        y = y + d_ref[0] * u_out
        y_ref[...] = jax.nn.gelu(y).reshape(t_steps, SUBLANES, SSM_CH_BLOCK)

        a_re = a_ref[0, :, :SSM_LANES]
        a_im = a_ref[0, :, SSM_LANES:]
        h_re = h_ref[:, :SSM_LANES]
        h_im = h_ref[:, SSM_LANES:]
        s_re, s_im = h_re, h_im
        for t in range(t_steps):
            row = slice(t * SUBLANES, (t + 1) * SUBLANES)
            b_re = scan_buf[row, :SSM_LANES]
            b_im = scan_buf[row, SSM_LANES:]
            s_re, s_im = a_re * s_re - a_im * s_im + b_re, a_re * s_im + a_im * s_re + b_im
            scan_buf[row, :SSM_LANES] = s_re
            scan_buf[row, SSM_LANES:] = s_im
        in_range = (t_idx >= 1) & (t_idx <= n_chunks)
        h_ref[:, :SSM_LANES] = jnp.where(in_range, s_re, h_re)
        h_ref[:, SSM_LANES:] = jnp.where(in_range, s_im, h_im)
        ht_ref[0] = h_ref[...]

    for k in range(S5_RING):
        pl.when(t_idx % S5_RING == k)(functools.partial(
            stages, ring[k], ring[(k + S5_RING - 1) % S5_RING], ring[(k + S5_RING - 2) % S5_RING]))


def _s5_branch(u_tm, h0_re, h0_im, ab_re, ab_im, bbt_re, bbt_im, c_re, c_im, d_skip):
    s, b, w = u_tm.shape
    assert b == SUBLANES, "the scan keeps one stream per sublane"
    g, n = ab_re.shape
    assert n == SSM_STATE and w == g * SSM_GROUP and w % SSM_CH_BLOCK == 0
    ncb = w // SSM_CH_BLOCK
    gl = SSM_GROUPS_PER_BLOCK
    t_steps = _row_tile(s, 128)

    def lanes(x):
        r = x.shape[0]
        return x.reshape(r, ncb, gl * n).transpose(1, 0, 2)

    h0 = jnp.concatenate([lanes(h0_re), lanes(h0_im)], axis=-1)
    a_tab = jnp.concatenate([lanes(jnp.broadcast_to(ab_re[None], (SUBLANES, g, n))),
                             lanes(jnp.broadcast_to(ab_im[None], (SUBLANES, g, n)))], axis=-1)
    bblk = jnp.concatenate([_block_diag(bbt_re.reshape(ncb, gl, SSM_GROUP, n)),
                            _block_diag(bbt_im.reshape(ncb, gl, SSM_GROUP, n))], axis=-1).astype(BF16)
    cblk = jnp.concatenate([_block_diag(c_re.reshape(ncb, gl, SSM_GROUP, n).transpose(0, 1, 3, 2)),
                            _block_diag(-c_im.reshape(ncb, gl, SSM_GROUP, n).transpose(0, 1, 3, 2))],
                           axis=1).astype(BF16)
    d_tab = d_skip.reshape(ncb, 1, SSM_CH_BLOCK)
    st_spec = pl.BlockSpec((1, SUBLANES, 2 * SSM_LANES), lambda c, t: (c, 0, 0))
    n_chunks = s // t_steps
    chunk = (t_steps, SUBLANES, SSM_CH_BLOCK)
    last = n_chunks - 1
    y_tm, ht = pl.pallas_call(
        functools.partial(_s5_kernel, t_steps=t_steps, n_chunks=n_chunks),
        grid=(ncb, n_chunks + S5_RING - 1),
        in_specs=[pl.BlockSpec(chunk, lambda c, t: (jnp.minimum(t, last), 0, c)),
                  pl.BlockSpec(chunk, lambda c, t: (jnp.clip(t - 2, 0, last), 0, c)),
                  st_spec, st_spec,
                  pl.BlockSpec((1, SSM_CH_BLOCK, 2 * SSM_LANES), lambda c, t: (c, 0, 0)),
                  pl.BlockSpec((1, 2 * SSM_LANES, SSM_CH_BLOCK), lambda c, t: (c, 0, 0)),
                  pl.BlockSpec((1, 1, SSM_CH_BLOCK), lambda c, t: (c, 0, 0))],
        out_specs=[pl.BlockSpec(chunk, lambda c, t: (jnp.clip(t - 2, 0, last), 0, c)), st_spec],
        out_shape=(jax.ShapeDtypeStruct((s, b, w), F32),
                   jax.ShapeDtypeStruct((ncb, SUBLANES, 2 * SSM_LANES), F32)),
        scratch_shapes=[pltpu.VMEM((t_steps * SUBLANES, 2 * SSM_LANES), F32)] * S5_RING
                       + [pltpu.VMEM((SUBLANES, 2 * SSM_LANES), F32)],
        compiler_params=_params("parallel", "arbitrary"),
        name="s5_scan",
    )(u_tm, u_tm, h0, a_tab, bblk, cblk, d_tab)

    def unlanes(x):
        return x.transpose(1, 0, 2).reshape(b, g, n)

    return y_tm, unlanes(ht[..., :SSM_LANES]), unlanes(ht[..., SSM_LANES:])


def _glu_kernel(y_ref, w_ref, b_ref, z_ref, o_ref):
    y = y_ref[...]
    gate = jnp.dot(y.astype(BF16), w_ref[...], preferred_element_type=F32) + b_ref[...]
    o_ref[...] = (y * jax.nn.sigmoid(gate) * jax.nn.silu(z_ref[...])).astype(o_ref.dtype)


def _glu_gate(y_tm, w_glu, l, b_glu, proj, z_col0, bsz, s):
    w = w_glu.shape[1]
    tm = _row_tile(s, 512)
    nt = s // tm
    zblk = z_col0 // w
    return pl.pallas_call(
        _glu_kernel,
        grid=(bsz, nt),
        in_specs=[pl.BlockSpec((tm, w), lambda b, t: (t, b)),
                  pl.BlockSpec((None, w, w), lambda b, t: (l, 0, 0)),
                  pl.BlockSpec((1, w), lambda b, t: (0, 0)),
                  pl.BlockSpec((tm, w), lambda b, t: (b * nt + t, zblk))],
        out_specs=pl.BlockSpec((tm, w), lambda b, t: (b * nt + t, 0)),
        out_shape=jax.ShapeDtypeStruct((bsz * s, w), BF16),
        compiler_params=_params("parallel", "arbitrary"),
        name="glu_gate",
    )(y_tm.reshape(s, bsz * w), w_glu, b_glu.reshape(1, w), proj)


def _head_rms(x, g):
    r = lax.rsqrt(jnp.mean(x * x, axis=-1, keepdims=True) + RMS_EPS)
    return x * r * g


def _qk(q, k):
    return lax.dot_general(q, k, (((1,), (1,)), ((), ())), preferred_element_type=F32)


def _band_prompt_kernel(q_ref, kp_ref, kc_ref, vp_ref, vc_ref, z_ref, bias_ref, qg_ref, kg_ref,
                        o_ref, kn_ref):
    i = pl.program_id(2)
    scale = 1.0 / math.sqrt(HEAD_DIM)
    qn = _head_rms(q_ref[0], qg_ref[...]).astype(BF16)
    kpn = _head_rms(kp_ref[0], kg_ref[...])
    kcn = _head_rms(kc_ref[0], kg_ref[...])
    tq = q_ref.shape[1]
    kn_ref[0] = kcn[tq - BAND_LEN:]
    k_all = jnp.concatenate([kpn.astype(BF16), kcn.astype(BF16)], axis=0)
    v_all = jnp.concatenate([vp_ref[0].astype(BF16), vc_ref[0].astype(BF16)], axis=0)
    bias = bias_ref[0]
    col = lax.broadcasted_iota(jnp.int32, (BAND_Q_SUB, BAND_WINDOW), 1)
    z = z_ref[0]
    n_sub = tq // BAND_Q_SUB
    rows = [slice(p * BAND_Q_SUB, (p + 1) * BAND_Q_SUB) for p in range(n_sub)]
    wins = [slice(p * BAND_Q_SUB, p * BAND_Q_SUB + BAND_WINDOW) for p in range(n_sub)]
    scores = []
    for p in range(n_sub):
        s = _qk(qn[rows[p]], k_all[wins[p]]) * scale + bias
        if p * BAND_Q_SUB < BAND_LEN:
            n_missing = jnp.where(i == 0, BAND_LEN - p * BAND_Q_SUB, 0)
            s = jnp.where(col < n_missing, NEG_INF, s)
        scores.append(s)
    probs = [jnp.exp(s - jnp.max(s, axis=-1, keepdims=True)) for s in scores]
    outs = [jnp.dot(pr.astype(BF16), v_all[wins[p]], preferred_element_type=F32) for p, pr in enumerate(probs)]
    for p in range(n_sub):
        l = jnp.sum(probs[p], axis=-1, keepdims=True)
        o_ref[0, rows[p], :] = ((outs[p] / l) * jax.nn.silu(z[rows[p]])).astype(o_ref.dtype)


def _static_take(tab, idx):
    nh = tab.shape[0]
    idx = np.asarray(idx)
    pieces = []
    i = 0
    while i < len(idx):
        step = int(idx[i + 1] - idx[i]) if i + 1 < len(idx) else 0
        if step not in (-1, 0, 1):
            step = 0
            j = i + 1
        else:
            j = i + 1
            while j < len(idx) and idx[j] - idx[j - 1] == step:
                j += 1
        n = j - i
        a = int(idx[i])
        if step == 0:
            pieces.append(jnp.broadcast_to(tab[:, a:a + 1], (nh, n)))
        elif step == 1:
            pieces.append(tab[:, a:a + n])
        else:
            pieces.append(tab[:, a - n + 1:a + 1][:, ::-1])
        i = j
    return jnp.concatenate(pieces, axis=1)


def _toeplitz_bias(rel_bias, rows, cols, offset):
    nh = rel_bias.shape[0]
    period = rows + cols
    m = np.arange(period)
    c_minus_r = np.where(m < cols, m, m - period)
    rel = np.clip(offset - c_minus_r, -MAX_REL, MAX_REL) + MAX_REL
    vec = _static_take(rel_bias.astype(F32), rel)
    return jnp.tile(vec, (1, rows))[:, :rows * (period - 1)].reshape(nh, rows, period - 1)[:, :, :cols]


def _band_bias_prompt(rel_bias):
    r = np.arange(BAND_Q_SUB)
    c = np.arange(BAND_WINDOW)
    jb = c[None, :] - (r // CHUNK * CHUNK)[:, None]
    inside = (jb >= 0) & (jb < (BAND_CHUNKS + 1) * CHUNK)
    toep = _toeplitz_bias(rel_bias, BAND_Q_SUB, BAND_WINDOW, BAND_LEN)
    return jnp.where(jnp.asarray(inside)[None], toep, NEG_INF)


def _band_prompt(proj3, q_col0, k_col0, v_col0, z_col0, rel_bias, qn_g, kn_g):
    bsz, s, _ = proj3.shape
    nh = rel_bias.shape[0]
    tq = _row_tile(s, BAND_Q_BLOCK, BAND_LEN)
    assert s % tq == 0 and tq % BAND_LEN == 0
    nq = s // tq
    per = tq // BAND_LEN
    qb, kb, vb, zb = (c // HEAD_DIM for c in (q_col0, k_col0, v_col0, z_col0))
    bias = _band_bias_prompt(rel_bias)
    blk = (1, tq, HEAD_DIM)
    prev = (1, BAND_LEN, HEAD_DIM)
    gspec = pl.BlockSpec((1, HEAD_DIM), lambda h, b, i: (0, 0))
    return pl.pallas_call(
        _band_prompt_kernel,
        grid=(nh, bsz, nq),
        in_specs=[pl.BlockSpec(blk, lambda h, b, i: (b, i, qb + h)),
                  pl.BlockSpec(prev, lambda h, b, i: (b, jnp.maximum(i * per - 1, 0), kb + h)),
                  pl.BlockSpec(blk, lambda h, b, i: (b, i, kb + h)),
                  pl.BlockSpec(prev, lambda h, b, i: (b, jnp.maximum(i * per - 1, 0), vb + h)),
                  pl.BlockSpec(blk, lambda h, b, i: (b, i, vb + h)),
                  pl.BlockSpec(blk, lambda h, b, i: (b, i, zb + h)),
                  pl.BlockSpec((1, BAND_Q_SUB, BAND_WINDOW), lambda h, b, i: (h, 0, 0)),
                  gspec, gspec],
        out_specs=[pl.BlockSpec(blk, lambda h, b, i: (b, i, h)),
                   pl.BlockSpec(prev, lambda h, b, i: (b, 0, h))],
        out_shape=(jax.ShapeDtypeStruct((bsz, s, nh * HEAD_DIM), BF16),
                   jax.ShapeDtypeStruct((bsz, BAND_LEN, nh * HEAD_DIM), F32)),
        compiler_params=_params("parallel", "parallel", "arbitrary"),
        name="band_prompt",
    )(proj3, proj3, proj3, proj3, proj3, proj3, bias, qn_g.reshape(1, HEAD_DIM), kn_g.reshape(1, HEAD_DIM))


def _copy_head_rows(head, srcs, dsts):
    n_rows = dsts[0].shape[0]
    nh = srcs[0].shape[1] // n_rows
    for hh in range(nh):
        @pl.when(head == hh)
        def _(hh=hh):
            for src, dst in zip(srcs, dsts):
                dst[...] = src[0, pl.ds(hh, n_rows, stride=nh), :]


def _band_sample_kernel(q_ref, k_ref, v_ref, ck_ref, cv_ref, z_ref, bias_ref, qg_ref, kg_ref,
                        o_ref, kn_ref, kbuf_ref, vbuf_ref, *, lc):
    scale = 1.0 / math.sqrt(HEAD_DIM)
    n = q_ref.shape[1]
    _copy_head_rows(pl.program_id(1), (ck_ref, cv_ref), (kbuf_ref, vbuf_ref))
    qn = _head_rms(q_ref[0], qg_ref[...]).astype(BF16)
    kn = _head_rms(k_ref[0], kg_ref[...])
    kn_ref[0] = kn
    zpad = jnp.zeros((LANES - n, HEAD_DIM), F32)
    kn = jnp.concatenate([kn, zpad], axis=0)
    v_new = jnp.concatenate([v_ref[0], zpad], axis=0)
    s_p = _qk(qn, kbuf_ref[...].astype(BF16)) * scale + bias_ref[0, :, :lc]
    s_c = _qk(qn, kn.astype(BF16)) * scale + bias_ref[0, :, lc:]
    m = jnp.maximum(jnp.max(s_p, axis=-1, keepdims=True), jnp.max(s_c, axis=-1, keepdims=True))
    p_p = jnp.exp(s_p - m)
    p_c = jnp.exp(s_c - m)
    l = jnp.sum(p_p, axis=-1, keepdims=True) + jnp.sum(p_c, axis=-1, keepdims=True)
    o = jnp.dot(p_p.astype(BF16), vbuf_ref[...].astype(BF16), preferred_element_type=F32)
    o = o + jnp.dot(p_c.astype(BF16), v_new.astype(BF16), preferred_element_type=F32)
    o_ref[0] = ((o / l) * jax.nn.silu(z_ref[0])).astype(o_ref.dtype)


def _band_sample(proj3, q_col0, k_col0, v_col0, z_col0, cache_k, cache_v, l, rel_bias, qn_g, kn_g):
    bsz, n, _ = proj3.shape
    lc = cache_k.shape[2]
    nh = rel_bias.shape[0]
    rows = lambda t: t.reshape(*t.shape[:2], lc * nh, HEAD_DIM)
    cache_k, cache_v = rows(cache_k), rows(cache_v)
    qb, kb, vb, zb = (c // HEAD_DIM for c in (q_col0, k_col0, v_col0, z_col0))
    bias = _toeplitz_bias(rel_bias, n, lc + n, lc)
    assert n <= LANES
    bias = jnp.pad(bias, ((0, 0), (0, 0), (0, LANES - n)), constant_values=NEG_INF)
    blk = (1, n, HEAD_DIM)
    cblk = (None, 1, lc * nh, HEAD_DIM)
    gspec = pl.BlockSpec((1, HEAD_DIM), lambda b, h: (0, 0))
    return pl.pallas_call(
        functools.partial(_band_sample_kernel, lc=lc),
        grid=(bsz, nh),
        in_specs=[pl.BlockSpec(blk, lambda b, h: (b, 0, qb + h)),
                  pl.BlockSpec(blk, lambda b, h: (b, 0, kb + h)),
                  pl.BlockSpec(blk, lambda b, h: (b, 0, vb + h)),
                  pl.BlockSpec(cblk, lambda b, h: (l, b, 0, 0)),
                  pl.BlockSpec(cblk, lambda b, h: (l, b, 0, 0)),
                  pl.BlockSpec(blk, lambda b, h: (b, 0, zb + h)),
                  pl.BlockSpec((1, n, lc + LANES), lambda b, h: (h, 0, 0)),
                  gspec, gspec],
        out_specs=[pl.BlockSpec(blk, lambda b, h: (b, 0, h)),
                   pl.BlockSpec(blk, lambda b, h: (b, 0, h))],
        out_shape=(jax.ShapeDtypeStruct((bsz, n, nh * HEAD_DIM), BF16),
                   jax.ShapeDtypeStruct((bsz, n, nh * HEAD_DIM), F32)),
        scratch_shapes=[pltpu.VMEM((lc, HEAD_DIM), F32)] * 2,
        compiler_params=_params("parallel", "arbitrary"),
        name="band_sample",
    )(proj3, proj3, proj3, cache_k, cache_v, proj3, bias, qn_g.reshape(1, HEAD_DIM), kn_g.reshape(1, HEAD_DIM))


def _suffix_matrix():
    kb = SB_K_BLOCK
    j = lax.broadcasted_iota(jnp.int32, (2 * kb, 2 * kb), 0) % kb
    c = lax.broadcasted_iota(jnp.int32, (2 * kb, 2 * kb), 1)
    return jnp.where((c >= kb) | (j >= c), -1.0, 0.0).astype(BF16)


def _sb_groups(q, groups, carry, acc, umat):
    scale = 1.0 / math.sqrt(HEAD_DIM)
    kb = SB_K_BLOCK
    staged = []
    for k, v, causal in groups:
        z = _qk(q, k.astype(BF16)) * scale
        fail = jnp.maximum(z, 0.0) + jnp.log(1.0 + jnp.exp(-jnp.abs(z)))
        if causal is not None:
            fail = jnp.where(causal, fail, 0.0)
        hi = fail.astype(BF16)
        lo = (fail - hi.astype(F32)).astype(BF16)
        parts = []
        for b in reversed(range(SB_GROUP // kb)):
            cols = slice(b * kb, (b + 1) * kb)
            sums = jnp.dot(jnp.concatenate([hi[:, cols], lo[:, cols]], axis=1), umat, preferred_element_type=F32)
            parts.append((b, z[:, cols] + sums[:, :kb], sums[:, kb:]))
        staged.append((parts, causal))
    ws = []
    for parts, causal in staged:
        log_w = [None] * len(parts)
        for b, part, total in parts:
            log_w[b] = part + carry
            carry = carry + total
        w = jnp.exp(jnp.concatenate(log_w, axis=1))
        if causal is not None:
            w = jnp.where(causal, w, 0.0)
        ws.append(w.astype(BF16))
    vs = [v.astype(BF16) for _, v, _ in groups]
    w_all = ws[0] if len(ws) == 1 else jnp.concatenate(ws, axis=1)
    v_all = vs[0] if len(vs) == 1 else jnp.concatenate(vs, axis=0)
    return carry, acc + jnp.dot(w_all, v_all, preferred_element_type=F32)


def _sb_prompt_kernel(q_ref, k_ref, v_ref, z_ref, o_ref):
    qi = pl.program_id(2)
    tq = SB_Q_BLOCK
    q = q_ref[0].astype(BF16)
    umat = _suffix_matrix()
    q0 = qi * tq
    carry = jnp.zeros((tq, SB_K_BLOCK), F32)
    acc = jnp.zeros((tq, HEAD_DIM), F32)

    def load(k0, causal):
        k0 = pl.multiple_of(k0, SB_GROUP)
        return k_ref[0, pl.ds(k0, SB_GROUP), :], v_ref[0, pl.ds(k0, SB_GROUP), :], causal

    n_diag = tq // SB_GROUP
    assert n_diag % SB_PROMPT_UNROLL == 0
    for d in reversed(range(n_diag)):
        r0 = d * SB_GROUP
        k, v, _ = load(q0 + r0, None)
        causal = (lax.broadcasted_iota(jnp.int32, (tq - r0, SB_GROUP), 1)
                  < lax.broadcasted_iota(jnp.int32, (tq - r0, SB_GROUP), 0))
        c_lo, a_lo = _sb_groups(q[r0:], [(k, v, causal)], carry[r0:], acc[r0:], umat)
        carry = c_lo if d == 0 else jnp.concatenate([carry[:r0], c_lo], axis=0)
        acc = a_lo if d == 0 else jnp.concatenate([acc[:r0], a_lo], axis=0)

    def body(j, ca):
        groups = [load(q0 - (j * SB_PROMPT_UNROLL + u + 1) * SB_GROUP, None) for u in range(SB_PROMPT_UNROLL)]
        return _sb_groups(q, groups, ca[0], ca[1], umat)

    carry, acc = lax.fori_loop(0, qi * (n_diag // SB_PROMPT_UNROLL), body, (carry, acc))
    o_ref[0] = (acc * jax.nn.silu(z_ref[0])).astype(o_ref.dtype)


def _sb_prompt(proj3, q_col0, k_col0, v_col0, z_col0, nh):
    bsz, s, _ = proj3.shape
    tq = _row_tile(s, SB_Q_BLOCK)
    assert tq == SB_Q_BLOCK
    qb, kb, vb, zb = (c // HEAD_DIM for c in (q_col0, k_col0, v_col0, z_col0))
    blk = (1, tq, HEAD_DIM)
    full = (1, s, HEAD_DIM)
    return pl.pallas_call(
        _sb_prompt_kernel,
        grid=(bsz, nh, s // tq),
        in_specs=[pl.BlockSpec(blk, lambda b, h, i: (b, i, qb + h)),
                  pl.BlockSpec(full, lambda b, h, i: (b, 0, kb + h)),
                  pl.BlockSpec(full, lambda b, h, i: (b, 0, vb + h)),
                  pl.BlockSpec(blk, lambda b, h, i: (b, i, zb + h))],
        out_specs=pl.BlockSpec(blk, lambda b, h, i: (b, i, h)),
        out_shape=jax.ShapeDtypeStruct((bsz, s, nh * HEAD_DIM), BF16),
        compiler_params=_params("parallel", "parallel", "arbitrary"),
        name="sb_prompt",
    )(proj3, proj3, proj3, proj3)


def _sb_sample_kernel(q_ref, k_ref, v_ref, ck_ref, cv_ref, z_ref, o_ref, kbuf_ref, vbuf_ref, *, past_len):
    n = q_ref.shape[1]
    _copy_head_rows(pl.program_id(1), (ck_ref, cv_ref), (kbuf_ref, vbuf_ref))

    q = q_ref[0].astype(BF16)
    umat = _suffix_matrix()
    row = lax.broadcasted_iota(jnp.int32, (n, SB_GROUP), 0)
    col = lax.broadcasted_iota(jnp.int32, (n, SB_GROUP), 1)
    carry = jnp.zeros((n, SB_K_BLOCK), F32)
    acc = jnp.zeros((n, HEAD_DIM), F32)
    carry, acc = _sb_groups(q, [(k_ref[0], v_ref[0], col < row)], carry, acc, umat)
    n_past = past_len // SB_GROUP
    unroll = math.gcd(n_past, SB_SAMPLE_UNROLL)

    def body(j, ca):
        groups = []
        for u in range(unroll):
            k0 = pl.multiple_of(past_len - (j * unroll + u + 1) * SB_GROUP, SB_GROUP)
            groups.append((kbuf_ref[pl.ds(k0, SB_GROUP), :], vbuf_ref[pl.ds(k0, SB_GROUP), :], None))
        return _sb_groups(q, groups, ca[0], ca[1], umat)

    carry, acc = lax.fori_loop(0, n_past // unroll, body, (carry, acc))
    o_ref[0] = (acc * jax.nn.silu(z_ref[0])).astype(o_ref.dtype)


def _sb_sample(proj3, q_col0, z_col0, k_new, v_new, cache_k, cache_v, l, nh):
    bsz, n, _ = proj3.shape
    past_len = cache_k.shape[2]
    assert past_len % SB_GROUP == 0 and n <= SB_GROUP
    qb, zb = q_col0 // HEAD_DIM, z_col0 // HEAD_DIM
    pad = ((0, 0), (0, SB_GROUP - n), (0, 0))
    k_pad = jnp.pad(k_new, pad)
    v_pad = jnp.pad(v_new, pad)
    blk = (1, n, HEAD_DIM)
    own = (1, SB_GROUP, HEAD_DIM)
    past = (None, 1, past_len * nh, HEAD_DIM)
    rows = lambda t: t.reshape(*t.shape[:2], past_len * nh, HEAD_DIM)
    cache_k, cache_v = rows(cache_k), rows(cache_v)
    return pl.pallas_call(
        functools.partial(_sb_sample_kernel, past_len=past_len),
        grid=(bsz, nh),
        in_specs=[pl.BlockSpec(blk, lambda b, h: (b, 0, qb + h)),
                  pl.BlockSpec(own, lambda b, h: (b, 0, h)),
                  pl.BlockSpec(own, lambda b, h: (b, 0, h)),
                  pl.BlockSpec(past, lambda b, h: (l, b, 0, 0)),
                  pl.BlockSpec(past, lambda b, h: (l, b, 0, 0)),
                  pl.BlockSpec(blk, lambda b, h: (b, 0, zb + h))],
        out_specs=pl.BlockSpec(blk, lambda b, h: (b, 0, h)),
        out_shape=jax.ShapeDtypeStruct((bsz, n, nh * HEAD_DIM), BF16),
        scratch_shapes=[pltpu.VMEM((past_len, HEAD_DIM), F32)] * 2,
        compiler_params=_params("parallel", "arbitrary"),
        name="sb_sample",
    )(proj3, k_pad, v_pad, cache_k, cache_v, proj3)


def _layer(x, l, w, cache):
    (norm_g, w_in, ssm, w_glu, b_glu, qn_g, kn_g, rel_bias, w_br_a, w_br_b, w_br_c, gate_b, w_out) = w
    ab_re, ab_im, bbt_re, bbt_im, c_re, c_im, d_skip = ssm
    bsz, s, d = x.shape
    mw = w_glu.shape[1]
    nh = mw // HEAD_DIM
    g = ab_re.shape[0]
    m = bsz * s
    x2 = x.reshape(m, d)
    hx = _rmsnorm(x2, norm_g)
    proj = _matmul(hx, w_in, l)
    n_in = proj.shape[1]
    proj3 = proj.reshape(bsz, s, n_in)
    col = lambda i: i * mw
    u_tm = proj3[:, :, :mw].transpose(1, 0, 2)
    if cache is None:
        h0_re = jnp.zeros((bsz, g, SSM_STATE), F32)
        h0_im = h0_re
    else:
        h0_re, h0_im = cache[4], cache[5]
    y_tm, ht_re, ht_im = _s5_branch(u_tm, h0_re, h0_im, ab_re, ab_im, bbt_re, bbt_im, c_re, c_im, d_skip)
    lhs_a = _glu_gate(y_tm, w_glu, l, b_glu, proj, col(1), bsz, s)
    k_c = proj3[:, :, col(7):col(8)]
    v_c = proj3[:, :, col(8):col(9)]
    v_b = proj3[:, :, col(4):col(5)]
    if cache is None:
        lhs_b, band_k = _band_prompt(proj3, col(2), col(3), col(4), col(5), rel_bias, qn_g, kn_g)
        keep = min(BAND_LEN, s)
        band_v = v_b[:, s - keep:]
        lhs_c = _sb_prompt(proj3, col(6), col(7), col(8), col(9), nh)
    else:
        sb_k, sb_v, bk, bv = cache[:4]
        lhs_b, band_k = _band_sample(proj3, col(2), col(3), col(4), col(5), bk, bv, l, rel_bias, qn_g, kn_g)
        band_v = v_b
        lhs_c = _sb_sample(proj3, col(6), col(9), k_c, v_c, sb_k, sb_v, l, nh)
    mixed = _branch_merge(lhs_a, lhs_b.reshape(m, mw), lhs_c.reshape(m, mw), w_br_a, w_br_b, w_br_c, l,
                          proj, gate_b, col(10))
    y = _matmul_residual(mixed, w_out, l, x2).reshape(bsz, s, d)
    heads = lambda t: t.reshape(bsz, t.shape[1], nh, HEAD_DIM)
    return y, (heads(k_c), heads(v_c), heads(band_k), heads(band_v), ht_re, ht_im)


def kernel(x_prompt, x_sample, cache_sb_k, cache_sb_v, cache_band_k, cache_band_v, state_ssm_re, state_ssm_im, norm_g, w_in, ssm_a_re, ssm_a_im, ssm_log_dt, ssm_b_re, ssm_b_im, ssm_c_re, ssm_c_im, ssm_d, w_glu, b_glu, q_norm_g, k_norm_g, rel_bias, w_br_a, w_br_b, w_br_c, gate_b, w_out):
    depth = w_in.shape[0]
    yp, ys = x_prompt, x_sample
    p_states, s_states = [], []
    w_in, w_glu, w_br_a, w_br_b, w_br_c, w_out = (t.astype(BF16) for t in (w_in, w_glu, w_br_a, w_br_b, w_br_c, w_out))
    caches = (cache_sb_k, cache_sb_v, cache_band_k, cache_band_v)
    for l in range(depth):
        ssm = _discretize(ssm_a_re[l], ssm_a_im[l], ssm_log_dt[l], ssm_b_re[l], ssm_b_im[l])
        w = (norm_g[l], w_in, (*ssm, ssm_c_re[l], ssm_c_im[l], ssm_d[l]), w_glu, b_glu[l], q_norm_g[l], k_norm_g[l],
             rel_bias[l], w_br_a, w_br_b, w_br_c, gate_b[l], w_out)
        yp, sp = _layer(yp, l, w, None)
        ys, ss = _layer(ys, l, w, (*caches, state_ssm_re[l], state_ssm_im[l]))
        p_states.append(sp)
        s_states.append(ss)
    stk = lambda states, i: jnp.stack([st[i] for st in states], axis=0)
    return (yp, ys,
            *(stk(p_states, i) for i in range(6)),
            *(stk(s_states, i) for i in range(6)))
```

```python
import functools
import math

import jax
import jax.numpy as jnp
import numpy as np
from jax import lax
from jax.experimental import pallas as pl
from jax.experimental.pallas import tpu as pltpu

F32 = jnp.float32
BF16 = jnp.bfloat16

CHUNK = 64
HEAD_DIM = 128
SSM_GROUP = 16
SSM_STATE = 64
BAND_CHUNKS = 8
BAND_LEN = BAND_CHUNKS * CHUNK
MAX_REL = 128
N_BRANCH = 3
RMS_EPS = 1e-6
NEG_INF = -1e30

LANES = 128
SUBLANES = 8
VMEM_LIMIT_BYTES = 56 * 1024 * 1024
MM_TILE = 1024
SSM_CH_BLOCK = 128
SSM_GROUPS_PER_BLOCK = SSM_CH_BLOCK // SSM_GROUP
SSM_LANES = SSM_GROUPS_PER_BLOCK * SSM_STATE
S5_RING = 3
SB_Q_BLOCK = 512
SB_K_BLOCK = 128
SB_GROUP = 2 * SB_K_BLOCK
SB_PROMPT_UNROLL = 2
SB_SAMPLE_UNROLL = 4
BAND_Q_BLOCK = 2 * BAND_LEN
BAND_Q_SUB = 2 * CHUNK
BAND_WINDOW = BAND_LEN + BAND_Q_SUB


def _params(*sem):
    return pltpu.CompilerParams(dimension_semantics=sem, vmem_limit_bytes=VMEM_LIMIT_BYTES)


def _row_tile(m, target, align=SUBLANES):
    if m <= target:
        return m
    for t in range(target - target % align, 0, -align):
        if m % t == 0:
            return t
    raise ValueError((m, target, align))


def _rmsnorm_kernel(x_ref, g_ref, o_ref):
    x = x_ref[...]
    r = lax.rsqrt(jnp.mean(x * x, axis=-1, keepdims=True) + RMS_EPS)
    o_ref[...] = (x * r * g_ref[...]).astype(o_ref.dtype)


def _rmsnorm(x, g):
    m, d = x.shape
    tm = _row_tile(m, 512)
    return pl.pallas_call(
        _rmsnorm_kernel,
        grid=(m // tm,),
        in_specs=[pl.BlockSpec((tm, d), lambda i: (i, 0)),
                  pl.BlockSpec((1, d), lambda i: (0, 0))],
        out_specs=pl.BlockSpec((tm, d), lambda i: (i, 0)),
        out_shape=jax.ShapeDtypeStruct((m, d), BF16),
        compiler_params=_params("parallel"),
        name="rmsnorm",
    )(x, g.reshape(1, d))


def _mm_kernel(x_ref, w_ref, o_ref):
    o_ref[...] = jnp.dot(x_ref[...], w_ref[...], preferred_element_type=F32).astype(o_ref.dtype)


def _matmul(x, w, l, out_dtype=F32):
    m, k = x.shape
    _, _, n = w.shape
    tm = _row_tile(m, MM_TILE)
    tn = _row_tile(n, MM_TILE, LANES)
    return pl.pallas_call(
        _mm_kernel,
        grid=(m // tm, n // tn),
        in_specs=[pl.BlockSpec((tm, k), lambda i, j: (i, 0)),
                  pl.BlockSpec((None, k, tn), lambda i, j: (l, 0, j))],
        out_specs=pl.BlockSpec((tm, tn), lambda i, j: (i, j)),
        out_shape=jax.ShapeDtypeStruct((m, n), out_dtype),
        compiler_params=_params("parallel", "arbitrary"),
        name="in_proj",
    )(x, w)


def _mm_res_kernel(x_ref, w_ref, r_ref, o_ref):
    o_ref[...] = r_ref[...] + jnp.dot(x_ref[...], w_ref[...], preferred_element_type=F32)


def _matmul_residual(x, w, l, res):
    m, k = x.shape
    _, _, n = w.shape
    tm = _row_tile(m, MM_TILE)
    tn = _row_tile(n, MM_TILE, LANES)
    return pl.pallas_call(
        _mm_res_kernel,
        grid=(m // tm, n // tn),
        in_specs=[pl.BlockSpec((tm, k), lambda i, j: (i, 0)),
                  pl.BlockSpec((None, k, tn), lambda i, j: (l, 0, j)),
                  pl.BlockSpec((tm, tn), lambda i, j: (i, j))],
        out_specs=pl.BlockSpec((tm, tn), lambda i, j: (i, j)),
        out_shape=jax.ShapeDtypeStruct((m, n), F32),
        compiler_params=_params("parallel", "arbitrary"),
        name="out_proj",
    )(x, w, res)


def _branch_kernel(la_ref, lb_ref, lc_ref, wa_ref, wb_ref, wc_ref,
                   ga_ref, gb_ref, gc_ref, ba_ref, bb_ref, bc_ref, o_ref):
    def one(l_ref, w_ref, g_ref, b_ref):
        o = jnp.dot(l_ref[...], w_ref[...], preferred_element_type=F32)
        return jax.nn.sigmoid(g_ref[...] + b_ref[...]) * o
    mixed = one(la_ref, wa_ref, ga_ref, ba_ref) + one(lb_ref, wb_ref, gb_ref, bb_ref)
    mixed = mixed + one(lc_ref, wc_ref, gc_ref, bc_ref)
    o_ref[...] = mixed.astype(o_ref.dtype)


def _branch_merge(lhs_a, lhs_b, lhs_c, w_a, w_b, w_c, l, proj, gate_b, g_col0):
    m, kw = lhs_a.shape
    d = w_a.shape[2]
    tm = _row_tile(m, 512)
    tn = _row_tile(math.gcd(d, g_col0), MM_TILE, LANES)
    assert g_col0 % tn == 0 and d % tn == 0
    gblk = g_col0 // tn
    nblk = d // tn
    lspec = pl.BlockSpec((tm, kw), lambda j, i: (i, 0))
    wspec = pl.BlockSpec((None, kw, tn), lambda j, i: (l, 0, j))
    gspecs = [pl.BlockSpec((tm, tn), functools.partial(lambda j, i, br: (i, gblk + br * nblk + j), br=br))
              for br in range(N_BRANCH)]
    bspecs = [pl.BlockSpec((1, tn), functools.partial(lambda j, i, br: (0, br * nblk + j), br=br))
              for br in range(N_BRANCH)]
    gb = gate_b.reshape(1, N_BRANCH * d)
    return pl.pallas_call(
        _branch_kernel,
        grid=(d // tn, m // tm),
        in_specs=[lspec, lspec, lspec, wspec, wspec, wspec, *gspecs, *bspecs],
        out_specs=pl.BlockSpec((tm, tn), lambda j, i: (i, j)),
        out_shape=jax.ShapeDtypeStruct((m, d), BF16),
        compiler_params=_params("parallel", "arbitrary"),
        name="branch_merge",
    )(lhs_a, lhs_b, lhs_c, w_a, w_b, w_c, proj, proj, proj, gb, gb, gb)


def _discretize_kernel(are_ref, aim_ref, ldt_ref, bre_ref, bim_ref,
                       abre_ref, abim_ref, bbre_ref, bbim_ref):
    a_re = are_ref[...]
    a_im = aim_ref[...]
    dt = jnp.exp(ldt_ref[...])
    mag = jnp.exp(a_re * dt)
    ab_re = mag * jnp.cos(a_im * dt)
    ab_im = mag * jnp.sin(a_im * dt)
    den = a_re * a_re + a_im * a_im
    nr = ab_re - 1.0
    k_re = (nr * a_re + ab_im * a_im) / den
    k_im = (ab_im * a_re - nr * a_im) / den
    abre_ref[...] = ab_re
    abim_ref[...] = ab_im
    b_re = bre_ref[...]
    b_im = bim_ref[...]
    bbre_ref[...] = k_re[:, None, :] * b_re - k_im[:, None, :] * b_im
    bbim_ref[...] = k_re[:, None, :] * b_im + k_im[:, None, :] * b_re


def _discretize(a_re, a_im, log_dt, b_re, b_im):
    g, n = a_re.shape
    p = b_re.shape[-1]
    gn = jax.ShapeDtypeStruct((g, n), F32)
    gpn = jax.ShapeDtypeStruct((g, p, n), F32)
    return pl.pallas_call(
        _discretize_kernel,
        out_shape=(gn, gn, gpn, gpn),
        name="s5_discretize",
    )(a_re, a_im, log_dt.reshape(g, 1), b_re.transpose(0, 2, 1), b_im.transpose(0, 2, 1))


def _block_diag(x):
    ncb, gl, r, c = x.shape
    eye = jnp.eye(gl, dtype=x.dtype)
    out = x[:, :, :, None, :] * eye[None, :, None, :, None]
    return out.reshape(ncb, gl * r, gl * c)


def _s5_kernel(u_ref, up_ref, h0_ref, a_ref, bblk_ref, cblk_ref, d_ref, y_ref, ht_ref, *scratch,
               t_steps, n_chunks):
    ring, h_ref = scratch[:S5_RING], scratch[S5_RING]
    t_idx = pl.program_id(1)

    @pl.when(t_idx == 0)
    def _():
        h_ref[...] = h0_ref[0]
        for buf in ring:
            buf[...] = jnp.zeros(buf.shape, F32)

    rows = t_steps * SUBLANES

    def stages(in_buf, scan_buf, out_buf):
        u = u_ref[...].reshape(rows, SSM_CH_BLOCK)
        in_buf[...] = jnp.dot(u.astype(BF16), bblk_ref[0], preferred_element_type=F32)

        u_out = up_ref[...].reshape(rows, SSM_CH_BLOCK)
        y = jnp.dot(out_buf[...].astype(BF16), cblk_ref[0], preferred_element_type=F32)
        y = y + d_ref[0] * u_out
        y_ref[...] = jax.nn.gelu(y).reshape(t_steps, SUBLANES, SSM_CH_BLOCK)

        a_re = a_ref[0, :, :SSM_LANES]
        a_im = a_ref[0, :, SSM_LANES:]
        h_re = h_ref[:, :SSM_LANES]
        h_im = h_ref[:, SSM_LANES:]
        s_re, s_im = h_re, h_im
        for t in range(t_steps):
            row = slice(t * SUBLANES, (t + 1) * SUBLANES)
            b_re = scan_buf[row, :SSM_LANES]
            b_im = scan_buf[row, SSM_LANES:]
            s_re, s_im = a_re * s_re - a_im * s_im + b_re, a_re * s_im + a_im * s_re + b_im
            scan_buf[row, :SSM_LANES] = s_re
            scan_buf[row, SSM_LANES:] = s_im
        in_range = (t_idx >= 1) & (t_idx <= n_chunks)
        h_ref[:, :SSM_LANES] = jnp.where(in_range, s_re, h_re)
        h_ref[:, SSM_LANES:] = jnp.where(in_range, s_im, h_im)
        ht_ref[0] = h_ref[...]

    for k in range(S5_RING):
        pl.when(t_idx % S5_RING == k)(functools.partial(
            stages, ring[k], ring[(k + S5_RING - 1) % S5_RING], ring[(k + S5_RING - 2) % S5_RING]))


def _s5_branch(u_tm, h0_re, h0_im, ab_re, ab_im, bbt_re, bbt_im, c_re, c_im, d_skip):
    s, b, w = u_tm.shape
    assert b == SUBLANES, "the scan keeps one stream per sublane"
    g, n = ab_re.shape
    assert n == SSM_STATE and w == g * SSM_GROUP and w % SSM_CH_BLOCK == 0
    ncb = w // SSM_CH_BLOCK
    gl = SSM_GROUPS_PER_BLOCK
    t_steps = _row_tile(s, 128)

    def lanes(x):
        r = x.shape[0]
        return x.reshape(r, ncb, gl * n).transpose(1, 0, 2)

    h0 = jnp.concatenate([lanes(h0_re), lanes(h0_im)], axis=-1)
    a_tab = jnp.concatenate([lanes(jnp.broadcast_to(ab_re[None], (SUBLANES, g, n))),
                             lanes(jnp.broadcast_to(ab_im[None], (SUBLANES, g, n)))], axis=-1)
    bblk = jnp.concatenate([_block_diag(bbt_re.reshape(ncb, gl, SSM_GROUP, n)),
                            _block_diag(bbt_im.reshape(ncb, gl, SSM_GROUP, n))], axis=-1).astype(BF16)
    cblk = jnp.concatenate([_block_diag(c_re.reshape(ncb, gl, SSM_GROUP, n).transpose(0, 1, 3, 2)),
                            _block_diag(-c_im.reshape(ncb, gl, SSM_GROUP, n).transpose(0, 1, 3, 2))],
                           axis=1).astype(BF16)
    d_tab = d_skip.reshape(ncb, 1, SSM_CH_BLOCK)
    st_spec = pl.BlockSpec((1, SUBLANES, 2 * SSM_LANES), lambda c, t: (c, 0, 0))
    n_chunks = s // t_steps
    chunk = (t_steps, SUBLANES, SSM_CH_BLOCK)
    last = n_chunks - 1
    y_tm, ht = pl.pallas_call(
        functools.partial(_s5_kernel, t_steps=t_steps, n_chunks=n_chunks),
        grid=(ncb, n_chunks + S5_RING - 1),
        in_specs=[pl.BlockSpec(chunk, lambda c, t: (jnp.minimum(t, last), 0, c)),
                  pl.BlockSpec(chunk, lambda c, t: (jnp.clip(t - 2, 0, last), 0, c)),
                  st_spec, st_spec,
                  pl.BlockSpec((1, SSM_CH_BLOCK, 2 * SSM_LANES), lambda c, t: (c, 0, 0)),
                  pl.BlockSpec((1, 2 * SSM_LANES, SSM_CH_BLOCK), lambda c, t: (c, 0, 0)),
                  pl.BlockSpec((1, 1, SSM_CH_BLOCK), lambda c, t: (c, 0, 0))],
        out_specs=[pl.BlockSpec(chunk, lambda c, t: (jnp.clip(t - 2, 0, last), 0, c)), st_spec],
        out_shape=(jax.ShapeDtypeStruct((s, b, w), F32),
                   jax.ShapeDtypeStruct((ncb, SUBLANES, 2 * SSM_LANES), F32)),
        scratch_shapes=[pltpu.VMEM((t_steps * SUBLANES, 2 * SSM_LANES), F32)] * S5_RING
                       + [pltpu.VMEM((SUBLANES, 2 * SSM_LANES), F32)],
        compiler_params=_params("parallel", "arbitrary"),
        name="s5_scan",
    )(u_tm, u_tm, h0, a_tab, bblk, cblk, d_tab)

    def unlanes(x):
        return x.transpose(1, 0, 2).reshape(b, g, n)

    return y_tm, unlanes(ht[..., :SSM_LANES]), unlanes(ht[..., SSM_LANES:])


def _glu_kernel(y_ref, w_ref, b_ref, z_ref, o_ref):
    y = y_ref[...]
    gate = jnp.dot(y.astype(BF16), w_ref[...], preferred_element_type=F32) + b_ref[...]
    o_ref[...] = (y * jax.nn.sigmoid(gate) * jax.nn.silu(z_ref[...])).astype(o_ref.dtype)


def _glu_gate(y_tm, w_glu, l, b_glu, proj, z_col0, bsz, s):
    w = w_glu.shape[1]
    tm = _row_tile(s, 512)
    nt = s // tm
    zblk = z_col0 // w
    return pl.pallas_call(
        _glu_kernel,
        grid=(bsz, nt),
        in_specs=[pl.BlockSpec((tm, w), lambda b, t: (t, b)),
                  pl.BlockSpec((None, w, w), lambda b, t: (l, 0, 0)),
                  pl.BlockSpec((1, w), lambda b, t: (0, 0)),
                  pl.BlockSpec((tm, w), lambda b, t: (b * nt + t, zblk))],
        out_specs=pl.BlockSpec((tm, w), lambda b, t: (b * nt + t, 0)),
        out_shape=jax.ShapeDtypeStruct((bsz * s, w), BF16),
        compiler_params=_params("parallel", "arbitrary"),
        name="glu_gate",
    )(y_tm.reshape(s, bsz * w), w_glu, b_glu.reshape(1, w), proj)


def _head_rms(x, g):
    r = lax.rsqrt(jnp.mean(x * x, axis=-1, keepdims=True) + RMS_EPS)
    return x * r * g


def _qk(q, k):
    return lax.dot_general(q, k, (((1,), (1,)), ((), ())), preferred_element_type=F32)


def _band_prompt_kernel(q_ref, kp_ref, kc_ref, vp_ref, vc_ref, z_ref, bias_ref, qg_ref, kg_ref,
                        o_ref, kn_ref):
    i = pl.program_id(2)
    scale = 1.0 / math.sqrt(HEAD_DIM)
    qn = _head_rms(q_ref[0], qg_ref[...]).astype(BF16)
    kpn = _head_rms(kp_ref[0], kg_ref[...])
    kcn = _head_rms(kc_ref[0], kg_ref[...])
    tq = q_ref.shape[1]
    kn_ref[0] = kcn[tq - BAND_LEN:]
    k_all = jnp.concatenate([kpn.astype(BF16), kcn.astype(BF16)], axis=0)
    v_all = jnp.concatenate([vp_ref[0].astype(BF16), vc_ref[0].astype(BF16)], axis=0)
    bias = bias_ref[0]
    col = lax.broadcasted_iota(jnp.int32, (BAND_Q_SUB, BAND_WINDOW), 1)
    z = z_ref[0]
    n_sub = tq // BAND_Q_SUB
    rows = [slice(p * BAND_Q_SUB, (p + 1) * BAND_Q_SUB) for p in range(n_sub)]
    wins = [slice(p * BAND_Q_SUB, p * BAND_Q_SUB + BAND_WINDOW) for p in range(n_sub)]
    scores = []
    for p in range(n_sub):
        s = _qk(qn[rows[p]], k_all[wins[p]]) * scale + bias
        if p * BAND_Q_SUB < BAND_LEN:
            n_missing = jnp.where(i == 0, BAND_LEN - p * BAND_Q_SUB, 0)
            s = jnp.where(col < n_missing, NEG_INF, s)
        scores.append(s)
    probs = [jnp.exp(s - jnp.max(s, axis=-1, keepdims=True)) for s in scores]
    outs = [jnp.dot(pr.astype(BF16), v_all[wins[p]], preferred_element_type=F32) for p, pr in enumerate(probs)]
    for p in range(n_sub):
        l = jnp.sum(probs[p], axis=-1, keepdims=True)
        o_ref[0, rows[p], :] = ((outs[p] / l) * jax.nn.silu(z[rows[p]])).astype(o_ref.dtype)


def _static_take(tab, idx):
    nh = tab.shape[0]
    idx = np.asarray(idx)
    pieces = []
    i = 0
    while i < len(idx):
        step = int(idx[i + 1] - idx[i]) if i + 1 < len(idx) else 0
        if step not in (-1, 0, 1):
            step = 0
            j = i + 1
        else:
            j = i + 1
            while j < len(idx) and idx[j] - idx[j - 1] == step:
                j += 1
        n = j - i
        a = int(idx[i])
        if step == 0:
            pieces.append(jnp.broadcast_to(tab[:, a:a + 1], (nh, n)))
        elif step == 1:
            pieces.append(tab[:, a:a + n])
        else:
            pieces.append(tab[:, a - n + 1:a + 1][:, ::-1])
        i = j
    return jnp.concatenate(pieces, axis=1)


def _toeplitz_bias(rel_bias, rows, cols, offset):
    nh = rel_bias.shape[0]
    period = rows + cols
    m = np.arange(period)
    c_minus_r = np.where(m < cols, m, m - period)
    rel = np.clip(offset - c_minus_r, -MAX_REL, MAX_REL) + MAX_REL
    vec = _static_take(rel_bias.astype(F32), rel)
    return jnp.tile(vec, (1, rows))[:, :rows * (period - 1)].reshape(nh, rows, period - 1)[:, :, :cols]


def _band_bias_prompt(rel_bias):
    r = np.arange(BAND_Q_SUB)
    c = np.arange(BAND_WINDOW)
    jb = c[None, :] - (r // CHUNK * CHUNK)[:, None]
    inside = (jb >= 0) & (jb < (BAND_CHUNKS + 1) * CHUNK)
    toep = _toeplitz_bias(rel_bias, BAND_Q_SUB, BAND_WINDOW, BAND_LEN)
    return jnp.where(jnp.asarray(inside)[None], toep, NEG_INF)


def _band_prompt(proj3, q_col0, k_col0, v_col0, z_col0, rel_bias, qn_g, kn_g):
    bsz, s, _ = proj3.shape
    nh = rel_bias.shape[0]
    tq = _row_tile(s, BAND_Q_BLOCK, BAND_LEN)
    assert s % tq == 0 and tq % BAND_LEN == 0
    nq = s // tq
    per = tq // BAND_LEN
    qb, kb, vb, zb = (c // HEAD_DIM for c in (q_col0, k_col0, v_col0, z_col0))
    bias = _band_bias_prompt(rel_bias)
    blk = (1, tq, HEAD_DIM)
    prev = (1, BAND_LEN, HEAD_DIM)
    gspec = pl.BlockSpec((1, HEAD_DIM), lambda h, b, i: (0, 0))
    return pl.pallas_call(
        _band_prompt_kernel,
        grid=(nh, bsz, nq),
        in_specs=[pl.BlockSpec(blk, lambda h, b, i: (b, i, qb + h)),
                  pl.BlockSpec(prev, lambda h, b, i: (b, jnp.maximum(i * per - 1, 0), kb + h)),
                  pl.BlockSpec(blk, lambda h, b, i: (b, i, kb + h)),
                  pl.BlockSpec(prev, lambda h, b, i: (b, jnp.maximum(i * per - 1, 0), vb + h)),
                  pl.BlockSpec(blk, lambda h, b, i: (b, i, vb + h)),
                  pl.BlockSpec(blk, lambda h, b, i: (b, i, zb + h)),
                  pl.BlockSpec((1, BAND_Q_SUB, BAND_WINDOW), lambda h, b, i: (h, 0, 0)),
                  gspec, gspec],
        out_specs=[pl.BlockSpec(blk, lambda h, b, i: (b, i, h)),
                   pl.BlockSpec(prev, lambda h, b, i: (b, 0, h))],
        out_shape=(jax.ShapeDtypeStruct((bsz, s, nh * HEAD_DIM), BF16),
                   jax.ShapeDtypeStruct((bsz, BAND_LEN, nh * HEAD_DIM), F32)),
        compiler_params=_params("parallel", "parallel", "arbitrary"),
        name="band_prompt",
    )(proj3, proj3, proj3, proj3, proj3, proj3, bias, qn_g.reshape(1, HEAD_DIM), kn_g.reshape(1, HEAD_DIM))


def _copy_head_rows(head, srcs, dsts):
    n_rows = dsts[0].shape[0]
    nh = srcs[0].shape[1] // n_rows
    for hh in range(nh):
        @pl.when(head == hh)
        def _(hh=hh):
            for src, dst in zip(srcs, dsts):
                dst[...] = src[0, pl.ds(hh, n_rows, stride=nh), :]


def _band_sample_kernel(q_ref, k_ref, v_ref, ck_ref, cv_ref, z_ref, bias_ref, qg_ref, kg_ref,
                        o_ref, kn_ref, kbuf_ref, vbuf_ref, *, lc):
    scale = 1.0 / math.sqrt(HEAD_DIM)
    n = q_ref.shape[1]
    _copy_head_rows(pl.program_id(1), (ck_ref, cv_ref), (kbuf_ref, vbuf_ref))
    qn = _head_rms(q_ref[0], qg_ref[...]).astype(BF16)
    kn = _head_rms(k_ref[0], kg_ref[...])
    kn_ref[0] = kn
    zpad = jnp.zeros((LANES - n, HEAD_DIM), F32)
    kn = jnp.concatenate([kn, zpad], axis=0)
    v_new = jnp.concatenate([v_ref[0], zpad], axis=0)
    s_p = _qk(qn, kbuf_ref[...].astype(BF16)) * scale + bias_ref[0, :, :lc]
    s_c = _qk(qn, kn.astype(BF16)) * scale + bias_ref[0, :, lc:]
    m = jnp.maximum(jnp.max(s_p, axis=-1, keepdims=True), jnp.max(s_c, axis=-1, keepdims=True))
    p_p = jnp.exp(s_p - m)
    p_c = jnp.exp(s_c - m)
    l = jnp.sum(p_p, axis=-1, keepdims=True) + jnp.sum(p_c, axis=-1, keepdims=True)
    o = jnp.dot(p_p.astype(BF16), vbuf_ref[...].astype(BF16), preferred_element_type=F32)
    o = o + jnp.dot(p_c.astype(BF16), v_new.astype(BF16), preferred_element_type=F32)
    o_ref[0] = ((o / l) * jax.nn.silu(z_ref[0])).astype(o_ref.dtype)


def _band_sample(proj3, q_col0, k_col0, v_col0, z_col0, cache_k, cache_v, l, rel_bias, qn_g, kn_g):
    bsz, n, _ = proj3.shape
    lc = cache_k.shape[2]
    nh = rel_bias.shape[0]
    rows = lambda t: t.reshape(*t.shape[:2], lc * nh, HEAD_DIM)
    cache_k, cache_v = rows(cache_k), rows(cache_v)
    qb, kb, vb, zb = (c // HEAD_DIM for c in (q_col0, k_col0, v_col0, z_col0))
    bias = _toeplitz_bias(rel_bias, n, lc + n, lc)
    assert n <= LANES
    bias = jnp.pad(bias, ((0, 0), (0, 0), (0, LANES - n)), constant_values=NEG_INF)
    blk = (1, n, HEAD_DIM)
    cblk = (None, 1, lc * nh, HEAD_DIM)
    gspec = pl.BlockSpec((1, HEAD_DIM), lambda b, h: (0, 0))
    return pl.pallas_call(
        functools.partial(_band_sample_kernel, lc=lc),
        grid=(bsz, nh),
        in_specs=[pl.BlockSpec(blk, lambda b, h: (b, 0, qb + h)),
                  pl.BlockSpec(blk, lambda b, h: (b, 0, kb + h)),
                  pl.BlockSpec(blk, lambda b, h: (b, 0, vb + h)),
                  pl.BlockSpec(cblk, lambda b, h: (l, b, 0, 0)),
                  pl.BlockSpec(cblk, lambda b, h: (l, b, 0, 0)),
                  pl.BlockSpec(blk, lambda b, h: (b, 0, zb + h)),
                  pl.BlockSpec((1, n, lc + LANES), lambda b, h: (h, 0, 0)),
                  gspec, gspec],
        out_specs=[pl.BlockSpec(blk, lambda b, h: (b, 0, h)),
                   pl.BlockSpec(blk, lambda b, h: (b, 0, h))],
        out_shape=(jax.ShapeDtypeStruct((bsz, n, nh * HEAD_DIM), BF16),
                   jax.ShapeDtypeStruct((bsz, n, nh * HEAD_DIM), F32)),
        scratch_shapes=[pltpu.VMEM((lc, HEAD_DIM), F32)] * 2,
        compiler_params=_params("parallel", "arbitrary"),
        name="band_sample",
    )(proj3, proj3, proj3, cache_k, cache_v, proj3, bias, qn_g.reshape(1, HEAD_DIM), kn_g.reshape(1, HEAD_DIM))


def _suffix_matrix():
    kb = SB_K_BLOCK
    j = lax.broadcasted_iota(jnp.int32, (2 * kb, 2 * kb), 0) % kb
    c = lax.broadcasted_iota(jnp.int32, (2 * kb, 2 * kb), 1)
    return jnp.where((c >= kb) | (j >= c), -1.0, 0.0).astype(BF16)


def _sb_groups(q, groups, carry, acc, umat):
    scale = 1.0 / math.sqrt(HEAD_DIM)
    kb = SB_K_BLOCK
    zs = [_qk(q, k.astype(BF16)) * scale for k, _, _ in groups]
    fails = [jnp.maximum(z, 0.0) + jnp.log(1.0 + jnp.exp(-jnp.abs(z))) for z in zs]
    fails = [f if causal is None else jnp.where(causal, f, 0.0) for f, (_, _, causal) in zip(fails, groups)]
    his = [f.astype(BF16) for f in fails]
    los = [(f - hi.astype(F32)).astype(BF16) for f, hi in zip(fails, his)]
    staged = []
    for z, hi, lo, (_, _, causal) in zip(zs, his, los, groups):
        parts = []
        for b in reversed(range(SB_GROUP // kb)):
            cols = slice(b * kb, (b + 1) * kb)
            sums = jnp.dot(jnp.concatenate([hi[:, cols], lo[:, cols]], axis=1), umat, preferred_element_type=F32)
            parts.append((b, z[:, cols] + sums[:, :kb], sums[:, kb:]))
        staged.append((parts, causal))
    ws = []
    for parts, causal in staged:
        log_w = [None] * len(parts)
        for b, part, total in parts:
            log_w[b] = part + carry
            carry = carry + total
        w = jnp.exp(jnp.concatenate(log_w, axis=1))
        if causal is not None:
            w = jnp.where(causal, w, 0.0)
        ws.append(w.astype(BF16))
    vs = [v.astype(BF16) for _, v, _ in groups]
    w_all = ws[0] if len(ws) == 1 else jnp.concatenate(ws, axis=1)
    v_all = vs[0] if len(vs) == 1 else jnp.concatenate(vs, axis=0)
    return carry, acc + jnp.dot(w_all, v_all, preferred_element_type=F32)


def _sb_prompt_kernel(q_ref, k_ref, v_ref, z_ref, o_ref):
    qi = pl.program_id(2)
    tq = SB_Q_BLOCK
    q = q_ref[0].astype(BF16)
    umat = _suffix_matrix()
    q0 = qi * tq
    carry = jnp.zeros((tq, SB_K_BLOCK), F32)
    acc = jnp.zeros((tq, HEAD_DIM), F32)

    def load(k0, causal):
        k0 = pl.multiple_of(k0, SB_GROUP)
        return k_ref[0, pl.ds(k0, SB_GROUP), :], v_ref[0, pl.ds(k0, SB_GROUP), :], causal

    n_diag = tq // SB_GROUP
    assert n_diag % SB_PROMPT_UNROLL == 0
    for d in reversed(range(n_diag)):
        r0 = d * SB_GROUP
        k, v, _ = load(q0 + r0, None)
        causal = (lax.broadcasted_iota(jnp.int32, (tq - r0, SB_GROUP), 1)
                  < lax.broadcasted_iota(jnp.int32, (tq - r0, SB_GROUP), 0))
        c_lo, a_lo = _sb_groups(q[r0:], [(k, v, causal)], carry[r0:], acc[r0:], umat)
        carry = c_lo if d == 0 else jnp.concatenate([carry[:r0], c_lo], axis=0)
        acc = a_lo if d == 0 else jnp.concatenate([acc[:r0], a_lo], axis=0)

    def body(j, ca):
        groups = [load(q0 - (j * SB_PROMPT_UNROLL + u + 1) * SB_GROUP, None) for u in range(SB_PROMPT_UNROLL)]
        return _sb_groups(q, groups, ca[0], ca[1], umat)

    carry, acc = lax.fori_loop(0, qi * (n_diag // SB_PROMPT_UNROLL), body, (carry, acc))
    o_ref[0] = (acc * jax.nn.silu(z_ref[0])).astype(o_ref.dtype)


def _sb_prompt(proj3, q_col0, k_col0, v_col0, z_col0, nh):
    bsz, s, _ = proj3.shape
    tq = _row_tile(s, SB_Q_BLOCK)
    assert tq == SB_Q_BLOCK
    qb, kb, vb, zb = (c // HEAD_DIM for c in (q_col0, k_col0, v_col0, z_col0))
    blk = (1, tq, HEAD_DIM)
    full = (1, s, HEAD_DIM)
    return pl.pallas_call(
        _sb_prompt_kernel,
        grid=(bsz, nh, s // tq),
        in_specs=[pl.BlockSpec(blk, lambda b, h, i: (b, i, qb + h)),
                  pl.BlockSpec(full, lambda b, h, i: (b, 0, kb + h)),
                  pl.BlockSpec(full, lambda b, h, i: (b, 0, vb + h)),
                  pl.BlockSpec(blk, lambda b, h, i: (b, i, zb + h))],
        out_specs=pl.BlockSpec(blk, lambda b, h, i: (b, i, h)),
        out_shape=jax.ShapeDtypeStruct((bsz, s, nh * HEAD_DIM), BF16),
        compiler_params=_params("parallel", "parallel", "arbitrary"),
        name="sb_prompt",
    )(proj3, proj3, proj3, proj3)


def _sb_sample_kernel(q_ref, k_ref, v_ref, ck_ref, cv_ref, z_ref, o_ref, kbuf_ref, vbuf_ref, *, past_len):
    n = q_ref.shape[1]
    _copy_head_rows(pl.program_id(1), (ck_ref, cv_ref), (kbuf_ref, vbuf_ref))

    q = q_ref[0].astype(BF16)
    umat = _suffix_matrix()
    row = lax.broadcasted_iota(jnp.int32, (n, SB_GROUP), 0)
    col = lax.broadcasted_iota(jnp.int32, (n, SB_GROUP), 1)
    carry = jnp.zeros((n, SB_K_BLOCK), F32)
    acc = jnp.zeros((n, HEAD_DIM), F32)
    n_past = past_len // SB_GROUP
    unroll = math.gcd(n_past, SB_SAMPLE_UNROLL)

    def past_groups(first):
        groups = []
        for u in range(unroll):
            k0 = pl.multiple_of(past_len - (first + u + 1) * SB_GROUP, SB_GROUP)
            groups.append((kbuf_ref[pl.ds(k0, SB_GROUP), :], vbuf_ref[pl.ds(k0, SB_GROUP), :], None))
        return groups

    carry, acc = _sb_groups(q, [(k_ref[0], v_ref[0], col < row)] + past_groups(0), carry, acc, umat)
    carry, acc = lax.fori_loop(1, n_past // unroll, lambda j, ca: _sb_groups(q, past_groups(j * unroll), *ca, umat),
                               (carry, acc))
    o_ref[0] = (acc * jax.nn.silu(z_ref[0])).astype(o_ref.dtype)


def _sb_sample(proj3, q_col0, z_col0, k_new, v_new, cache_k, cache_v, l, nh):
    bsz, n, _ = proj3.shape
    past_len = cache_k.shape[2]
    assert past_len % SB_GROUP == 0 and n <= SB_GROUP
    qb, zb = q_col0 // HEAD_DIM, z_col0 // HEAD_DIM
    pad = ((0, 0), (0, SB_GROUP - n), (0, 0))
    k_pad = jnp.pad(k_new, pad)
    v_pad = jnp.pad(v_new, pad)
    blk = (1, n, HEAD_DIM)
    own = (1, SB_GROUP, HEAD_DIM)
    past = (None, 1, past_len * nh, HEAD_DIM)
    rows = lambda t: t.reshape(*t.shape[:2], past_len * nh, HEAD_DIM)
    cache_k, cache_v = rows(cache_k), rows(cache_v)
    return pl.pallas_call(
        functools.partial(_sb_sample_kernel, past_len=past_len),
        grid=(bsz, nh),
        in_specs=[pl.BlockSpec(blk, lambda b, h: (b, 0, qb + h)),
                  pl.BlockSpec(own, lambda b, h: (b, 0, h)),
                  pl.BlockSpec(own, lambda b, h: (b, 0, h)),
                  pl.BlockSpec(past, lambda b, h: (l, b, 0, 0)),
                  pl.BlockSpec(past, lambda b, h: (l, b, 0, 0)),
                  pl.BlockSpec(blk, lambda b, h: (b, 0, zb + h))],
        out_specs=pl.BlockSpec(blk, lambda b, h: (b, 0, h)),
        out_shape=jax.ShapeDtypeStruct((bsz, n, nh * HEAD_DIM), BF16),
        scratch_shapes=[pltpu.VMEM((past_len, HEAD_DIM), F32)] * 2,
        compiler_params=_params("parallel", "arbitrary"),
        name="sb_sample",
    )(proj3, k_pad, v_pad, cache_k, cache_v, proj3)


def _layer(x, l, w, cache):
    (norm_g, w_in, ssm, w_glu, b_glu, qn_g, kn_g, rel_bias, w_br_a, w_br_b, w_br_c, gate_b, w_out) = w
    ab_re, ab_im, bbt_re, bbt_im, c_re, c_im, d_skip = ssm
    bsz, s, d = x.shape
    mw = w_glu.shape[1]
    nh = mw // HEAD_DIM
    g = ab_re.shape[0]
    m = bsz * s
    x2 = x.reshape(m, d)
    hx = _rmsnorm(x2, norm_g)
    proj = _matmul(hx, w_in, l)
    n_in = proj.shape[1]
    proj3 = proj.reshape(bsz, s, n_in)
    col = lambda i: i * mw
    u_tm = proj3[:, :, :mw].transpose(1, 0, 2)
    if cache is None:
        h0_re = jnp.zeros((bsz, g, SSM_STATE), F32)
        h0_im = h0_re
    else:
        h0_re, h0_im = cache[4], cache[5]
    y_tm, ht_re, ht_im = _s5_branch(u_tm, h0_re, h0_im, ab_re, ab_im, bbt_re, bbt_im, c_re, c_im, d_skip)
    lhs_a = _glu_gate(y_tm, w_glu, l, b_glu, proj, col(1), bsz, s)
    k_c = proj3[:, :, col(7):col(8)]
    v_c = proj3[:, :, col(8):col(9)]
    v_b = proj3[:, :, col(4):col(5)]
    if cache is None:
        lhs_b, band_k = _band_prompt(proj3, col(2), col(3), col(4), col(5), rel_bias, qn_g, kn_g)
        keep = min(BAND_LEN, s)
        band_v = v_b[:, s - keep:]
        lhs_c = _sb_prompt(proj3, col(6), col(7), col(8), col(9), nh)
    else:
        sb_k, sb_v, bk, bv = cache[:4]
        lhs_b, band_k = _band_sample(proj3, col(2), col(3), col(4), col(5), bk, bv, l, rel_bias, qn_g, kn_g)
        band_v = v_b
        lhs_c = _sb_sample(proj3, col(6), col(9), k_c, v_c, sb_k, sb_v, l, nh)
    mixed = _branch_merge(lhs_a, lhs_b.reshape(m, mw), lhs_c.reshape(m, mw), w_br_a, w_br_b, w_br_c, l,
                          proj, gate_b, col(10))
    y = _matmul_residual(mixed, w_out, l, x2).reshape(bsz, s, d)
    heads = lambda t: t.reshape(bsz, t.shape[1], nh, HEAD_DIM)
    return y, (heads(k_c), heads(v_c), heads(band_k), heads(band_v), ht_re, ht_im)


def kernel(x_prompt, x_sample, cache_sb_k, cache_sb_v, cache_band_k, cache_band_v, state_ssm_re, state_ssm_im, norm_g, w_in, ssm_a_re, ssm_a_im, ssm_log_dt, ssm_b_re, ssm_b_im, ssm_c_re, ssm_c_im, ssm_d, w_glu, b_glu, q_norm_g, k_norm_g, rel_bias, w_br_a, w_br_b, w_br_c, gate_b, w_out):
    depth = w_in.shape[0]
    yp, ys = x_prompt, x_sample
    p_states, s_states = [], []
    w_in, w_glu, w_br_a, w_br_b, w_br_c, w_out = (t.astype(BF16) for t in (w_in, w_glu, w_br_a, w_br_b, w_br_c, w_out))
    caches = (cache_sb_k, cache_sb_v, cache_band_k, cache_band_v)
    for l in range(depth):
        ssm = _discretize(ssm_a_re[l], ssm_a_im[l], ssm_log_dt[l], ssm_b_re[l], ssm_b_im[l])
        w = (norm_g[l], w_in, (*ssm, ssm_c_re[l], ssm_c_im[l], ssm_d[l]), w_glu, b_glu[l], q_norm_g[l], k_norm_g[l],
             rel_bias[l], w_br_a, w_br_b, w_br_c, gate_b[l], w_out)
        yp, sp = _layer(yp, l, w, None)
        ys, ss = _layer(ys, l, w, (*caches, state_ssm_re[l], state_ssm_im[l]))
        p_states.append(sp)
        s_states.append(ss)
    stk = lambda states, i: jnp.stack([st[i] for st in states], axis=0)
    return (yp, ys,
            *(stk(p_states, i) for i in range(6)),
            *(stk(s_states, i) for i in range(6)))
```

```python
import functools
import math

import jax
import jax.numpy as jnp
import numpy as np
from jax import lax
from jax.experimental import pallas as pl
from jax.experimental.pallas import tpu as pltpu

F32 = jnp.float32
BF16 = jnp.bfloat16

CHUNK = 64
HEAD_DIM = 128
SSM_GROUP = 16
SSM_STATE = 64
BAND_CHUNKS = 8
BAND_LEN = BAND_CHUNKS * CHUNK
MAX_REL = 128
N_BRANCH = 3
RMS_EPS = 1e-6
NEG_INF = -1e30

LANES = 128
SUBLANES = 8
VMEM_LIMIT_BYTES = 56 * 1024 * 1024
IN_PROJ_VMEM_LIMIT_BYTES = 60 * 1024 * 1024
MM_TILE = 1024
SSM_CH_BLOCK = 128
SSM_GROUPS_PER_BLOCK = SSM_CH_BLOCK // SSM_GROUP
SSM_LANES = SSM_GROUPS_PER_BLOCK * SSM_STATE
S5_RING = 3
SB_Q_BLOCK = 512
SB_K_BLOCK = 128
SB_GROUP = 2 * SB_K_BLOCK
SB_PROMPT_UNROLL = 2
SB_SAMPLE_UNROLL = 4
BAND_Q_BLOCK = 2 * BAND_LEN
BAND_Q_SUB = 2 * CHUNK
BAND_WINDOW = BAND_LEN + BAND_Q_SUB


def _params(*sem):
    return pltpu.CompilerParams(dimension_semantics=sem, vmem_limit_bytes=VMEM_LIMIT_BYTES)


def _row_tile(m, target, align=SUBLANES):
    if m <= target:
        return m
    for t in range(target - target % align, 0, -align):
        if m % t == 0:
            return t
    raise ValueError((m, target, align))


def _rmsnorm_kernel(x_ref, g_ref, o_ref):
    x = x_ref[...]
    r = lax.rsqrt(jnp.mean(x * x, axis=-1, keepdims=True) + RMS_EPS)
    o_ref[...] = (x * r * g_ref[...]).astype(o_ref.dtype)


def _rmsnorm(x, g):
    m, d = x.shape
    tm = _row_tile(m, 512)
    return pl.pallas_call(
        _rmsnorm_kernel,
        grid=(m // tm,),
        in_specs=[pl.BlockSpec((tm, d), lambda i: (i, 0)),
                  pl.BlockSpec((1, d), lambda i: (0, 0))],
        out_specs=pl.BlockSpec((tm, d), lambda i: (i, 0)),
        out_shape=jax.ShapeDtypeStruct((m, d), BF16),
        compiler_params=_params("parallel"),
        name="rmsnorm",
    )(x, g.reshape(1, d))


def _mm_kernel(x_ref, w_ref, o_ref):
    o_ref[...] = jnp.dot(x_ref[...], w_ref[...], preferred_element_type=F32).astype(o_ref.dtype)


def _mm_heads_kernel(x_ref, w_ref, o_ref, kv_ref, *, kv_block0):
    acc = jnp.dot(x_ref[...], w_ref[...], preferred_element_type=F32)
    o_ref[...] = acc
    j = pl.program_id(1)

    @pl.when((j == kv_block0) | (j == kv_block0 + 1))
    def _():
        kv_ref[...] = acc.reshape(kv_ref.shape)


def _in_proj_with_heads(x, w, l, kv_col0, nh):
    m, k = x.shape
    _, _, n = w.shape
    tm = _row_tile(m, MM_TILE)
    tn = nh * HEAD_DIM
    assert n % tn == 0 and kv_col0 % tn == 0
    b0 = kv_col0 // tn
    return pl.pallas_call(
        functools.partial(_mm_heads_kernel, kv_block0=b0),
        grid=(m // tm, n // tn),
        in_specs=[pl.BlockSpec((tm, k), lambda i, j: (i, 0)),
                  pl.BlockSpec((None, k, tn), lambda i, j: (l, 0, j))],
        out_specs=[pl.BlockSpec((tm, tn), lambda i, j: (i, j)),
                   pl.BlockSpec((None, tm, nh, HEAD_DIM), lambda i, j: (jnp.clip(j - b0, 0, 1), i, 0, 0))],
        out_shape=(jax.ShapeDtypeStruct((m, n), F32),
                   jax.ShapeDtypeStruct((2, m, nh, HEAD_DIM), F32)),
        compiler_params=pltpu.CompilerParams(dimension_semantics=("parallel", "arbitrary"),
                                             vmem_limit_bytes=IN_PROJ_VMEM_LIMIT_BYTES),
        name="in_proj",
    )(x, w)


def _matmul(x, w, l, out_dtype=F32):
    m, k = x.shape
    _, _, n = w.shape
    tm = _row_tile(m, MM_TILE)
    tn = _row_tile(n, MM_TILE, LANES)
    return pl.pallas_call(
        _mm_kernel,
        grid=(m // tm, n // tn),
        in_specs=[pl.BlockSpec((tm, k), lambda i, j: (i, 0)),
                  pl.BlockSpec((None, k, tn), lambda i, j: (l, 0, j))],
        out_specs=pl.BlockSpec((tm, tn), lambda i, j: (i, j)),
        out_shape=jax.ShapeDtypeStruct((m, n), out_dtype),
        compiler_params=_params("parallel", "arbitrary"),
        name="in_proj",
    )(x, w)


def _mm_res_kernel(x_ref, w_ref, r_ref, o_ref):
    o_ref[...] = r_ref[...] + jnp.dot(x_ref[...], w_ref[...], preferred_element_type=F32)


def _matmul_residual(x, w, l, res):
    m, k = x.shape
    _, _, n = w.shape
    tm = _row_tile(m, MM_TILE)
    tn = _row_tile(n, MM_TILE, LANES)
    return pl.pallas_call(
        _mm_res_kernel,
        grid=(m // tm, n // tn),
        in_specs=[pl.BlockSpec((tm, k), lambda i, j: (i, 0)),
                  pl.BlockSpec((None, k, tn), lambda i, j: (l, 0, j)),
                  pl.BlockSpec((tm, tn), lambda i, j: (i, j))],
        out_specs=pl.BlockSpec((tm, tn), lambda i, j: (i, j)),
        out_shape=jax.ShapeDtypeStruct((m, n), F32),
        compiler_params=_params("parallel", "arbitrary"),
        name="out_proj",
    )(x, w, res)


def _branch_kernel(la_ref, lb_ref, lc_ref, wa_ref, wb_ref, wc_ref,
                   ga_ref, gb_ref, gc_ref, ba_ref, bb_ref, bc_ref, o_ref):
    def one(l_ref, w_ref, g_ref, b_ref):
        o = jnp.dot(l_ref[...], w_ref[...], preferred_element_type=F32)
        return jax.nn.sigmoid(g_ref[...] + b_ref[...]) * o
    mixed = one(la_ref, wa_ref, ga_ref, ba_ref) + one(lb_ref, wb_ref, gb_ref, bb_ref)
    mixed = mixed + one(lc_ref, wc_ref, gc_ref, bc_ref)
    o_ref[...] = mixed.astype(o_ref.dtype)


def _branch_merge(lhs_a, lhs_b, lhs_c, w_a, w_b, w_c, l, proj, gate_b, g_col0):
    m, kw = lhs_a.shape
    d = w_a.shape[2]
    tm = _row_tile(m, 512)
    tn = _row_tile(math.gcd(d, g_col0), MM_TILE, LANES)
    assert g_col0 % tn == 0 and d % tn == 0
    gblk = g_col0 // tn
    nblk = d // tn
    lspec = pl.BlockSpec((tm, kw), lambda j, i: (i, 0))
    wspec = pl.BlockSpec((None, kw, tn), lambda j, i: (l, 0, j))
    gspecs = [pl.BlockSpec((tm, tn), functools.partial(lambda j, i, br: (i, gblk + br * nblk + j), br=br))
              for br in range(N_BRANCH)]
    bspecs = [pl.BlockSpec((1, tn), functools.partial(lambda j, i, br: (0, br * nblk + j), br=br))
              for br in range(N_BRANCH)]
    gb = gate_b.reshape(1, N_BRANCH * d)
    return pl.pallas_call(
        _branch_kernel,
        grid=(d // tn, m // tm),
        in_specs=[lspec, lspec, lspec, wspec, wspec, wspec, *gspecs, *bspecs],
        out_specs=pl.BlockSpec((tm, tn), lambda j, i: (i, j)),
        out_shape=jax.ShapeDtypeStruct((m, d), BF16),
        compiler_params=_params("parallel", "arbitrary"),
        name="branch_merge",
    )(lhs_a, lhs_b, lhs_c, w_a, w_b, w_c, proj, proj, proj, gb, gb, gb)


def _discretize_kernel(are_ref, aim_ref, ldt_ref, bre_ref, bim_ref,
                       abre_ref, abim_ref, bbre_ref, bbim_ref):
    a_re = are_ref[...]
    a_im = aim_ref[...]
    dt = jnp.exp(ldt_ref[...])
    mag = jnp.exp(a_re * dt)
    ab_re = mag * jnp.cos(a_im * dt)
    ab_im = mag * jnp.sin(a_im * dt)
    den = a_re * a_re + a_im * a_im
    nr = ab_re - 1.0
    k_re = (nr * a_re + ab_im * a_im) / den
    k_im = (ab_im * a_re - nr * a_im) / den
    abre_ref[...] = ab_re
    abim_ref[...] = ab_im
    b_re = bre_ref[...]
    b_im = bim_ref[...]
    bbre_ref[...] = k_re[:, None, :] * b_re - k_im[:, None, :] * b_im
    bbim_ref[...] = k_re[:, None, :] * b_im + k_im[:, None, :] * b_re


def _discretize(a_re, a_im, log_dt, b_re, b_im):
    g, n = a_re.shape
    p = b_re.shape[-1]
    gn = jax.ShapeDtypeStruct((g, n), F32)
    gpn = jax.ShapeDtypeStruct((g, p, n), F32)
    return pl.pallas_call(
        _discretize_kernel,
        out_shape=(gn, gn, gpn, gpn),
        name="s5_discretize",
    )(a_re, a_im, log_dt.reshape(g, 1), b_re.transpose(0, 2, 1), b_im.transpose(0, 2, 1))


def _block_diag(x):
    ncb, gl, r, c = x.shape
    eye = jnp.eye(gl, dtype=x.dtype)
    out = x[:, :, :, None, :] * eye[None, :, None, :, None]
    return out.reshape(ncb, gl * r, gl * c)


def _s5_kernel(u_ref, up_ref, h0_ref, a_ref, bblk_ref, cblk_ref, d_ref, y_ref, ht_ref, *scratch,
               t_steps, n_chunks):
    ring, h_ref = scratch[:S5_RING], scratch[S5_RING]
    t_idx = pl.program_id(1)

    @pl.when(t_idx == 0)
    def _():
        h_ref[...] = h0_ref[0]
        for buf in ring:
            buf[...] = jnp.zeros(buf.shape, F32)

    rows = t_steps * SUBLANES

    def stages(in_buf, scan_buf, out_buf):
        u = u_ref[...].reshape(rows, SSM_CH_BLOCK)
        in_buf[...] = jnp.dot(u.astype(BF16), bblk_ref[0], preferred_element_type=F32)

        u_out = up_ref[...].reshape(rows, SSM_CH_BLOCK)
        y = jnp.dot(out_buf[...].astype(BF16), cblk_ref[0], preferred_element_type=F32)
        y = y + d_ref[0] * u_out
        y_ref[...] = jax.nn.gelu(y).reshape(t_steps, SUBLANES, SSM_CH_BLOCK)

        a_re = a_ref[0, :, :SSM_LANES]
        a_im = a_ref[0, :, SSM_LANES:]
        h_re = h_ref[:, :SSM_LANES]
        h_im = h_ref[:, SSM_LANES:]
        s_re, s_im = h_re, h_im
        for t in range(t_steps):
            row = slice(t * SUBLANES, (t + 1) * SUBLANES)
            b_re = scan_buf[row, :SSM_LANES]
            b_im = scan_buf[row, SSM_LANES:]
            s_re, s_im = a_re * s_re - a_im * s_im + b_re, a_re * s_im + a_im * s_re + b_im
            scan_buf[row, :SSM_LANES] = s_re
            scan_buf[row, SSM_LANES:] = s_im
        in_range = (t_idx >= 1) & (t_idx <= n_chunks)
        h_ref[:, :SSM_LANES] = jnp.where(in_range, s_re, h_re)
        h_ref[:, SSM_LANES:] = jnp.where(in_range, s_im, h_im)
        ht_ref[0] = h_ref[...]

    for k in range(S5_RING):
        pl.when(t_idx % S5_RING == k)(functools.partial(
            stages, ring[k], ring[(k + S5_RING - 1) % S5_RING], ring[(k + S5_RING - 2) % S5_RING]))


def _s5_branch(u_tm, h0_re, h0_im, ab_re, ab_im, bbt_re, bbt_im, c_re, c_im, d_skip):
    s, b, w = u_tm.shape
    assert b == SUBLANES, "the scan keeps one stream per sublane"
    g, n = ab_re.shape
    assert n == SSM_STATE and w == g * SSM_GROUP and w % SSM_CH_BLOCK == 0
    ncb = w // SSM_CH_BLOCK
    gl = SSM_GROUPS_PER_BLOCK
    t_steps = _row_tile(s, 128)

    def lanes(x):
        r = x.shape[0]
        return x.reshape(r, ncb, gl * n).transpose(1, 0, 2)

    h0 = jnp.concatenate([lanes(h0_re), lanes(h0_im)], axis=-1)
    a_tab = jnp.concatenate([lanes(jnp.broadcast_to(ab_re[None], (SUBLANES, g, n))),
                             lanes(jnp.broadcast_to(ab_im[None], (SUBLANES, g, n)))], axis=-1)
    bblk = jnp.concatenate([_block_diag(bbt_re.reshape(ncb, gl, SSM_GROUP, n)),
                            _block_diag(bbt_im.reshape(ncb, gl, SSM_GROUP, n))], axis=-1).astype(BF16)
    cblk = jnp.concatenate([_block_diag(c_re.reshape(ncb, gl, SSM_GROUP, n).transpose(0, 1, 3, 2)),
                            _block_diag(-c_im.reshape(ncb, gl, SSM_GROUP, n).transpose(0, 1, 3, 2))],
                           axis=1).astype(BF16)
    d_tab = d_skip.reshape(ncb, 1, SSM_CH_BLOCK)
    st_spec = pl.BlockSpec((1, SUBLANES, 2 * SSM_LANES), lambda c, t: (c, 0, 0))
    n_chunks = s // t_steps
    chunk = (t_steps, SUBLANES, SSM_CH_BLOCK)
    last = n_chunks - 1
    y_tm, ht = pl.pallas_call(
        functools.partial(_s5_kernel, t_steps=t_steps, n_chunks=n_chunks),
        grid=(ncb, n_chunks + S5_RING - 1),
        in_specs=[pl.BlockSpec(chunk, lambda c, t: (jnp.minimum(t, last), 0, c)),
                  pl.BlockSpec(chunk, lambda c, t: (jnp.clip(t - 2, 0, last), 0, c)),
                  st_spec, st_spec,
                  pl.BlockSpec((1, SSM_CH_BLOCK, 2 * SSM_LANES), lambda c, t: (c, 0, 0)),
                  pl.BlockSpec((1, 2 * SSM_LANES, SSM_CH_BLOCK), lambda c, t: (c, 0, 0)),
                  pl.BlockSpec((1, 1, SSM_CH_BLOCK), lambda c, t: (c, 0, 0))],
        out_specs=[pl.BlockSpec(chunk, lambda c, t: (jnp.clip(t - 2, 0, last), 0, c)), st_spec],
        out_shape=(jax.ShapeDtypeStruct((s, b, w), F32),
                   jax.ShapeDtypeStruct((ncb, SUBLANES, 2 * SSM_LANES), F32)),
        scratch_shapes=[pltpu.VMEM((t_steps * SUBLANES, 2 * SSM_LANES), F32)] * S5_RING
                       + [pltpu.VMEM((SUBLANES, 2 * SSM_LANES), F32)],
        compiler_params=_params("parallel", "arbitrary"),
        name="s5_scan",
    )(u_tm, u_tm, h0, a_tab, bblk, cblk, d_tab)

    def unlanes(x):
        return x.transpose(1, 0, 2).reshape(b, g, n)

    return y_tm, unlanes(ht[..., :SSM_LANES]), unlanes(ht[..., SSM_LANES:])


def _glu_kernel(y_ref, w_ref, b_ref, z_ref, o_ref):
    y = y_ref[...]
    gate = jnp.dot(y.astype(BF16), w_ref[...], preferred_element_type=F32) + b_ref[...]
    o_ref[...] = (y * jax.nn.sigmoid(gate) * jax.nn.silu(z_ref[...])).astype(o_ref.dtype)


def _glu_gate(y_tm, w_glu, l, b_glu, proj, z_col0, bsz, s):
    w = w_glu.shape[1]
    tm = _row_tile(s, 512)
    nt = s // tm
    zblk = z_col0 // w
    return pl.pallas_call(
        _glu_kernel,
        grid=(bsz, nt),
        in_specs=[pl.BlockSpec((tm, w), lambda b, t: (t, b)),
                  pl.BlockSpec((None, w, w), lambda b, t: (l, 0, 0)),
                  pl.BlockSpec((1, w), lambda b, t: (0, 0)),
                  pl.BlockSpec((tm, w), lambda b, t: (b * nt + t, zblk))],
        out_specs=pl.BlockSpec((tm, w), lambda b, t: (b * nt + t, 0)),
        out_shape=jax.ShapeDtypeStruct((bsz * s, w), BF16),
        compiler_params=_params("parallel", "arbitrary"),
        name="glu_gate",
    )(y_tm.reshape(s, bsz * w), w_glu, b_glu.reshape(1, w), proj)


def _head_rms(x, g):
    r = lax.rsqrt(jnp.mean(x * x, axis=-1, keepdims=True) + RMS_EPS)
    return x * r * g


def _qk(q, k):
    return lax.dot_general(q, k, (((1,), (1,)), ((), ())), preferred_element_type=F32)


def _band_prompt_kernel(q_ref, kp_ref, kc_ref, vp_ref, vc_ref, z_ref, bias_ref, qg_ref, kg_ref,
                        o_ref, kn_ref):
    i = pl.program_id(2)
    scale = 1.0 / math.sqrt(HEAD_DIM)
    qn = _head_rms(q_ref[0], qg_ref[...]).astype(BF16)
    kpn = _head_rms(kp_ref[0], kg_ref[...])
    kcn = _head_rms(kc_ref[0], kg_ref[...])
    tq = q_ref.shape[1]
    kn_ref[0] = kcn[tq - BAND_LEN:]
    k_all = jnp.concatenate([kpn.astype(BF16), kcn.astype(BF16)], axis=0)
    v_all = jnp.concatenate([vp_ref[0].astype(BF16), vc_ref[0].astype(BF16)], axis=0)
    bias = bias_ref[0]
    col = lax.broadcasted_iota(jnp.int32, (BAND_Q_SUB, BAND_WINDOW), 1)
    z = z_ref[0]
    n_sub = tq // BAND_Q_SUB
    rows = [slice(p * BAND_Q_SUB, (p + 1) * BAND_Q_SUB) for p in range(n_sub)]
    wins = [slice(p * BAND_Q_SUB, p * BAND_Q_SUB + BAND_WINDOW) for p in range(n_sub)]
    scores = []
    for p in range(n_sub):
        s = _qk(qn[rows[p]], k_all[wins[p]]) * scale + bias
        if p * BAND_Q_SUB < BAND_LEN:
            n_missing = jnp.where(i == 0, BAND_LEN - p * BAND_Q_SUB, 0)
            s = jnp.where(col < n_missing, NEG_INF, s)
        scores.append(s)
    probs = [jnp.exp(s - jnp.max(s, axis=-1, keepdims=True)) for s in scores]
    outs = [jnp.dot(pr.astype(BF16), v_all[wins[p]], preferred_element_type=F32) for p, pr in enumerate(probs)]
    for p in range(n_sub):
        l = jnp.sum(probs[p], axis=-1, keepdims=True)
        o_ref[0, rows[p], :] = ((outs[p] / l) * jax.nn.silu(z[rows[p]])).astype(o_ref.dtype)


def _static_take(tab, idx):
    nh = tab.shape[0]
    idx = np.asarray(idx)
    pieces = []
    i = 0
    while i < len(idx):
        step = int(idx[i + 1] - idx[i]) if i + 1 < len(idx) else 0
        if step not in (-1, 0, 1):
            step = 0
            j = i + 1
        else:
            j = i + 1
            while j < len(idx) and idx[j] - idx[j - 1] == step:
                j += 1
        n = j - i
        a = int(idx[i])
        if step == 0:
            pieces.append(jnp.broadcast_to(tab[:, a:a + 1], (nh, n)))
        elif step == 1:
            pieces.append(tab[:, a:a + n])
        else:
            pieces.append(tab[:, a - n + 1:a + 1][:, ::-1])
        i = j
    return jnp.concatenate(pieces, axis=1)


def _toeplitz_bias(rel_bias, rows, cols, offset):
    nh = rel_bias.shape[0]
    period = rows + cols
    m = np.arange(period)
    c_minus_r = np.where(m < cols, m, m - period)
    rel = np.clip(offset - c_minus_r, -MAX_REL, MAX_REL) + MAX_REL
    vec = _static_take(rel_bias.astype(F32), rel)
    return jnp.tile(vec, (1, rows))[:, :rows * (period - 1)].reshape(nh, rows, period - 1)[:, :, :cols]


def _band_bias_prompt(rel_bias):
    r = np.arange(BAND_Q_SUB)
    c = np.arange(BAND_WINDOW)
    jb = c[None, :] - (r // CHUNK * CHUNK)[:, None]
    inside = (jb >= 0) & (jb < (BAND_CHUNKS + 1) * CHUNK)
    toep = _toeplitz_bias(rel_bias, BAND_Q_SUB, BAND_WINDOW, BAND_LEN)
    return jnp.where(jnp.asarray(inside)[None], toep, NEG_INF)


def _band_prompt(proj3, q_col0, k_col0, v_col0, z_col0, rel_bias, qn_g, kn_g):
    bsz, s, _ = proj3.shape
    nh = rel_bias.shape[0]
    tq = _row_tile(s, BAND_Q_BLOCK, BAND_LEN)
    assert s % tq == 0 and tq % BAND_LEN == 0
    nq = s // tq
    per = tq // BAND_LEN
    qb, kb, vb, zb = (c // HEAD_DIM for c in (q_col0, k_col0, v_col0, z_col0))
    bias = _band_bias_prompt(rel_bias)
    blk = (1, tq, HEAD_DIM)
    prev = (1, BAND_LEN, HEAD_DIM)
    gspec = pl.BlockSpec((1, HEAD_DIM), lambda h, b, i: (0, 0))
    return pl.pallas_call(
        _band_prompt_kernel,
        grid=(nh, bsz, nq),
        in_specs=[pl.BlockSpec(blk, lambda h, b, i: (b, i, qb + h)),
                  pl.BlockSpec(prev, lambda h, b, i: (b, jnp.maximum(i * per - 1, 0), kb + h)),
                  pl.BlockSpec(blk, lambda h, b, i: (b, i, kb + h)),
                  pl.BlockSpec(prev, lambda h, b, i: (b, jnp.maximum(i * per - 1, 0), vb + h)),
                  pl.BlockSpec(blk, lambda h, b, i: (b, i, vb + h)),
                  pl.BlockSpec(blk, lambda h, b, i: (b, i, zb + h)),
                  pl.BlockSpec((1, BAND_Q_SUB, BAND_WINDOW), lambda h, b, i: (h, 0, 0)),
                  gspec, gspec],
        out_specs=[pl.BlockSpec(blk, lambda h, b, i: (b, i, h)),
                   pl.BlockSpec(prev, lambda h, b, i: (b, 0, h))],
        out_shape=(jax.ShapeDtypeStruct((bsz, s, nh * HEAD_DIM), BF16),
                   jax.ShapeDtypeStruct((bsz, BAND_LEN, nh * HEAD_DIM), F32)),
        compiler_params=_params("parallel", "parallel", "arbitrary"),
        name="band_prompt",
    )(proj3, proj3, proj3, proj3, proj3, proj3, bias, qn_g.reshape(1, HEAD_DIM), kn_g.reshape(1, HEAD_DIM))


def _copy_head_rows(head, srcs, dsts):
    n_rows = dsts[0].shape[0]
    nh = srcs[0].shape[1] // n_rows
    for hh in range(nh):
        @pl.when(head == hh)
        def _(hh=hh):
            for src, dst in zip(srcs, dsts):
                dst[...] = src[0, pl.ds(hh, n_rows, stride=nh), :]


def _band_sample_kernel(q_ref, k_ref, v_ref, ck_ref, cv_ref, z_ref, bias_ref, qg_ref, kg_ref,
                        o_ref, kn_ref, kbuf_ref, vbuf_ref, *, lc):
    scale = 1.0 / math.sqrt(HEAD_DIM)
    n = q_ref.shape[1]
    _copy_head_rows(pl.program_id(1), (ck_ref, cv_ref), (kbuf_ref, vbuf_ref))
    qn = _head_rms(q_ref[0], qg_ref[...]).astype(BF16)
    kn = _head_rms(k_ref[0], kg_ref[...])
    kn_ref[0] = kn
    zpad = jnp.zeros((LANES - n, HEAD_DIM), F32)
    kn = jnp.concatenate([kn, zpad], axis=0)
    v_new = jnp.concatenate([v_ref[0], zpad], axis=0)
    s_p = _qk(qn, kbuf_ref[...].astype(BF16)) * scale + bias_ref[0, :, :lc]
    s_c = _qk(qn, kn.astype(BF16)) * scale + bias_ref[0, :, lc:]
    m = jnp.maximum(jnp.max(s_p, axis=-1, keepdims=True), jnp.max(s_c, axis=-1, keepdims=True))
    p_p = jnp.exp(s_p - m)
    p_c = jnp.exp(s_c - m)
    l = jnp.sum(p_p, axis=-1, keepdims=True) + jnp.sum(p_c, axis=-1, keepdims=True)
    o = jnp.dot(p_p.astype(BF16), vbuf_ref[...].astype(BF16), preferred_element_type=F32)
    o = o + jnp.dot(p_c.astype(BF16), v_new.astype(BF16), preferred_element_type=F32)
    o_ref[0] = ((o / l) * jax.nn.silu(z_ref[0])).astype(o_ref.dtype)


def _band_sample(proj3, q_col0, k_col0, v_col0, z_col0, cache_k, cache_v, l, rel_bias, qn_g, kn_g):
    bsz, n, _ = proj3.shape
    lc = cache_k.shape[2]
    nh = rel_bias.shape[0]
    rows = lambda t: t.reshape(*t.shape[:2], lc * nh, HEAD_DIM)
    cache_k, cache_v = rows(cache_k), rows(cache_v)
    qb, kb, vb, zb = (c // HEAD_DIM for c in (q_col0, k_col0, v_col0, z_col0))
    bias = _toeplitz_bias(rel_bias, n, lc + n, lc)
    assert n <= LANES
    bias = jnp.pad(bias, ((0, 0), (0, 0), (0, LANES - n)), constant_values=NEG_INF)
    blk = (1, n, HEAD_DIM)
    cblk = (None, 1, lc * nh, HEAD_DIM)
    gspec = pl.BlockSpec((1, HEAD_DIM), lambda b, h: (0, 0))
    return pl.pallas_call(
        functools.partial(_band_sample_kernel, lc=lc),
        grid=(bsz, nh),
        in_specs=[pl.BlockSpec(blk, lambda b, h: (b, 0, qb + h)),
                  pl.BlockSpec(blk, lambda b, h: (b, 0, kb + h)),
                  pl.BlockSpec(blk, lambda b, h: (b, 0, vb + h)),
                  pl.BlockSpec(cblk, lambda b, h: (l, b, 0, 0)),
                  pl.BlockSpec(cblk, lambda b, h: (l, b, 0, 0)),
                  pl.BlockSpec(blk, lambda b, h: (b, 0, zb + h)),
                  pl.BlockSpec((1, n, lc + LANES), lambda b, h: (h, 0, 0)),
                  gspec, gspec],
        out_specs=[pl.BlockSpec(blk, lambda b, h: (b, 0, h)),
                   pl.BlockSpec(blk, lambda b, h: (b, 0, h))],
        out_shape=(jax.ShapeDtypeStruct((bsz, n, nh * HEAD_DIM), BF16),
                   jax.ShapeDtypeStruct((bsz, n, nh * HEAD_DIM), F32)),
        scratch_shapes=[pltpu.VMEM((lc, HEAD_DIM), F32)] * 2,
        compiler_params=_params("parallel", "arbitrary"),
        name="band_sample",
    )(proj3, proj3, proj3, cache_k, cache_v, proj3, bias, qn_g.reshape(1, HEAD_DIM), kn_g.reshape(1, HEAD_DIM))


def _suffix_matrix():
    kb = SB_K_BLOCK
    j = lax.broadcasted_iota(jnp.int32, (2 * kb, 2 * kb), 0) % kb
    c = lax.broadcasted_iota(jnp.int32, (2 * kb, 2 * kb), 1)
    return jnp.where((c >= kb) | (j >= c), -1.0, 0.0).astype(BF16)


def _sb_groups(q, groups, carry, acc, umat):
    scale = 1.0 / math.sqrt(HEAD_DIM)
    kb = SB_K_BLOCK
    zs = [_qk(q, k.astype(BF16)) * scale for k, _, _ in groups]
    fails = [jnp.maximum(z, 0.0) + jnp.log(1.0 + jnp.exp(-jnp.abs(z))) for z in zs]
    fails = [f if causal is None else jnp.where(causal, f, 0.0) for f, (_, _, causal) in zip(fails, groups)]
    his = [f.astype(BF16) for f in fails]
    los = [(f - hi.astype(F32)).astype(BF16) for f, hi in zip(fails, his)]
    staged = []
    for z, hi, lo, (_, _, causal) in zip(zs, his, los, groups):
        parts = []
        for b in reversed(range(SB_GROUP // kb)):
            cols = slice(b * kb, (b + 1) * kb)
            sums = jnp.dot(jnp.concatenate([hi[:, cols], lo[:, cols]], axis=1), umat, preferred_element_type=F32)
            parts.append((b, z[:, cols] + sums[:, :kb], sums[:, kb:]))
        staged.append((parts, causal))
    ws = []
    for parts, causal in staged:
        log_w = [None] * len(parts)
        for b, part, total in parts:
            log_w[b] = part + carry
            carry = carry + total
        w = jnp.exp(jnp.concatenate(log_w, axis=1))
        if causal is not None:
            w = jnp.where(causal, w, 0.0)
        ws.append(w.astype(BF16))
    vs = [v.astype(BF16) for _, v, _ in groups]
    w_all = ws[0] if len(ws) == 1 else jnp.concatenate(ws, axis=1)
    v_all = vs[0] if len(vs) == 1 else jnp.concatenate(vs, axis=0)
    return carry, acc + jnp.dot(w_all, v_all, preferred_element_type=F32)


def _sb_prompt_kernel(q_ref, k_ref, v_ref, z_ref, o_ref):
    qi = pl.program_id(2)
    tq = SB_Q_BLOCK
    q = q_ref[0].astype(BF16)
    umat = _suffix_matrix()
    q0 = qi * tq
    carry = jnp.zeros((tq, SB_K_BLOCK), F32)
    acc = jnp.zeros((tq, HEAD_DIM), F32)

    def load(k0, causal):
        k0 = pl.multiple_of(k0, SB_GROUP)
        return k_ref[0, pl.ds(k0, SB_GROUP), :], v_ref[0, pl.ds(k0, SB_GROUP), :], causal

    n_diag = tq // SB_GROUP
    assert n_diag % SB_PROMPT_UNROLL == 0
    for d in reversed(range(n_diag)):
        r0 = d * SB_GROUP
        k, v, _ = load(q0 + r0, None)
        causal = (lax.broadcasted_iota(jnp.int32, (tq - r0, SB_GROUP), 1)
                  < lax.broadcasted_iota(jnp.int32, (tq - r0, SB_GROUP), 0))
        c_lo, a_lo = _sb_groups(q[r0:], [(k, v, causal)], carry[r0:], acc[r0:], umat)
        carry = c_lo if d == 0 else jnp.concatenate([carry[:r0], c_lo], axis=0)
        acc = a_lo if d == 0 else jnp.concatenate([acc[:r0], a_lo], axis=0)

    def body(j, ca):
        groups = [load(q0 - (j * SB_PROMPT_UNROLL + u + 1) * SB_GROUP, None) for u in range(SB_PROMPT_UNROLL)]
        return _sb_groups(q, groups, ca[0], ca[1], umat)

    carry, acc = lax.fori_loop(0, qi * (n_diag // SB_PROMPT_UNROLL), body, (carry, acc))
    o_ref[0] = (acc * jax.nn.silu(z_ref[0])).astype(o_ref.dtype)


def _sb_prompt(proj3, q_col0, k_col0, v_col0, z_col0, nh):
    bsz, s, _ = proj3.shape
    tq = _row_tile(s, SB_Q_BLOCK)
    assert tq == SB_Q_BLOCK
    qb, kb, vb, zb = (c // HEAD_DIM for c in (q_col0, k_col0, v_col0, z_col0))
    blk = (1, tq, HEAD_DIM)
    full = (1, s, HEAD_DIM)
    return pl.pallas_call(
        _sb_prompt_kernel,
        grid=(bsz, nh, s // tq),
        in_specs=[pl.BlockSpec(blk, lambda b, h, i: (b, i, qb + h)),
                  pl.BlockSpec(full, lambda b, h, i: (b, 0, kb + h)),
                  pl.BlockSpec(full, lambda b, h, i: (b, 0, vb + h)),
                  pl.BlockSpec(blk, lambda b, h, i: (b, i, zb + h))],
        out_specs=pl.BlockSpec(blk, lambda b, h, i: (b, i, h)),
        out_shape=jax.ShapeDtypeStruct((bsz, s, nh * HEAD_DIM), BF16),
        compiler_params=_params("parallel", "parallel", "arbitrary"),
        name="sb_prompt",
    )(proj3, proj3, proj3, proj3)


def _sb_sample_kernel(q_ref, k_ref, v_ref, ck_ref, cv_ref, z_ref, o_ref, kbuf_ref, vbuf_ref, *, past_len):
    n = q_ref.shape[1]
    _copy_head_rows(pl.program_id(1), (ck_ref, cv_ref), (kbuf_ref, vbuf_ref))

    q = q_ref[0].astype(BF16)
    umat = _suffix_matrix()
    row = lax.broadcasted_iota(jnp.int32, (n, SB_GROUP), 0)
    col = lax.broadcasted_iota(jnp.int32, (n, SB_GROUP), 1)
    carry = jnp.zeros((n, SB_K_BLOCK), F32)
    acc = jnp.zeros((n, HEAD_DIM), F32)
    n_past = past_len // SB_GROUP
    unroll = math.gcd(n_past, SB_SAMPLE_UNROLL)

    def past_groups(first):
        groups = []
        for u in range(unroll):
            k0 = pl.multiple_of(past_len - (first + u + 1) * SB_GROUP, SB_GROUP)
            groups.append((kbuf_ref[pl.ds(k0, SB_GROUP), :], vbuf_ref[pl.ds(k0, SB_GROUP), :], None))
        return groups

    carry, acc = _sb_groups(q, [(k_ref[0], v_ref[0], col < row)] + past_groups(0), carry, acc, umat)
    carry, acc = lax.fori_loop(1, n_past // unroll, lambda j, ca: _sb_groups(q, past_groups(j * unroll), *ca, umat),
                               (carry, acc))
    o_ref[0] = (acc * jax.nn.silu(z_ref[0])).astype(o_ref.dtype)


def _sb_sample(proj3, q_col0, z_col0, k_new, v_new, cache_k, cache_v, l, nh):
    bsz, n, _ = proj3.shape
    past_len = cache_k.shape[2]
    assert past_len % SB_GROUP == 0 and n <= SB_GROUP
    qb, zb = q_col0 // HEAD_DIM, z_col0 // HEAD_DIM
    pad = ((0, 0), (0, SB_GROUP - n), (0, 0))
    k_pad = jnp.pad(k_new, pad)
    v_pad = jnp.pad(v_new, pad)
    blk = (1, n, HEAD_DIM)
    own = (1, SB_GROUP, HEAD_DIM)
    past = (None, 1, past_len * nh, HEAD_DIM)
    rows = lambda t: t.reshape(*t.shape[:2], past_len * nh, HEAD_DIM)
    cache_k, cache_v = rows(cache_k), rows(cache_v)
    return pl.pallas_call(
        functools.partial(_sb_sample_kernel, past_len=past_len),
        grid=(bsz, nh),
        in_specs=[pl.BlockSpec(blk, lambda b, h: (b, 0, qb + h)),
                  pl.BlockSpec(own, lambda b, h: (b, 0, h)),
                  pl.BlockSpec(own, lambda b, h: (b, 0, h)),
                  pl.BlockSpec(past, lambda b, h: (l, b, 0, 0)),
                  pl.BlockSpec(past, lambda b, h: (l, b, 0, 0)),
                  pl.BlockSpec(blk, lambda b, h: (b, 0, zb + h))],
        out_specs=pl.BlockSpec(blk, lambda b, h: (b, 0, h)),
        out_shape=jax.ShapeDtypeStruct((bsz, n, nh * HEAD_DIM), BF16),
        scratch_shapes=[pltpu.VMEM((past_len, HEAD_DIM), F32)] * 2,
        compiler_params=_params("parallel", "arbitrary"),
        name="sb_sample",
    )(proj3, k_pad, v_pad, cache_k, cache_v, proj3)


def _layer(x, l, w, cache):
    (norm_g, w_in, ssm, w_glu, b_glu, qn_g, kn_g, rel_bias, w_br_a, w_br_b, w_br_c, gate_b, w_out) = w
    ab_re, ab_im, bbt_re, bbt_im, c_re, c_im, d_skip = ssm
    bsz, s, d = x.shape
    mw = w_glu.shape[1]
    nh = mw // HEAD_DIM
    g = ab_re.shape[0]
    m = bsz * s
    x2 = x.reshape(m, d)
    hx = _rmsnorm(x2, norm_g)
    col = lambda i: i * mw
    kv_heads = None
    if cache is None and mw <= MM_TILE and w_in.shape[2] % mw == 0:
        proj, kv_heads = _in_proj_with_heads(hx, w_in, l, col(7), nh)
    else:
        proj = _matmul(hx, w_in, l)
    n_in = proj.shape[1]
    proj3 = proj.reshape(bsz, s, n_in)
    u_tm = proj3[:, :, :mw].transpose(1, 0, 2)
    if cache is None:
        h0_re = jnp.zeros((bsz, g, SSM_STATE), F32)
        h0_im = h0_re
    else:
        h0_re, h0_im = cache[4], cache[5]
    y_tm, ht_re, ht_im = _s5_branch(u_tm, h0_re, h0_im, ab_re, ab_im, bbt_re, bbt_im, c_re, c_im, d_skip)
    lhs_a = _glu_gate(y_tm, w_glu, l, b_glu, proj, col(1), bsz, s)
    heads = lambda t: t.reshape(bsz, s, nh, HEAD_DIM)
    if kv_heads is None:
        k_c = proj3[:, :, col(7):col(8)]
        v_c = proj3[:, :, col(8):col(9)]
        k_out, v_out = heads(k_c), heads(v_c)
    else:
        k_out, v_out = heads(kv_heads[0]), heads(kv_heads[1])
    v_b = proj3[:, :, col(4):col(5)]
    if cache is None:
        lhs_b, band_k = _band_prompt(proj3, col(2), col(3), col(4), col(5), rel_bias, qn_g, kn_g)
        keep = min(BAND_LEN, s)
        band_v = v_b[:, s - keep:]
        lhs_c = _sb_prompt(proj3, col(6), col(7), col(8), col(9), nh)
    else:
        sb_k, sb_v, bk, bv = cache[:4]
        lhs_b, band_k = _band_sample(proj3, col(2), col(3), col(4), col(5), bk, bv, l, rel_bias, qn_g, kn_g)
        band_v = v_b
        lhs_c = _sb_sample(proj3, col(6), col(9), k_c, v_c, sb_k, sb_v, l, nh)
    mixed = _branch_merge(lhs_a, lhs_b.reshape(m, mw), lhs_c.reshape(m, mw), w_br_a, w_br_b, w_br_c, l,
                          proj, gate_b, col(10))
    y = _matmul_residual(mixed, w_out, l, x2).reshape(bsz, s, d)
    band_heads = lambda t: t.reshape(bsz, t.shape[1], nh, HEAD_DIM)
    return y, (k_out, v_out, band_heads(band_k), band_heads(band_v), ht_re, ht_im)


def kernel(x_prompt, x_sample, cache_sb_k, cache_sb_v, cache_band_k, cache_band_v, state_ssm_re, state_ssm_im, norm_g, w_in, ssm_a_re, ssm_a_im, ssm_log_dt, ssm_b_re, ssm_b_im, ssm_c_re, ssm_c_im, ssm_d, w_glu, b_glu, q_norm_g, k_norm_g, rel_bias, w_br_a, w_br_b, w_br_c, gate_b, w_out):
    depth = w_in.shape[0]
    yp, ys = x_prompt, x_sample
    p_states, s_states = [], []
    w_in, w_glu, w_br_a, w_br_b, w_br_c, w_out = (t.astype(BF16) for t in (w_in, w_glu, w_br_a, w_br_b, w_br_c, w_out))
    caches = (cache_sb_k, cache_sb_v, cache_band_k, cache_band_v)
    for l in range(depth):
        ssm = _discretize(ssm_a_re[l], ssm_a_im[l], ssm_log_dt[l], ssm_b_re[l], ssm_b_im[l])
        w = (norm_g[l], w_in, (*ssm, ssm_c_re[l], ssm_c_im[l], ssm_d[l]), w_glu, b_glu[l], q_norm_g[l], k_norm_g[l],
             rel_bias[l], w_br_a, w_br_b, w_br_c, gate_b[l], w_out)
        yp, sp = _layer(yp, l, w, None)
        ys, ss = _layer(ys, l, w, (*caches, state_ssm_re[l], state_ssm_im[l]))
        p_states.append(sp)
        s_states.append(ss)
    stk = lambda states, i: jnp.stack([st[i] for st in states], axis=0)
    return (yp, ys,
            *(stk(p_states, i) for i in range(6)),
            *(stk(s_states, i) for i in range(6)))
```

```python
import functools
import math

import jax
import jax.numpy as jnp
import numpy as np
from jax import lax
from jax.experimental import pallas as pl
from jax.experimental.pallas import tpu as pltpu

F32 = jnp.float32
BF16 = jnp.bfloat16

CHUNK = 64
HEAD_DIM = 128
SSM_GROUP = 16
SSM_STATE = 64
BAND_CHUNKS = 8
BAND_LEN = BAND_CHUNKS * CHUNK
MAX_REL = 128
N_BRANCH = 3
RMS_EPS = 1e-6
NEG_INF = -1e30

LANES = 128
SUBLANES = 8
VMEM_LIMIT_BYTES = 56 * 1024 * 1024
MM_TILE = 1024
SSM_CH_BLOCK = 128
SSM_GROUPS_PER_BLOCK = SSM_CH_BLOCK // SSM_GROUP
SSM_LANES = SSM_GROUPS_PER_BLOCK * SSM_STATE
S5_RING = 3
S5_CHUNK = 128
SB_Q_BLOCK = 1024
SB_K_BLOCK = 128
SB_GROUP = 2 * SB_K_BLOCK
SB_PROMPT_UNROLL = 4
SB_SAMPLE_UNROLL = 4
BAND_Q_BLOCK = 4 * BAND_LEN
BAND_Q_SUB = 2 * CHUNK
BAND_WINDOW = BAND_LEN + BAND_Q_SUB


def _params(*sem):
    return pltpu.CompilerParams(dimension_semantics=sem, vmem_limit_bytes=VMEM_LIMIT_BYTES)


def _row_tile(m, target, align=SUBLANES):
    if m <= target:
        return m
    for t in range(target - target % align, 0, -align):
        if m % t == 0:
            return t
    raise ValueError((m, target, align))


def _rmsnorm_kernel(x_ref, g_ref, o_ref):
    x = x_ref[...]
    r = lax.rsqrt(jnp.mean(x * x, axis=-1, keepdims=True) + RMS_EPS)
    o_ref[...] = (x * r * g_ref[...]).astype(o_ref.dtype)


def _rmsnorm(x, g):
    m, d = x.shape
    tm = _row_tile(m, 512)
    return pl.pallas_call(
        _rmsnorm_kernel,
        grid=(m // tm,),
        in_specs=[pl.BlockSpec((tm, d), lambda i: (i, 0)),
                  pl.BlockSpec((1, d), lambda i: (0, 0))],
        out_specs=pl.BlockSpec((tm, d), lambda i: (i, 0)),
        out_shape=jax.ShapeDtypeStruct((m, d), BF16),
        compiler_params=_params("parallel"),
        name="rmsnorm",
    )(x, g.reshape(1, d))


def _mm_kernel(x_ref, w_ref, o_ref):
    o_ref[...] = jnp.dot(x_ref[...], w_ref[...], preferred_element_type=F32).astype(o_ref.dtype)


def _matmul(x, w, l, out_dtype=F32):
    m, k = x.shape
    _, _, n = w.shape
    tm = _row_tile(m, MM_TILE)
    tn = _row_tile(n, MM_TILE, LANES)
    return pl.pallas_call(
        _mm_kernel,
        grid=(m // tm, n // tn),
        in_specs=[pl.BlockSpec((tm, k), lambda i, j: (i, 0)),
                  pl.BlockSpec((None, k, tn), lambda i, j: (l, 0, j))],
        out_specs=pl.BlockSpec((tm, tn), lambda i, j: (i, j)),
        out_shape=jax.ShapeDtypeStruct((m, n), out_dtype),
        compiler_params=_params("parallel", "arbitrary"),
        name="in_proj",
    )(x, w)


def _mm_res_kernel(x_ref, w_ref, r_ref, o_ref):
    o_ref[...] = r_ref[...] + jnp.dot(x_ref[...], w_ref[...], preferred_element_type=F32)


def _matmul_residual(x, w, l, res):
    m, k = x.shape
    _, _, n = w.shape
    tm = _row_tile(m, MM_TILE)
    tn = _row_tile(n, MM_TILE, LANES)
    return pl.pallas_call(
        _mm_res_kernel,
        grid=(m // tm, n // tn),
        in_specs=[pl.BlockSpec((tm, k), lambda i, j: (i, 0)),
                  pl.BlockSpec((None, k, tn), lambda i, j: (l, 0, j)),
                  pl.BlockSpec((tm, tn), lambda i, j: (i, j))],
        out_specs=pl.BlockSpec((tm, tn), lambda i, j: (i, j)),
        out_shape=jax.ShapeDtypeStruct((m, n), F32),
        compiler_params=_params("parallel", "arbitrary"),
        name="out_proj",
    )(x, w, res)


def _branch_kernel(la_ref, lb_ref, lc_ref, wa_ref, wb_ref, wc_ref,
                   ga_ref, gb_ref, gc_ref, ba_ref, bb_ref, bc_ref, o_ref):
    def one(l_ref, w_ref, g_ref, b_ref):
        o = jnp.dot(l_ref[...], w_ref[...], preferred_element_type=F32)
        return jax.nn.sigmoid(g_ref[...] + b_ref[...]) * o
    mixed = one(la_ref, wa_ref, ga_ref, ba_ref) + one(lb_ref, wb_ref, gb_ref, bb_ref)
    mixed = mixed + one(lc_ref, wc_ref, gc_ref, bc_ref)
    o_ref[...] = mixed.astype(o_ref.dtype)


def _branch_merge(lhs_a, lhs_b, lhs_c, w_a, w_b, w_c, l, proj, gate_b, g_col0):
    m, kw = lhs_a.shape
    d = w_a.shape[2]
    tm = _row_tile(m, 512)
    tn = _row_tile(math.gcd(d, g_col0), MM_TILE, LANES)
    assert g_col0 % tn == 0 and d % tn == 0
    gblk = g_col0 // tn
    nblk = d // tn
    lspec = pl.BlockSpec((tm, kw), lambda j, i: (i, 0))
    wspec = pl.BlockSpec((None, kw, tn), lambda j, i: (l, 0, j))
    gspecs = [pl.BlockSpec((tm, tn), functools.partial(lambda j, i, br: (i, gblk + br * nblk + j), br=br))
              for br in range(N_BRANCH)]
    bspecs = [pl.BlockSpec((1, tn), functools.partial(lambda j, i, br: (0, br * nblk + j), br=br))
              for br in range(N_BRANCH)]
    gb = gate_b.reshape(1, N_BRANCH * d)
    return pl.pallas_call(
        _branch_kernel,
        grid=(d // tn, m // tm),
        in_specs=[lspec, lspec, lspec, wspec, wspec, wspec, *gspecs, *bspecs],
        out_specs=pl.BlockSpec((tm, tn), lambda j, i: (i, j)),
        out_shape=jax.ShapeDtypeStruct((m, d), BF16),
        compiler_params=_params("parallel", "arbitrary"),
        name="branch_merge",
    )(lhs_a, lhs_b, lhs_c, w_a, w_b, w_c, proj, proj, proj, gb, gb, gb)


def _discretize_kernel(are_ref, aim_ref, ldt_ref, bre_ref, bim_ref,
                       abre_ref, abim_ref, bbre_ref, bbim_ref):
    a_re = are_ref[...]
    a_im = aim_ref[...]
    dt = jnp.exp(ldt_ref[...])
    mag = jnp.exp(a_re * dt)
    ab_re = mag * jnp.cos(a_im * dt)
    ab_im = mag * jnp.sin(a_im * dt)
    den = a_re * a_re + a_im * a_im
    nr = ab_re - 1.0
    k_re = (nr * a_re + ab_im * a_im) / den
    k_im = (ab_im * a_re - nr * a_im) / den
    abre_ref[...] = ab_re
    abim_ref[...] = ab_im
    b_re = bre_ref[...]
    b_im = bim_ref[...]
    bbre_ref[...] = k_re[:, None, :] * b_re - k_im[:, None, :] * b_im
    bbim_ref[...] = k_re[:, None, :] * b_im + k_im[:, None, :] * b_re


def _discretize(a_re, a_im, log_dt, b_re, b_im):
    g, n = a_re.shape
    p = b_re.shape[-1]
    gn = jax.ShapeDtypeStruct((g, n), F32)
    gpn = jax.ShapeDtypeStruct((g, p, n), F32)
    return pl.pallas_call(
        _discretize_kernel,
        out_shape=(gn, gn, gpn, gpn),
        name="s5_discretize",
    )(a_re, a_im, log_dt.reshape(g, 1), b_re.transpose(0, 2, 1), b_im.transpose(0, 2, 1))


def _block_diag(x):
    ncb, gl, r, c = x.shape
    eye = jnp.eye(gl, dtype=x.dtype)
    out = x[:, :, :, None, :] * eye[None, :, None, :, None]
    return out.reshape(ncb, gl * r, gl * c)


def _s5_kernel(u_ref, up_ref, h0_ref, a_ref, bblk_ref, cblk_ref, d_ref, y_ref, ht_ref, *scratch,
               t_steps, n_chunks):
    ring, h_ref = scratch[:S5_RING], scratch[S5_RING]
    t_idx = pl.program_id(1)

    @pl.when(t_idx == 0)
    def _():
        h_ref[...] = h0_ref[0]
        for buf in ring:
            buf[...] = jnp.zeros(buf.shape, F32)

    rows = t_steps * SUBLANES

    def stages(in_buf, scan_buf, out_buf):
        u = u_ref[...].reshape(rows, SSM_CH_BLOCK)
        in_buf[...] = jnp.dot(u.astype(BF16), bblk_ref[0], preferred_element_type=F32)

        u_out = up_ref[...].reshape(rows, SSM_CH_BLOCK)
        y = jnp.dot(out_buf[...].astype(BF16), cblk_ref[0], preferred_element_type=F32)
        y = y + d_ref[0] * u_out
        y_ref[...] = jax.nn.gelu(y).reshape(t_steps, SUBLANES, SSM_CH_BLOCK)

        a_re = a_ref[0, :, :SSM_LANES]
        a_im = a_ref[0, :, SSM_LANES:]
        h_re = h_ref[:, :SSM_LANES]
        h_im = h_ref[:, SSM_LANES:]
        s_re, s_im = h_re, h_im
        for t in range(t_steps):
            row = slice(t * SUBLANES, (t + 1) * SUBLANES)
            b_re = scan_buf[row, :SSM_LANES]
            b_im = scan_buf[row, SSM_LANES:]
            s_re, s_im = a_re * s_re - a_im * s_im + b_re, a_re * s_im + a_im * s_re + b_im
            scan_buf[row, :SSM_LANES] = s_re
            scan_buf[row, SSM_LANES:] = s_im
        in_range = (t_idx >= 1) & (t_idx <= n_chunks)
        h_ref[:, :SSM_LANES] = jnp.where(in_range, s_re, h_re)
        h_ref[:, SSM_LANES:] = jnp.where(in_range, s_im, h_im)
        ht_ref[0] = h_ref[...]

    for k in range(S5_RING):
        pl.when(t_idx % S5_RING == k)(functools.partial(
            stages, ring[k], ring[(k + S5_RING - 1) % S5_RING], ring[(k + S5_RING - 2) % S5_RING]))


def _s5_branch(u_tm, h0_re, h0_im, ab_re, ab_im, bbt_re, bbt_im, c_re, c_im, d_skip):
    s, b, w = u_tm.shape
    assert b == SUBLANES, "the scan keeps one stream per sublane"
    g, n = ab_re.shape
    assert n == SSM_STATE and w == g * SSM_GROUP and w % SSM_CH_BLOCK == 0
    ncb = w // SSM_CH_BLOCK
    gl = SSM_GROUPS_PER_BLOCK
    t_steps = _row_tile(s, S5_CHUNK)

    def lanes(x):
        r = x.shape[0]
        return x.reshape(r, ncb, gl * n).transpose(1, 0, 2)

    h0 = jnp.concatenate([lanes(h0_re), lanes(h0_im)], axis=-1)
    a_tab = jnp.concatenate([lanes(jnp.broadcast_to(ab_re[None], (SUBLANES, g, n))),
                             lanes(jnp.broadcast_to(ab_im[None], (SUBLANES, g, n)))], axis=-1)
    bblk = jnp.concatenate([_block_diag(bbt_re.reshape(ncb, gl, SSM_GROUP, n)),
                            _block_diag(bbt_im.reshape(ncb, gl, SSM_GROUP, n))], axis=-1).astype(BF16)
    cblk = jnp.concatenate([_block_diag(c_re.reshape(ncb, gl, SSM_GROUP, n).transpose(0, 1, 3, 2)),
                            _block_diag(-c_im.reshape(ncb, gl, SSM_GROUP, n).transpose(0, 1, 3, 2))],
                           axis=1).astype(BF16)
    d_tab = d_skip.reshape(ncb, 1, SSM_CH_BLOCK)
    st_spec = pl.BlockSpec((1, SUBLANES, 2 * SSM_LANES), lambda c, t: (c, 0, 0))
    n_chunks = s // t_steps
    chunk = (t_steps, SUBLANES, SSM_CH_BLOCK)
    last = n_chunks - 1
    y_tm, ht = pl.pallas_call(
        functools.partial(_s5_kernel, t_steps=t_steps, n_chunks=n_chunks),
        grid=(ncb, n_chunks + S5_RING - 1),
        in_specs=[pl.BlockSpec(chunk, lambda c, t: (jnp.minimum(t, last), 0, c)),
                  pl.BlockSpec(chunk, lambda c, t: (jnp.clip(t - 2, 0, last), 0, c)),
                  st_spec, st_spec,
                  pl.BlockSpec((1, SSM_CH_BLOCK, 2 * SSM_LANES), lambda c, t: (c, 0, 0)),
                  pl.BlockSpec((1, 2 * SSM_LANES, SSM_CH_BLOCK), lambda c, t: (c, 0, 0)),
                  pl.BlockSpec((1, 1, SSM_CH_BLOCK), lambda c, t: (c, 0, 0))],
        out_specs=[pl.BlockSpec(chunk, lambda c, t: (jnp.clip(t - 2, 0, last), 0, c)), st_spec],
        out_shape=(jax.ShapeDtypeStruct((s, b, w), F32),
                   jax.ShapeDtypeStruct((ncb, SUBLANES, 2 * SSM_LANES), F32)),
        scratch_shapes=[pltpu.VMEM((t_steps * SUBLANES, 2 * SSM_LANES), F32)] * S5_RING
                       + [pltpu.VMEM((SUBLANES, 2 * SSM_LANES), F32)],
        compiler_params=_params("parallel", "arbitrary"),
        name="s5_scan",
    )(u_tm, u_tm, h0, a_tab, bblk, cblk, d_tab)

    def unlanes(x):
        return x.transpose(1, 0, 2).reshape(b, g, n)

    return y_tm, unlanes(ht[..., :SSM_LANES]), unlanes(ht[..., SSM_LANES:])


def _glu_kernel(y_ref, w_ref, b_ref, z_ref, o_ref):
    y = y_ref[...]
    gate = jnp.dot(y.astype(BF16), w_ref[...], preferred_element_type=F32) + b_ref[...]
    o_ref[...] = (y * jax.nn.sigmoid(gate) * jax.nn.silu(z_ref[...])).astype(o_ref.dtype)


def _glu_gate(y_tm, w_glu, l, b_glu, proj, z_col0, bsz, s):
    w = w_glu.shape[1]
    tm = _row_tile(s, 512)
    nt = s // tm
    zblk = z_col0 // w
    return pl.pallas_call(
        _glu_kernel,
        grid=(bsz, nt),
        in_specs=[pl.BlockSpec((tm, w), lambda b, t: (t, b)),
                  pl.BlockSpec((None, w, w), lambda b, t: (l, 0, 0)),
                  pl.BlockSpec((1, w), lambda b, t: (0, 0)),
                  pl.BlockSpec((tm, w), lambda b, t: (b * nt + t, zblk))],
        out_specs=pl.BlockSpec((tm, w), lambda b, t: (b * nt + t, 0)),
        out_shape=jax.ShapeDtypeStruct((bsz * s, w), BF16),
        compiler_params=_params("parallel", "arbitrary"),
        name="glu_gate",
    )(y_tm.reshape(s, bsz * w), w_glu, b_glu.reshape(1, w), proj)


def _head_rms(x, g):
    r = lax.rsqrt(jnp.mean(x * x, axis=-1, keepdims=True) + RMS_EPS)
    return x * r * g


def _qk(q, k):
    return lax.dot_general(q, k, (((1,), (1,)), ((), ())), preferred_element_type=F32)


def _band_prompt_kernel(q_ref, kp_ref, kc_ref, vp_ref, vc_ref, z_ref, bias_ref, qg_ref, kg_ref,
                        o_ref, kn_ref):
    i = pl.program_id(2)
    scale = 1.0 / math.sqrt(HEAD_DIM)
    qn = _head_rms(q_ref[0], qg_ref[...]).astype(BF16)
    kpn = _head_rms(kp_ref[0], kg_ref[...])
    kcn = _head_rms(kc_ref[0], kg_ref[...])
    tq = q_ref.shape[1]
    kn_ref[0] = kcn[tq - BAND_LEN:]
    k_all = jnp.concatenate([kpn.astype(BF16), kcn.astype(BF16)], axis=0)
    v_all = jnp.concatenate([vp_ref[0].astype(BF16), vc_ref[0].astype(BF16)], axis=0)
    bias = bias_ref[0]
    col = lax.broadcasted_iota(jnp.int32, (BAND_Q_SUB, BAND_WINDOW), 1)
    z = z_ref[0]
    n_sub = tq // BAND_Q_SUB
    rows = [slice(p * BAND_Q_SUB, (p + 1) * BAND_Q_SUB) for p in range(n_sub)]
    wins = [slice(p * BAND_Q_SUB, p * BAND_Q_SUB + BAND_WINDOW) for p in range(n_sub)]
    scores = []
    for p in range(n_sub):
        s = _qk(qn[rows[p]], k_all[wins[p]]) * scale + bias
        if p * BAND_Q_SUB < BAND_LEN:
            n_missing = jnp.where(i == 0, BAND_LEN - p * BAND_Q_SUB, 0)
            s = jnp.where(col < n_missing, NEG_INF, s)
        scores.append(s)
    probs = [jnp.exp(s - jnp.max(s, axis=-1, keepdims=True)) for s in scores]
    outs = [jnp.dot(pr.astype(BF16), v_all[wins[p]], preferred_element_type=F32) for p, pr in enumerate(probs)]
    for p in range(n_sub):
        l = jnp.sum(probs[p], axis=-1, keepdims=True)
        o_ref[0, rows[p], :] = ((outs[p] / l) * jax.nn.silu(z[rows[p]])).astype(o_ref.dtype)


def _static_take(tab, idx):
    nh = tab.shape[0]
    idx = np.asarray(idx)
    pieces = []
    i = 0
    while i < len(idx):
        step = int(idx[i + 1] - idx[i]) if i + 1 < len(idx) else 0
        if step not in (-1, 0, 1):
            step = 0
            j = i + 1
        else:
            j = i + 1
            while j < len(idx) and idx[j] - idx[j - 1] == step:
                j += 1
        n = j - i
        a = int(idx[i])
        if step == 0:
            pieces.append(jnp.broadcast_to(tab[:, a:a + 1], (nh, n)))
        elif step == 1:
            pieces.append(tab[:, a:a + n])
        else:
            pieces.append(tab[:, a - n + 1:a + 1][:, ::-1])
        i = j
    return jnp.concatenate(pieces, axis=1)


def _toeplitz_bias(rel_bias, rows, cols, offset):
    nh = rel_bias.shape[0]
    period = rows + cols
    m = np.arange(period)
    c_minus_r = np.where(m < cols, m, m - period)
    rel = np.clip(offset - c_minus_r, -MAX_REL, MAX_REL) + MAX_REL
    vec = _static_take(rel_bias.astype(F32), rel)
    return jnp.tile(vec, (1, rows))[:, :rows * (period - 1)].reshape(nh, rows, period - 1)[:, :, :cols]


def _band_bias_prompt(rel_bias):
    r = np.arange(BAND_Q_SUB)
    c = np.arange(BAND_WINDOW)
    jb = c[None, :] - (r // CHUNK * CHUNK)[:, None]
    inside = (jb >= 0) & (jb < (BAND_CHUNKS + 1) * CHUNK)
    toep = _toeplitz_bias(rel_bias, BAND_Q_SUB, BAND_WINDOW, BAND_LEN)
    return jnp.where(jnp.asarray(inside)[None], toep, NEG_INF)


def _band_prompt(proj3, q_col0, k_col0, v_col0, z_col0, rel_bias, qn_g, kn_g):
    bsz, s, _ = proj3.shape
    nh = rel_bias.shape[0]
    tq = _row_tile(s, BAND_Q_BLOCK, BAND_LEN)
    assert s % tq == 0 and tq % BAND_LEN == 0
    nq = s // tq
    per = tq // BAND_LEN
    qb, kb, vb, zb = (c // HEAD_DIM for c in (q_col0, k_col0, v_col0, z_col0))
    bias = _band_bias_prompt(rel_bias)
    blk = (1, tq, HEAD_DIM)
    prev = (1, BAND_LEN, HEAD_DIM)
    gspec = pl.BlockSpec((1, HEAD_DIM), lambda h, b, i: (0, 0))
    return pl.pallas_call(
        _band_prompt_kernel,
        grid=(nh, bsz, nq),
        in_specs=[pl.BlockSpec(blk, lambda h, b, i: (b, i, qb + h)),
                  pl.BlockSpec(prev, lambda h, b, i: (b, jnp.maximum(i * per - 1, 0), kb + h)),
                  pl.BlockSpec(blk, lambda h, b, i: (b, i, kb + h)),
                  pl.BlockSpec(prev, lambda h, b, i: (b, jnp.maximum(i * per - 1, 0), vb + h)),
                  pl.BlockSpec(blk, lambda h, b, i: (b, i, vb + h)),
                  pl.BlockSpec(blk, lambda h, b, i: (b, i, zb + h)),
                  pl.BlockSpec((1, BAND_Q_SUB, BAND_WINDOW), lambda h, b, i: (h, 0, 0)),
                  gspec, gspec],
        out_specs=[pl.BlockSpec(blk, lambda h, b, i: (b, i, h)),
                   pl.BlockSpec(prev, lambda h, b, i: (b, 0, h))],
        out_shape=(jax.ShapeDtypeStruct((bsz, s, nh * HEAD_DIM), BF16),
                   jax.ShapeDtypeStruct((bsz, BAND_LEN, nh * HEAD_DIM), F32)),
        compiler_params=_params("parallel", "parallel", "arbitrary"),
        name="band_prompt",
    )(proj3, proj3, proj3, proj3, proj3, proj3, bias, qn_g.reshape(1, HEAD_DIM), kn_g.reshape(1, HEAD_DIM))


def _copy_head_rows(head, srcs, dsts):
    n_rows = dsts[0].shape[0]
    nh = srcs[0].shape[1] // n_rows
    for hh in range(nh):
        @pl.when(head == hh)
        def _(hh=hh):
            for src, dst in zip(srcs, dsts):
                dst[...] = src[0, pl.ds(hh, n_rows, stride=nh), :]


def _band_sample_kernel(q_ref, k_ref, v_ref, ck_ref, cv_ref, z_ref, bias_ref, qg_ref, kg_ref,
                        o_ref, kn_ref, kbuf_ref, vbuf_ref, *, lc):
    scale = 1.0 / math.sqrt(HEAD_DIM)
    n = q_ref.shape[1]
    _copy_head_rows(pl.program_id(1), (ck_ref, cv_ref), (kbuf_ref, vbuf_ref))
    qn = _head_rms(q_ref[0], qg_ref[...]).astype(BF16)
    kn = _head_rms(k_ref[0], kg_ref[...])
    kn_ref[0] = kn
    zpad = jnp.zeros((LANES - n, HEAD_DIM), F32)
    kn = jnp.concatenate([kn, zpad], axis=0)
    v_new = jnp.concatenate([v_ref[0], zpad], axis=0)
    s_p = _qk(qn, kbuf_ref[...].astype(BF16)) * scale + bias_ref[0, :, :lc]
    s_c = _qk(qn, kn.astype(BF16)) * scale + bias_ref[0, :, lc:]
    m = jnp.maximum(jnp.max(s_p, axis=-1, keepdims=True), jnp.max(s_c, axis=-1, keepdims=True))
    p_p = jnp.exp(s_p - m)
    p_c = jnp.exp(s_c - m)
    l = jnp.sum(p_p, axis=-1, keepdims=True) + jnp.sum(p_c, axis=-1, keepdims=True)
    o = jnp.dot(p_p.astype(BF16), vbuf_ref[...].astype(BF16), preferred_element_type=F32)
    o = o + jnp.dot(p_c.astype(BF16), v_new.astype(BF16), preferred_element_type=F32)
    o_ref[0] = ((o / l) * jax.nn.silu(z_ref[0])).astype(o_ref.dtype)


def _band_sample(proj3, q_col0, k_col0, v_col0, z_col0, cache_k, cache_v, l, rel_bias, qn_g, kn_g):
    bsz, n, _ = proj3.shape
    lc = cache_k.shape[2]
    nh = rel_bias.shape[0]
    rows = lambda t: t.reshape(*t.shape[:2], lc * nh, HEAD_DIM)
    cache_k, cache_v = rows(cache_k), rows(cache_v)
    qb, kb, vb, zb = (c // HEAD_DIM for c in (q_col0, k_col0, v_col0, z_col0))
    bias = _toeplitz_bias(rel_bias, n, lc + n, lc)
    assert n <= LANES
    bias = jnp.pad(bias, ((0, 0), (0, 0), (0, LANES - n)), constant_values=NEG_INF)
    blk = (1, n, HEAD_DIM)
    cblk = (None, 1, lc * nh, HEAD_DIM)
    gspec = pl.BlockSpec((1, HEAD_DIM), lambda b, h: (0, 0))
    return pl.pallas_call(
        functools.partial(_band_sample_kernel, lc=lc),
        grid=(bsz, nh),
        in_specs=[pl.BlockSpec(blk, lambda b, h: (b, 0, qb + h)),
                  pl.BlockSpec(blk, lambda b, h: (b, 0, kb + h)),
                  pl.BlockSpec(blk, lambda b, h: (b, 0, vb + h)),
                  pl.BlockSpec(cblk, lambda b, h: (l, b, 0, 0)),
                  pl.BlockSpec(cblk, lambda b, h: (l, b, 0, 0)),
                  pl.BlockSpec(blk, lambda b, h: (b, 0, zb + h)),
                  pl.BlockSpec((1, n, lc + LANES), lambda b, h: (h, 0, 0)),
                  gspec, gspec],
        out_specs=[pl.BlockSpec(blk, lambda b, h: (b, 0, h)),
                   pl.BlockSpec(blk, lambda b, h: (b, 0, h))],
        out_shape=(jax.ShapeDtypeStruct((bsz, n, nh * HEAD_DIM), BF16),
                   jax.ShapeDtypeStruct((bsz, n, nh * HEAD_DIM), F32)),
        scratch_shapes=[pltpu.VMEM((lc, HEAD_DIM), F32)] * 2,
        compiler_params=_params("parallel", "arbitrary"),
        name="band_sample",
    )(proj3, proj3, proj3, cache_k, cache_v, proj3, bias, qn_g.reshape(1, HEAD_DIM), kn_g.reshape(1, HEAD_DIM))


def _suffix_matrix():
    kb = SB_K_BLOCK
    j = lax.broadcasted_iota(jnp.int32, (2 * kb, 2 * kb), 0) % kb
    c = lax.broadcasted_iota(jnp.int32, (2 * kb, 2 * kb), 1)
    return jnp.where((c >= kb) | (j >= c), -1.0, 0.0).astype(BF16)


def _sb_groups(q, groups, carry, acc, umat):
    scale = 1.0 / math.sqrt(HEAD_DIM)
    kb = SB_K_BLOCK
    zs = [_qk(q, k.astype(BF16)) * scale for k, _, _ in groups]
    fails = [jnp.maximum(z, 0.0) + jnp.log(1.0 + jnp.exp(-jnp.abs(z))) for z in zs]
    fails = [f if causal is None else jnp.where(causal, f, 0.0) for f, (_, _, causal) in zip(fails, groups)]
    his = [f.astype(BF16) for f in fails]
    los = [(f - hi.astype(F32)).astype(BF16) for f, hi in zip(fails, his)]
    staged = []
    for z, hi, lo, (_, _, causal) in zip(zs, his, los, groups):
        parts = []
        for b in reversed(range(SB_GROUP // kb)):
            cols = slice(b * kb, (b + 1) * kb)
            sums = jnp.dot(jnp.concatenate([hi[:, cols], lo[:, cols]], axis=1), umat, preferred_element_type=F32)
            parts.append((b, z[:, cols] + sums[:, :kb], sums[:, kb:]))
        staged.append((parts, causal))
    ws = []
    for parts, causal in staged:
        log_w = [None] * len(parts)
        for b, part, total in parts:
            log_w[b] = part + carry
            carry = carry + total
        w = jnp.exp(jnp.concatenate(log_w, axis=1))
        if causal is not None:
            w = jnp.where(causal, w, 0.0)
        ws.append(w.astype(BF16))
    vs = [v.astype(BF16) for _, v, _ in groups]
    w_all = ws[0] if len(ws) == 1 else jnp.concatenate(ws, axis=1)
    v_all = vs[0] if len(vs) == 1 else jnp.concatenate(vs, axis=0)
    return carry, acc + jnp.dot(w_all, v_all, preferred_element_type=F32)


def _sb_prompt_kernel(q_ref, k_ref, v_ref, z_ref, o_ref):
    qi = pl.program_id(2)
    tq = SB_Q_BLOCK
    q = q_ref[0].astype(BF16)
    umat = _suffix_matrix()
    q0 = qi * tq
    carry = jnp.zeros((tq, SB_K_BLOCK), F32)
    acc = jnp.zeros((tq, HEAD_DIM), F32)

    def load(k0, causal):
        k0 = pl.multiple_of(k0, SB_GROUP)
        return k_ref[0, pl.ds(k0, SB_GROUP), :], v_ref[0, pl.ds(k0, SB_GROUP), :], causal

    n_diag = tq // SB_GROUP
    assert n_diag % SB_PROMPT_UNROLL == 0
    for d in reversed(range(n_diag)):
        r0 = d * SB_GROUP
        k, v, _ = load(q0 + r0, None)
        causal = (lax.broadcasted_iota(jnp.int32, (tq - r0, SB_GROUP), 1)
                  < lax.broadcasted_iota(jnp.int32, (tq - r0, SB_GROUP), 0))
        c_lo, a_lo = _sb_groups(q[r0:], [(k, v, causal)], carry[r0:], acc[r0:], umat)
        carry = c_lo if d == 0 else jnp.concatenate([carry[:r0], c_lo], axis=0)
        acc = a_lo if d == 0 else jnp.concatenate([acc[:r0], a_lo], axis=0)

    def body(j, ca):
        groups = [load(q0 - (j * SB_PROMPT_UNROLL + u + 1) * SB_GROUP, None) for u in range(SB_PROMPT_UNROLL)]
        return _sb_groups(q, groups, ca[0], ca[1], umat)

    carry, acc = lax.fori_loop(0, qi * (n_diag // SB_PROMPT_UNROLL), body, (carry, acc))
    o_ref[0] = (acc * jax.nn.silu(z_ref[0])).astype(o_ref.dtype)


def _sb_prompt(proj3, q_col0, k_col0, v_col0, z_col0, nh):
    bsz, s, _ = proj3.shape
    tq = _row_tile(s, SB_Q_BLOCK)
    assert tq == SB_Q_BLOCK
    qb, kb, vb, zb = (c // HEAD_DIM for c in (q_col0, k_col0, v_col0, z_col0))
    blk = (1, tq, HEAD_DIM)
    full = (1, s, HEAD_DIM)
    return pl.pallas_call(
        _sb_prompt_kernel,
        grid=(bsz, nh, s // tq),
        in_specs=[pl.BlockSpec(blk, lambda b, h, i: (b, i, qb + h)),
                  pl.BlockSpec(full, lambda b, h, i: (b, 0, kb + h)),
                  pl.BlockSpec(full, lambda b, h, i: (b, 0, vb + h)),
                  pl.BlockSpec(blk, lambda b, h, i: (b, i, zb + h))],
        out_specs=pl.BlockSpec(blk, lambda b, h, i: (b, i, h)),
        out_shape=jax.ShapeDtypeStruct((bsz, s, nh * HEAD_DIM), BF16),
        compiler_params=_params("parallel", "parallel", "arbitrary"),
        name="sb_prompt",
    )(proj3, proj3, proj3, proj3)


def _sb_sample_kernel(q_ref, k_ref, v_ref, ck_ref, cv_ref, z_ref, o_ref, kbuf_ref, vbuf_ref, *, past_len):
    n = q_ref.shape[1]
    _copy_head_rows(pl.program_id(1), (ck_ref, cv_ref), (kbuf_ref, vbuf_ref))

    q = q_ref[0].astype(BF16)
    umat = _suffix_matrix()
    row = lax.broadcasted_iota(jnp.int32, (n, SB_GROUP), 0)
    col = lax.broadcasted_iota(jnp.int32, (n, SB_GROUP), 1)
    carry = jnp.zeros((n, SB_K_BLOCK), F32)
    acc = jnp.zeros((n, HEAD_DIM), F32)
    n_past = past_len // SB_GROUP
    unroll = math.gcd(n_past, SB_SAMPLE_UNROLL)

    def past_groups(first):
        groups = []
        for u in range(unroll):
            k0 = pl.multiple_of(past_len - (first + u + 1) * SB_GROUP, SB_GROUP)
            groups.append((kbuf_ref[pl.ds(k0, SB_GROUP), :], vbuf_ref[pl.ds(k0, SB_GROUP), :], None))
        return groups

    carry, acc = _sb_groups(q, [(k_ref[0], v_ref[0], col < row)] + past_groups(0), carry, acc, umat)
    carry, acc = lax.fori_loop(1, n_past // unroll, lambda j, ca: _sb_groups(q, past_groups(j * unroll), *ca, umat),
                               (carry, acc))
    o_ref[0] = (acc * jax.nn.silu(z_ref[0])).astype(o_ref.dtype)


def _sb_sample(proj3, q_col0, z_col0, k_new, v_new, cache_k, cache_v, l, nh):
    bsz, n, _ = proj3.shape
    past_len = cache_k.shape[2]
    assert past_len % SB_GROUP == 0 and n <= SB_GROUP
    qb, zb = q_col0 // HEAD_DIM, z_col0 // HEAD_DIM
    pad = ((0, 0), (0, SB_GROUP - n), (0, 0))
    k_pad = jnp.pad(k_new, pad)
    v_pad = jnp.pad(v_new, pad)
    blk = (1, n, HEAD_DIM)
    own = (1, SB_GROUP, HEAD_DIM)
    past = (None, 1, past_len * nh, HEAD_DIM)
    rows = lambda t: t.reshape(*t.shape[:2], past_len * nh, HEAD_DIM)
    cache_k, cache_v = rows(cache_k), rows(cache_v)
    return pl.pallas_call(
        functools.partial(_sb_sample_kernel, past_len=past_len),
        grid=(bsz, nh),
        in_specs=[pl.BlockSpec(blk, lambda b, h: (b, 0, qb + h)),
                  pl.BlockSpec(own, lambda b, h: (b, 0, h)),
                  pl.BlockSpec(own, lambda b, h: (b, 0, h)),
                  pl.BlockSpec(past, lambda b, h: (l, b, 0, 0)),
                  pl.BlockSpec(past, lambda b, h: (l, b, 0, 0)),
                  pl.BlockSpec(blk, lambda b, h: (b, 0, zb + h))],
        out_specs=pl.BlockSpec(blk, lambda b, h: (b, 0, h)),
        out_shape=jax.ShapeDtypeStruct((bsz, n, nh * HEAD_DIM), BF16),
        scratch_shapes=[pltpu.VMEM((past_len, HEAD_DIM), F32)] * 2,
        compiler_params=_params("parallel", "arbitrary"),
        name="sb_sample",
    )(proj3, k_pad, v_pad, cache_k, cache_v, proj3)


def _layer(x, l, w, cache):
    (norm_g, w_in, ssm, w_glu, b_glu, qn_g, kn_g, rel_bias, w_br_a, w_br_b, w_br_c, gate_b, w_out) = w
    ab_re, ab_im, bbt_re, bbt_im, c_re, c_im, d_skip = ssm
    bsz, s, d = x.shape
    mw = w_glu.shape[1]
    nh = mw // HEAD_DIM
    g = ab_re.shape[0]
    m = bsz * s
    x2 = x.reshape(m, d)
    hx = _rmsnorm(x2, norm_g)
    proj = _matmul(hx, w_in, l)
    n_in = proj.shape[1]
    proj3 = proj.reshape(bsz, s, n_in)
    col = lambda i: i * mw
    u_tm = proj3[:, :, :mw].transpose(1, 0, 2)
    if cache is None:
        h0_re = jnp.zeros((bsz, g, SSM_STATE), F32)
        h0_im = h0_re
    else:
        h0_re, h0_im = cache[4], cache[5]
    y_tm, ht_re, ht_im = _s5_branch(u_tm, h0_re, h0_im, ab_re, ab_im, bbt_re, bbt_im, c_re, c_im, d_skip)
    lhs_a = _glu_gate(y_tm, w_glu, l, b_glu, proj, col(1), bsz, s)
    k_c = proj3[:, :, col(7):col(8)]
    v_c = proj3[:, :, col(8):col(9)]
    v_b = proj3[:, :, col(4):col(5)]
    if cache is None:
        lhs_b, band_k = _band_prompt(proj3, col(2), col(3), col(4), col(5), rel_bias, qn_g, kn_g)
        keep = min(BAND_LEN, s)
        band_v = v_b[:, s - keep:]
        lhs_c = _sb_prompt(proj3, col(6), col(7), col(8), col(9), nh)
    else:
        sb_k, sb_v, bk, bv = cache[:4]
        lhs_b, band_k = _band_sample(proj3, col(2), col(3), col(4), col(5), bk, bv, l, rel_bias, qn_g, kn_g)
        band_v = v_b
        lhs_c = _sb_sample(proj3, col(6), col(9), k_c, v_c, sb_k, sb_v, l, nh)
    mixed = _branch_merge(lhs_a, lhs_b.reshape(m, mw), lhs_c.reshape(m, mw), w_br_a, w_br_b, w_br_c, l,
                          proj, gate_b, col(10))
    y = _matmul_residual(mixed, w_out, l, x2).reshape(bsz, s, d)
    heads = lambda t: t.reshape(bsz, t.shape[1], nh, HEAD_DIM)
    return y, (heads(k_c), heads(v_c), heads(band_k), heads(band_v), ht_re, ht_im)


def kernel(x_prompt, x_sample, cache_sb_k, cache_sb_v, cache_band_k, cache_band_v, state_ssm_re, state_ssm_im, norm_g, w_in, ssm_a_re, ssm_a_im, ssm_log_dt, ssm_b_re, ssm_b_im, ssm_c_re, ssm_c_im, ssm_d, w_glu, b_glu, q_norm_g, k_norm_g, rel_bias, w_br_a, w_br_b, w_br_c, gate_b, w_out):
    depth = w_in.shape[0]
    yp, ys = x_prompt, x_sample
    p_states, s_states = [], []
    w_in, w_glu, w_br_a, w_br_b, w_br_c, w_out = (t.astype(BF16) for t in (w_in, w_glu, w_br_a, w_br_b, w_br_c, w_out))
    caches = (cache_sb_k, cache_sb_v, cache_band_k, cache_band_v)
    for l in range(depth):
        ssm = _discretize(ssm_a_re[l], ssm_a_im[l], ssm_log_dt[l], ssm_b_re[l], ssm_b_im[l])
        w = (norm_g[l], w_in, (*ssm, ssm_c_re[l], ssm_c_im[l], ssm_d[l]), w_glu, b_glu[l], q_norm_g[l], k_norm_g[l],
             rel_bias[l], w_br_a, w_br_b, w_br_c, gate_b[l], w_out)
        yp, sp = _layer(yp, l, w, None)
        ys, ss = _layer(ys, l, w, (*caches, state_ssm_re[l], state_ssm_im[l]))
        p_states.append(sp)
        s_states.append(ss)
    stk = lambda states, i: jnp.stack([st[i] for st in states], axis=0)
    return (yp, ys,
            *(stk(p_states, i) for i in range(6)),
            *(stk(s_states, i) for i in range(6)))
```

```python
import functools
import math

import jax
import jax.numpy as jnp
import numpy as np
from jax import lax
from jax.experimental import pallas as pl
from jax.experimental.pallas import tpu as pltpu

F32 = jnp.float32
BF16 = jnp.bfloat16

CHUNK = 64
HEAD_DIM = 128
SSM_GROUP = 16
SSM_STATE = 64
BAND_CHUNKS = 8
BAND_LEN = BAND_CHUNKS * CHUNK
MAX_REL = 128
N_BRANCH = 3
RMS_EPS = 1e-6
NEG_INF = -1e30

LANES = 128
SUBLANES = 8
VMEM_LIMIT_BYTES = 56 * 1024 * 1024
MM_TILE = 1024
ROW_TILE = 512
SSM_CH_BLOCK = 128
SSM_GROUPS_PER_BLOCK = SSM_CH_BLOCK // SSM_GROUP
SSM_LANES = SSM_GROUPS_PER_BLOCK * SSM_STATE
S5_RING = 3
S5_CHUNK = 128
SB_Q_BLOCK = 2048
SB_K_BLOCK = 128
SB_GROUP = 2 * SB_K_BLOCK
SB_PROMPT_UNROLL = 4
SB_SAMPLE_UNROLL = 4
BAND_Q_BLOCK = 4 * BAND_LEN
BAND_Q_SUB = 2 * CHUNK
BAND_WINDOW = BAND_LEN + BAND_Q_SUB


def _params(*sem):
    return pltpu.CompilerParams(dimension_semantics=sem, vmem_limit_bytes=VMEM_LIMIT_BYTES)


def _row_tile(m, target, align=SUBLANES):
    if m <= target:
        return m
    for t in range(target - target % align, 0, -align):
        if m % t == 0:
            return t
    raise ValueError((m, target, align))


def _rmsnorm_kernel(x_ref, g_ref, o_ref):
    x = x_ref[...]
    r = lax.rsqrt(jnp.mean(x * x, axis=-1, keepdims=True) + RMS_EPS)
    o_ref[...] = (x * r * g_ref[...]).astype(o_ref.dtype)


def _rmsnorm(x, g):
    m, d = x.shape
    tm = _row_tile(m, ROW_TILE)
    return pl.pallas_call(
        _rmsnorm_kernel,
        grid=(m // tm,),
        in_specs=[pl.BlockSpec((tm, d), lambda i: (i, 0)),
                  pl.BlockSpec((1, d), lambda i: (0, 0))],
        out_specs=pl.BlockSpec((tm, d), lambda i: (i, 0)),
        out_shape=jax.ShapeDtypeStruct((m, d), BF16),
        compiler_params=_params("parallel"),
        name="rmsnorm",
    )(x, g.reshape(1, d))


def _mm_kernel(x_ref, w_ref, o_ref):
    o_ref[...] = jnp.dot(x_ref[...], w_ref[...], preferred_element_type=F32).astype(o_ref.dtype)


def _matmul(x, w, l, out_dtype=F32):
    m, k = x.shape
    _, _, n = w.shape
    tm = _row_tile(m, MM_TILE)
    tn = _row_tile(n, MM_TILE, LANES)
    return pl.pallas_call(
        _mm_kernel,
        grid=(m // tm, n // tn),
        in_specs=[pl.BlockSpec((tm, k), lambda i, j: (i, 0)),
                  pl.BlockSpec((None, k, tn), lambda i, j: (l, 0, j))],
        out_specs=pl.BlockSpec((tm, tn), lambda i, j: (i, j)),
        out_shape=jax.ShapeDtypeStruct((m, n), out_dtype),
        compiler_params=_params("parallel", "arbitrary"),
        name="in_proj",
    )(x, w)


def _mm_res_kernel(x_ref, w_ref, r_ref, o_ref):
    o_ref[...] = r_ref[...] + jnp.dot(x_ref[...], w_ref[...], preferred_element_type=F32)


def _matmul_residual(x, w, l, res):
    m, k = x.shape
    _, _, n = w.shape
    tm = _row_tile(m, MM_TILE)
    tn = _row_tile(n, MM_TILE, LANES)
    return pl.pallas_call(
        _mm_res_kernel,
        grid=(m // tm, n // tn),
        in_specs=[pl.BlockSpec((tm, k), lambda i, j: (i, 0)),
                  pl.BlockSpec((None, k, tn), lambda i, j: (l, 0, j)),
                  pl.BlockSpec((tm, tn), lambda i, j: (i, j))],
        out_specs=pl.BlockSpec((tm, tn), lambda i, j: (i, j)),
        out_shape=jax.ShapeDtypeStruct((m, n), F32),
        compiler_params=_params("parallel", "arbitrary"),
        name="out_proj",
    )(x, w, res)


def _branch_kernel(la_ref, lb_ref, lc_ref, wa_ref, wb_ref, wc_ref,
                   ga_ref, gb_ref, gc_ref, ba_ref, bb_ref, bc_ref, o_ref):
    def one(l_ref, w_ref, g_ref, b_ref):
        o = jnp.dot(l_ref[...], w_ref[...], preferred_element_type=F32)
        return jax.nn.sigmoid(g_ref[...] + b_ref[...]) * o
    mixed = one(la_ref, wa_ref, ga_ref, ba_ref) + one(lb_ref, wb_ref, gb_ref, bb_ref)
    mixed = mixed + one(lc_ref, wc_ref, gc_ref, bc_ref)
    o_ref[...] = mixed.astype(o_ref.dtype)


def _branch_merge(lhs_a, lhs_b, lhs_c, w_a, w_b, w_c, l, proj, gate_b, g_col0):
    m, kw = lhs_a.shape
    d = w_a.shape[2]
    tm = _row_tile(m, ROW_TILE)
    tn = _row_tile(math.gcd(d, g_col0), MM_TILE, LANES)
    assert g_col0 % tn == 0 and d % tn == 0
    gblk = g_col0 // tn
    nblk = d // tn
    lspec = pl.BlockSpec((tm, kw), lambda j, i: (i, 0))
    wspec = pl.BlockSpec((None, kw, tn), lambda j, i: (l, 0, j))
    gspecs = [pl.BlockSpec((tm, tn), functools.partial(lambda j, i, br: (i, gblk + br * nblk + j), br=br))
              for br in range(N_BRANCH)]
    bspecs = [pl.BlockSpec((1, tn), functools.partial(lambda j, i, br: (0, br * nblk + j), br=br))
              for br in range(N_BRANCH)]
    gb = gate_b.reshape(1, N_BRANCH * d)
    return pl.pallas_call(
        _branch_kernel,
        grid=(d // tn, m // tm),
        in_specs=[lspec, lspec, lspec, wspec, wspec, wspec, *gspecs, *bspecs],
        out_specs=pl.BlockSpec((tm, tn), lambda j, i: (i, j)),
        out_shape=jax.ShapeDtypeStruct((m, d), BF16),
        compiler_params=_params("parallel", "arbitrary"),
        name="branch_merge",
    )(lhs_a, lhs_b, lhs_c, w_a, w_b, w_c, proj, proj, proj, gb, gb, gb)


def _discretize_kernel(are_ref, aim_ref, ldt_ref, bre_ref, bim_ref,
                       abre_ref, abim_ref, bbre_ref, bbim_ref):
    a_re = are_ref[...]
    a_im = aim_ref[...]
    dt = jnp.exp(ldt_ref[...])
    mag = jnp.exp(a_re * dt)
    ab_re = mag * jnp.cos(a_im * dt)
    ab_im = mag * jnp.sin(a_im * dt)
    den = a_re * a_re + a_im * a_im
    nr = ab_re - 1.0
    k_re = (nr * a_re + ab_im * a_im) / den
    k_im = (ab_im * a_re - nr * a_im) / den
    abre_ref[...] = ab_re
    abim_ref[...] = ab_im
    b_re = bre_ref[...]
    b_im = bim_ref[...]
    bbre_ref[...] = k_re[:, None, :] * b_re - k_im[:, None, :] * b_im
    bbim_ref[...] = k_re[:, None, :] * b_im + k_im[:, None, :] * b_re


def _discretize(a_re, a_im, log_dt, b_re, b_im):
    g, n = a_re.shape
    p = b_re.shape[-1]
    gn = jax.ShapeDtypeStruct((g, n), F32)
    gpn = jax.ShapeDtypeStruct((g, p, n), F32)
    return pl.pallas_call(
        _discretize_kernel,
        out_shape=(gn, gn, gpn, gpn),
        name="s5_discretize",
    )(a_re, a_im, log_dt.reshape(g, 1), b_re.transpose(0, 2, 1), b_im.transpose(0, 2, 1))


def _block_diag(x):
    ncb, gl, r, c = x.shape
    eye = jnp.eye(gl, dtype=x.dtype)
    out = x[:, :, :, None, :] * eye[None, :, None, :, None]
    return out.reshape(ncb, gl * r, gl * c)


def _s5_kernel(u_ref, up_ref, h0_ref, a_ref, bblk_ref, cblk_ref, d_ref, y_ref, ht_ref, *scratch,
               t_steps, n_chunks):
    ring, h_ref = scratch[:S5_RING], scratch[S5_RING]
    t_idx = pl.program_id(1)

    @pl.when(t_idx == 0)
    def _():
        h_ref[...] = h0_ref[0]
        for buf in ring:
            buf[...] = jnp.zeros(buf.shape, F32)

    rows = t_steps * SUBLANES

    def stages(in_buf, scan_buf, out_buf):
        u = u_ref[...].reshape(rows, SSM_CH_BLOCK)
        in_buf[...] = jnp.dot(u.astype(BF16), bblk_ref[0], preferred_element_type=F32)

        u_out = up_ref[...].reshape(rows, SSM_CH_BLOCK)
        y = jnp.dot(out_buf[...].astype(BF16), cblk_ref[0], preferred_element_type=F32)
        y = y + d_ref[0] * u_out
        y_ref[...] = jax.nn.gelu(y).reshape(t_steps, SUBLANES, SSM_CH_BLOCK)

        a_re = a_ref[0, :, :SSM_LANES]
        a_im = a_ref[0, :, SSM_LANES:]
        h_re = h_ref[:, :SSM_LANES]
        h_im = h_ref[:, SSM_LANES:]
        s_re, s_im = h_re, h_im
        for t in range(t_steps):
            row = slice(t * SUBLANES, (t + 1) * SUBLANES)
            b_re = scan_buf[row, :SSM_LANES]
            b_im = scan_buf[row, SSM_LANES:]
            s_re, s_im = a_re * s_re - a_im * s_im + b_re, a_re * s_im + a_im * s_re + b_im
            scan_buf[row, :SSM_LANES] = s_re
            scan_buf[row, SSM_LANES:] = s_im
        in_range = (t_idx >= 1) & (t_idx <= n_chunks)
        h_ref[:, :SSM_LANES] = jnp.where(in_range, s_re, h_re)
        h_ref[:, SSM_LANES:] = jnp.where(in_range, s_im, h_im)
        ht_ref[0] = h_ref[...]

    for k in range(S5_RING):
        pl.when(t_idx % S5_RING == k)(functools.partial(
            stages, ring[k], ring[(k + S5_RING - 1) % S5_RING], ring[(k + S5_RING - 2) % S5_RING]))


def _s5_branch(u_tm, h0_re, h0_im, ab_re, ab_im, bbt_re, bbt_im, c_re, c_im, d_skip):
    s, b, w = u_tm.shape
    assert b == SUBLANES, "the scan keeps one stream per sublane"
    g, n = ab_re.shape
    assert n == SSM_STATE and w == g * SSM_GROUP and w % SSM_CH_BLOCK == 0
    ncb = w // SSM_CH_BLOCK
    gl = SSM_GROUPS_PER_BLOCK
    t_steps = _row_tile(s, S5_CHUNK)

    def lanes(x):
        r = x.shape[0]
        return x.reshape(r, ncb, gl * n).transpose(1, 0, 2)

    h0 = jnp.concatenate([lanes(h0_re), lanes(h0_im)], axis=-1)
    a_tab = jnp.concatenate([lanes(jnp.broadcast_to(ab_re[None], (SUBLANES, g, n))),
                             lanes(jnp.broadcast_to(ab_im[None], (SUBLANES, g, n)))], axis=-1)
    bblk = jnp.concatenate([_block_diag(bbt_re.reshape(ncb, gl, SSM_GROUP, n)),
                            _block_diag(bbt_im.reshape(ncb, gl, SSM_GROUP, n))], axis=-1).astype(BF16)
    cblk = jnp.concatenate([_block_diag(c_re.reshape(ncb, gl, SSM_GROUP, n).transpose(0, 1, 3, 2)),
                            _block_diag(-c_im.reshape(ncb, gl, SSM_GROUP, n).transpose(0, 1, 3, 2))],
                           axis=1).astype(BF16)
    d_tab = d_skip.reshape(ncb, 1, SSM_CH_BLOCK)
    st_spec = pl.BlockSpec((1, SUBLANES, 2 * SSM_LANES), lambda c, t: (c, 0, 0))
    n_chunks = s // t_steps
    chunk = (t_steps, SUBLANES, SSM_CH_BLOCK)
    last = n_chunks - 1
    y_tm, ht = pl.pallas_call(
        functools.partial(_s5_kernel, t_steps=t_steps, n_chunks=n_chunks),
        grid=(ncb, n_chunks + S5_RING - 1),
        in_specs=[pl.BlockSpec(chunk, lambda c, t: (jnp.minimum(t, last), 0, c)),
                  pl.BlockSpec(chunk, lambda c, t: (jnp.clip(t - 2, 0, last), 0, c)),
                  st_spec, st_spec,
                  pl.BlockSpec((1, SSM_CH_BLOCK, 2 * SSM_LANES), lambda c, t: (c, 0, 0)),
                  pl.BlockSpec((1, 2 * SSM_LANES, SSM_CH_BLOCK), lambda c, t: (c, 0, 0)),
                  pl.BlockSpec((1, 1, SSM_CH_BLOCK), lambda c, t: (c, 0, 0))],
        out_specs=[pl.BlockSpec(chunk, lambda c, t: (jnp.clip(t - 2, 0, last), 0, c)), st_spec],
        out_shape=(jax.ShapeDtypeStruct((s, b, w), F32),
                   jax.ShapeDtypeStruct((ncb, SUBLANES, 2 * SSM_LANES), F32)),
        scratch_shapes=[pltpu.VMEM((t_steps * SUBLANES, 2 * SSM_LANES), F32)] * S5_RING
                       + [pltpu.VMEM((SUBLANES, 2 * SSM_LANES), F32)],
        compiler_params=_params("parallel", "arbitrary"),
        name="s5_scan",
    )(u_tm, u_tm, h0, a_tab, bblk, cblk, d_tab)

    def unlanes(x):
        return x.transpose(1, 0, 2).reshape(b, g, n)

    return y_tm, unlanes(ht[..., :SSM_LANES]), unlanes(ht[..., SSM_LANES:])


def _glu_kernel(y_ref, w_ref, b_ref, z_ref, o_ref):
    y = y_ref[...]
    gate = jnp.dot(y.astype(BF16), w_ref[...], preferred_element_type=F32) + b_ref[...]
    o_ref[...] = (y * jax.nn.sigmoid(gate) * jax.nn.silu(z_ref[...])).astype(o_ref.dtype)


def _glu_gate(y_tm, w_glu, l, b_glu, proj, z_col0, bsz, s):
    w = w_glu.shape[1]
    tm = _row_tile(s, ROW_TILE)
    nt = s // tm
    zblk = z_col0 // w
    return pl.pallas_call(
        _glu_kernel,
        grid=(bsz, nt),
        in_specs=[pl.BlockSpec((tm, w), lambda b, t: (t, b)),
                  pl.BlockSpec((None, w, w), lambda b, t: (l, 0, 0)),
                  pl.BlockSpec((1, w), lambda b, t: (0, 0)),
                  pl.BlockSpec((tm, w), lambda b, t: (b * nt + t, zblk))],
        out_specs=pl.BlockSpec((tm, w), lambda b, t: (b * nt + t, 0)),
        out_shape=jax.ShapeDtypeStruct((bsz * s, w), BF16),
        compiler_params=_params("parallel", "arbitrary"),
        name="glu_gate",
    )(y_tm.reshape(s, bsz * w), w_glu, b_glu.reshape(1, w), proj)


def _head_rms(x, g):
    r = lax.rsqrt(jnp.mean(x * x, axis=-1, keepdims=True) + RMS_EPS)
    return x * r * g


def _qk(q, k):
    return lax.dot_general(q, k, (((1,), (1,)), ((), ())), preferred_element_type=F32)


def _band_prompt_kernel(q_ref, kp_ref, kc_ref, vp_ref, vc_ref, z_ref, bias_ref, qg_ref, kg_ref,
                        o_ref, kn_ref):
    i = pl.program_id(2)
    scale = 1.0 / math.sqrt(HEAD_DIM)
    qn = _head_rms(q_ref[0], qg_ref[...]).astype(BF16)
    kpn = _head_rms(kp_ref[0], kg_ref[...])
    kcn = _head_rms(kc_ref[0], kg_ref[...])
    tq = q_ref.shape[1]
    kn_ref[0] = kcn[tq - BAND_LEN:]
    k_all = jnp.concatenate([kpn.astype(BF16), kcn.astype(BF16)], axis=0)
    v_all = jnp.concatenate([vp_ref[0].astype(BF16), vc_ref[0].astype(BF16)], axis=0)
    bias = bias_ref[0]
    col = lax.broadcasted_iota(jnp.int32, (BAND_Q_SUB, BAND_WINDOW), 1)
    z = z_ref[0]
    n_sub = tq // BAND_Q_SUB
    rows = [slice(p * BAND_Q_SUB, (p + 1) * BAND_Q_SUB) for p in range(n_sub)]
    wins = [slice(p * BAND_Q_SUB, p * BAND_Q_SUB + BAND_WINDOW) for p in range(n_sub)]
    scores = []
    for p in range(n_sub):
        s = _qk(qn[rows[p]], k_all[wins[p]]) * scale + bias
        if p * BAND_Q_SUB < BAND_LEN:
            n_missing = jnp.where(i == 0, BAND_LEN - p * BAND_Q_SUB, 0)
            s = jnp.where(col < n_missing, NEG_INF, s)
        scores.append(s)
    probs = [jnp.exp(s - jnp.max(s, axis=-1, keepdims=True)) for s in scores]
    outs = [jnp.dot(pr.astype(BF16), v_all[wins[p]], preferred_element_type=F32) for p, pr in enumerate(probs)]
    for p in range(n_sub):
        l = jnp.sum(probs[p], axis=-1, keepdims=True)
        o_ref[0, rows[p], :] = ((outs[p] / l) * jax.nn.silu(z[rows[p]])).astype(o_ref.dtype)


def _static_take(tab, idx):
    nh = tab.shape[0]
    idx = np.asarray(idx)
    pieces = []
    i = 0
    while i < len(idx):
        step = int(idx[i + 1] - idx[i]) if i + 1 < len(idx) else 0
        if step not in (-1, 0, 1):
            step = 0
            j = i + 1
        else:
            j = i + 1
            while j < len(idx) and idx[j] - idx[j - 1] == step:
                j += 1
        n = j - i
        a = int(idx[i])
        if step == 0:
            pieces.append(jnp.broadcast_to(tab[:, a:a + 1], (nh, n)))
        elif step == 1:
            pieces.append(tab[:, a:a + n])
        else:
            pieces.append(tab[:, a - n + 1:a + 1][:, ::-1])
        i = j
    return jnp.concatenate(pieces, axis=1)


def _toeplitz_bias(rel_bias, rows, cols, offset):
    nh = rel_bias.shape[0]
    period = rows + cols
    m = np.arange(period)
    c_minus_r = np.where(m < cols, m, m - period)
    rel = np.clip(offset - c_minus_r, -MAX_REL, MAX_REL) + MAX_REL
    vec = _static_take(rel_bias.astype(F32), rel)
    return jnp.tile(vec, (1, rows))[:, :rows * (period - 1)].reshape(nh, rows, period - 1)[:, :, :cols]


def _band_bias_prompt(rel_bias):
    r = np.arange(BAND_Q_SUB)
    c = np.arange(BAND_WINDOW)
    jb = c[None, :] - (r // CHUNK * CHUNK)[:, None]
    inside = (jb >= 0) & (jb < (BAND_CHUNKS + 1) * CHUNK)
    toep = _toeplitz_bias(rel_bias, BAND_Q_SUB, BAND_WINDOW, BAND_LEN)
    return jnp.where(jnp.asarray(inside)[None], toep, NEG_INF)


def _band_prompt(proj3, q_col0, k_col0, v_col0, z_col0, rel_bias, qn_g, kn_g):
    bsz, s, _ = proj3.shape
    nh = rel_bias.shape[0]
    tq = _row_tile(s, BAND_Q_BLOCK, BAND_LEN)
    assert s % tq == 0 and tq % BAND_LEN == 0
    nq = s // tq
    per = tq // BAND_LEN
    qb, kb, vb, zb = (c // HEAD_DIM for c in (q_col0, k_col0, v_col0, z_col0))
    bias = _band_bias_prompt(rel_bias)
    blk = (1, tq, HEAD_DIM)
    prev = (1, BAND_LEN, HEAD_DIM)
    gspec = pl.BlockSpec((1, HEAD_DIM), lambda h, b, i: (0, 0))
    return pl.pallas_call(
        _band_prompt_kernel,
        grid=(nh, bsz, nq),
        in_specs=[pl.BlockSpec(blk, lambda h, b, i: (b, i, qb + h)),
                  pl.BlockSpec(prev, lambda h, b, i: (b, jnp.maximum(i * per - 1, 0), kb + h)),
                  pl.BlockSpec(blk, lambda h, b, i: (b, i, kb + h)),
                  pl.BlockSpec(prev, lambda h, b, i: (b, jnp.maximum(i * per - 1, 0), vb + h)),
                  pl.BlockSpec(blk, lambda h, b, i: (b, i, vb + h)),
                  pl.BlockSpec(blk, lambda h, b, i: (b, i, zb + h)),
                  pl.BlockSpec((1, BAND_Q_SUB, BAND_WINDOW), lambda h, b, i: (h, 0, 0)),
                  gspec, gspec],
        out_specs=[pl.BlockSpec(blk, lambda h, b, i: (b, i, h)),
                   pl.BlockSpec(prev, lambda h, b, i: (b, 0, h))],
        out_shape=(jax.ShapeDtypeStruct((bsz, s, nh * HEAD_DIM), BF16),
                   jax.ShapeDtypeStruct((bsz, BAND_LEN, nh * HEAD_DIM), F32)),
        compiler_params=_params("parallel", "parallel", "arbitrary"),
        name="band_prompt",
    )(proj3, proj3, proj3, proj3, proj3, proj3, bias, qn_g.reshape(1, HEAD_DIM), kn_g.reshape(1, HEAD_DIM))


def _copy_head_rows(head, srcs, dsts):
    n_rows = dsts[0].shape[0]
    nh = srcs[0].shape[1] // n_rows
    for hh in range(nh):
        @pl.when(head == hh)
        def _(hh=hh):
            for src, dst in zip(srcs, dsts):
                dst[...] = src[0, pl.ds(hh, n_rows, stride=nh), :]


def _band_sample_kernel(q_ref, k_ref, v_ref, ck_ref, cv_ref, z_ref, bias_ref, qg_ref, kg_ref,
                        o_ref, kn_ref, kbuf_ref, vbuf_ref, *, lc):
    scale = 1.0 / math.sqrt(HEAD_DIM)
    n = q_ref.shape[1]
    _copy_head_rows(pl.program_id(1), (ck_ref, cv_ref), (kbuf_ref, vbuf_ref))
    qn = _head_rms(q_ref[0], qg_ref[...]).astype(BF16)
    kn = _head_rms(k_ref[0], kg_ref[...])
    kn_ref[0] = kn
    zpad = jnp.zeros((LANES - n, HEAD_DIM), F32)
    kn = jnp.concatenate([kn, zpad], axis=0)
    v_new = jnp.concatenate([v_ref[0], zpad], axis=0)
    s_p = _qk(qn, kbuf_ref[...].astype(BF16)) * scale + bias_ref[0, :, :lc]
    s_c = _qk(qn, kn.astype(BF16)) * scale + bias_ref[0, :, lc:]
    m = jnp.maximum(jnp.max(s_p, axis=-1, keepdims=True), jnp.max(s_c, axis=-1, keepdims=True))
    p_p = jnp.exp(s_p - m)
    p_c = jnp.exp(s_c - m)
    l = jnp.sum(p_p, axis=-1, keepdims=True) + jnp.sum(p_c, axis=-1, keepdims=True)
    o = jnp.dot(p_p.astype(BF16), vbuf_ref[...].astype(BF16), preferred_element_type=F32)
    o = o + jnp.dot(p_c.astype(BF16), v_new.astype(BF16), preferred_element_type=F32)
    o_ref[0] = ((o / l) * jax.nn.silu(z_ref[0])).astype(o_ref.dtype)


def _band_sample(proj3, q_col0, k_col0, v_col0, z_col0, cache_k, cache_v, l, rel_bias, qn_g, kn_g):
    bsz, n, _ = proj3.shape
    lc = cache_k.shape[2]
    nh = rel_bias.shape[0]
    rows = lambda t: t.reshape(*t.shape[:2], lc * nh, HEAD_DIM)
    cache_k, cache_v = rows(cache_k), rows(cache_v)
    qb, kb, vb, zb = (c // HEAD_DIM for c in (q_col0, k_col0, v_col0, z_col0))
    bias = _toeplitz_bias(rel_bias, n, lc + n, lc)
    assert n <= LANES
    bias = jnp.pad(bias, ((0, 0), (0, 0), (0, LANES - n)), constant_values=NEG_INF)
    blk = (1, n, HEAD_DIM)
    cblk = (None, 1, lc * nh, HEAD_DIM)
    gspec = pl.BlockSpec((1, HEAD_DIM), lambda b, h: (0, 0))
    return pl.pallas_call(
        functools.partial(_band_sample_kernel, lc=lc),
        grid=(bsz, nh),
        in_specs=[pl.BlockSpec(blk, lambda b, h: (b, 0, qb + h)),
                  pl.BlockSpec(blk, lambda b, h: (b, 0, kb + h)),
                  pl.BlockSpec(blk, lambda b, h: (b, 0, vb + h)),
                  pl.BlockSpec(cblk, lambda b, h: (l, b, 0, 0)),
                  pl.BlockSpec(cblk, lambda b, h: (l, b, 0, 0)),
                  pl.BlockSpec(blk, lambda b, h: (b, 0, zb + h)),
                  pl.BlockSpec((1, n, lc + LANES), lambda b, h: (h, 0, 0)),
                  gspec, gspec],
        out_specs=[pl.BlockSpec(blk, lambda b, h: (b, 0, h)),
                   pl.BlockSpec(blk, lambda b, h: (b, 0, h))],
        out_shape=(jax.ShapeDtypeStruct((bsz, n, nh * HEAD_DIM), BF16),
                   jax.ShapeDtypeStruct((bsz, n, nh * HEAD_DIM), F32)),
        scratch_shapes=[pltpu.VMEM((lc, HEAD_DIM), F32)] * 2,
        compiler_params=_params("parallel", "arbitrary"),
        name="band_sample",
    )(proj3, proj3, proj3, cache_k, cache_v, proj3, bias, qn_g.reshape(1, HEAD_DIM), kn_g.reshape(1, HEAD_DIM))


def _suffix_matrix():
    kb = SB_K_BLOCK
    j = lax.broadcasted_iota(jnp.int32, (2 * kb, 2 * kb), 0) % kb
    c = lax.broadcasted_iota(jnp.int32, (2 * kb, 2 * kb), 1)
    return jnp.where((c >= kb) | (j >= c), -1.0, 0.0).astype(BF16)


def _sb_groups(q, groups, carry, acc, umat):
    scale = 1.0 / math.sqrt(HEAD_DIM)
    kb = SB_K_BLOCK
    zs = [_qk(q, k.astype(BF16)) * scale for k, _, _ in groups]
    fails = [jnp.maximum(z, 0.0) + jnp.log(1.0 + jnp.exp(-jnp.abs(z))) for z in zs]
    fails = [f if causal is None else jnp.where(causal, f, 0.0) for f, (_, _, causal) in zip(fails, groups)]
    his = [f.astype(BF16) for f in fails]
    los = [(f - hi.astype(F32)).astype(BF16) for f, hi in zip(fails, his)]
    staged = []
    for z, hi, lo, (_, _, causal) in zip(zs, his, los, groups):
        parts = []
        for b in reversed(range(SB_GROUP // kb)):
            cols = slice(b * kb, (b + 1) * kb)
            sums = jnp.dot(jnp.concatenate([hi[:, cols], lo[:, cols]], axis=1), umat, preferred_element_type=F32)
            parts.append((b, z[:, cols] + sums[:, :kb], sums[:, kb:]))
        staged.append((parts, causal))
    ws = []
    for parts, causal in staged:
        log_w = [None] * len(parts)
        for b, part, total in parts:
            log_w[b] = part + carry
            carry = carry + total
        w = jnp.exp(jnp.concatenate(log_w, axis=1))
        if causal is not None:
            w = jnp.where(causal, w, 0.0)
        ws.append(w.astype(BF16))
    vs = [v.astype(BF16) for _, v, _ in groups]
    w_all = ws[0] if len(ws) == 1 else jnp.concatenate(ws, axis=1)
    v_all = vs[0] if len(vs) == 1 else jnp.concatenate(vs, axis=0)
    return carry, acc + jnp.dot(w_all, v_all, preferred_element_type=F32)


def _sb_prompt_kernel(q_ref, k_ref, v_ref, z_ref, o_ref):
    qi = pl.program_id(2)
    tq = q_ref.shape[1]
    q = q_ref[0].astype(BF16)
    umat = _suffix_matrix()
    q0 = qi * tq
    carry = jnp.zeros((tq, SB_K_BLOCK), F32)
    acc = jnp.zeros((tq, HEAD_DIM), F32)

    def load(k0, causal):
        k0 = pl.multiple_of(k0, SB_GROUP)
        return k_ref[0, pl.ds(k0, SB_GROUP), :], v_ref[0, pl.ds(k0, SB_GROUP), :], causal

    n_diag = tq // SB_GROUP
    assert n_diag % SB_PROMPT_UNROLL == 0
    for d in reversed(range(n_diag)):
        r0 = d * SB_GROUP
        k, v, _ = load(q0 + r0, None)
        causal = (lax.broadcasted_iota(jnp.int32, (tq - r0, SB_GROUP), 1)
                  < lax.broadcasted_iota(jnp.int32, (tq - r0, SB_GROUP), 0))
        c_lo, a_lo = _sb_groups(q[r0:], [(k, v, causal)], carry[r0:], acc[r0:], umat)
        carry = c_lo if d == 0 else jnp.concatenate([carry[:r0], c_lo], axis=0)
        acc = a_lo if d == 0 else jnp.concatenate([acc[:r0], a_lo], axis=0)

    def body(j, ca):
        groups = [load(q0 - (j * SB_PROMPT_UNROLL + u + 1) * SB_GROUP, None) for u in range(SB_PROMPT_UNROLL)]
        return _sb_groups(q, groups, ca[0], ca[1], umat)

    carry, acc = lax.fori_loop(0, qi * (n_diag // SB_PROMPT_UNROLL), body, (carry, acc))
    o_ref[0] = (acc * jax.nn.silu(z_ref[0])).astype(o_ref.dtype)


def _sb_prompt(proj3, q_col0, k_col0, v_col0, z_col0, nh):
    bsz, s, _ = proj3.shape
    tq = _row_tile(s, SB_Q_BLOCK, SB_GROUP * SB_PROMPT_UNROLL)
    qb, kb, vb, zb = (c // HEAD_DIM for c in (q_col0, k_col0, v_col0, z_col0))
    blk = (1, tq, HEAD_DIM)
    full = (1, s, HEAD_DIM)
    return pl.pallas_call(
        _sb_prompt_kernel,
        grid=(bsz, nh, s // tq),
        in_specs=[pl.BlockSpec(blk, lambda b, h, i: (b, i, qb + h)),
                  pl.BlockSpec(full, lambda b, h, i: (b, 0, kb + h)),
                  pl.BlockSpec(full, lambda b, h, i: (b, 0, vb + h)),
                  pl.BlockSpec(blk, lambda b, h, i: (b, i, zb + h))],
        out_specs=pl.BlockSpec(blk, lambda b, h, i: (b, i, h)),
        out_shape=jax.ShapeDtypeStruct((bsz, s, nh * HEAD_DIM), BF16),
        compiler_params=_params("parallel", "parallel", "arbitrary"),
        name="sb_prompt",
    )(proj3, proj3, proj3, proj3)


def _sb_sample_kernel(q_ref, k_ref, v_ref, ck_ref, cv_ref, z_ref, o_ref, kbuf_ref, vbuf_ref, *, past_len):
    n = q_ref.shape[1]
    _copy_head_rows(pl.program_id(1), (ck_ref, cv_ref), (kbuf_ref, vbuf_ref))

    q = q_ref[0].astype(BF16)
    umat = _suffix_matrix()
    row = lax.broadcasted_iota(jnp.int32, (n, SB_GROUP), 0)
    col = lax.broadcasted_iota(jnp.int32, (n, SB_GROUP), 1)
    carry = jnp.zeros((n, SB_K_BLOCK), F32)
    acc = jnp.zeros((n, HEAD_DIM), F32)
    n_past = past_len // SB_GROUP
    unroll = math.gcd(n_past, SB_SAMPLE_UNROLL)

    def past_groups(first):
        groups = []
        for u in range(unroll):
            k0 = pl.multiple_of(past_len - (first + u + 1) * SB_GROUP, SB_GROUP)
            groups.append((kbuf_ref[pl.ds(k0, SB_GROUP), :], vbuf_ref[pl.ds(k0, SB_GROUP), :], None))
        return groups

    carry, acc = _sb_groups(q, [(k_ref[0], v_ref[0], col < row)] + past_groups(0), carry, acc, umat)
    carry, acc = lax.fori_loop(1, n_past // unroll, lambda j, ca: _sb_groups(q, past_groups(j * unroll), *ca, umat),
                               (carry, acc))
    o_ref[0] = (acc * jax.nn.silu(z_ref[0])).astype(o_ref.dtype)


def _sb_sample(proj3, q_col0, z_col0, k_new, v_new, cache_k, cache_v, l, nh):
    bsz, n, _ = proj3.shape
    past_len = cache_k.shape[2]
    assert past_len % SB_GROUP == 0 and n <= SB_GROUP
    qb, zb = q_col0 // HEAD_DIM, z_col0 // HEAD_DIM
    pad = ((0, 0), (0, SB_GROUP - n), (0, 0))
    k_pad = jnp.pad(k_new, pad)
    v_pad = jnp.pad(v_new, pad)
    blk = (1, n, HEAD_DIM)
    own = (1, SB_GROUP, HEAD_DIM)
    past = (None, 1, past_len * nh, HEAD_DIM)
    rows = lambda t: t.reshape(*t.shape[:2], past_len * nh, HEAD_DIM)
    cache_k, cache_v = rows(cache_k), rows(cache_v)
    return pl.pallas_call(
        functools.partial(_sb_sample_kernel, past_len=past_len),
        grid=(bsz, nh),
        in_specs=[pl.BlockSpec(blk, lambda b, h: (b, 0, qb + h)),
                  pl.BlockSpec(own, lambda b, h: (b, 0, h)),
                  pl.BlockSpec(own, lambda b, h: (b, 0, h)),
                  pl.BlockSpec(past, lambda b, h: (l, b, 0, 0)),
                  pl.BlockSpec(past, lambda b, h: (l, b, 0, 0)),
                  pl.BlockSpec(blk, lambda b, h: (b, 0, zb + h))],
        out_specs=pl.BlockSpec(blk, lambda b, h: (b, 0, h)),
        out_shape=jax.ShapeDtypeStruct((bsz, n, nh * HEAD_DIM), BF16),
        scratch_shapes=[pltpu.VMEM((past_len, HEAD_DIM), F32)] * 2,
        compiler_params=_params("parallel", "arbitrary"),
        name="sb_sample",
    )(proj3, k_pad, v_pad, cache_k, cache_v, proj3)


def _layer(x, l, w, cache):
    (norm_g, w_in, ssm, w_glu, b_glu, qn_g, kn_g, rel_bias, w_br_a, w_br_b, w_br_c, gate_b, w_out) = w
    ab_re, ab_im, bbt_re, bbt_im, c_re, c_im, d_skip = ssm
    bsz, s, d = x.shape
    mw = w_glu.shape[1]
    nh = mw // HEAD_DIM
    g = ab_re.shape[0]
    m = bsz * s
    x2 = x.reshape(m, d)
    hx = _rmsnorm(x2, norm_g)
    proj = _matmul(hx, w_in, l)
    n_in = proj.shape[1]
    proj3 = proj.reshape(bsz, s, n_in)
    col = lambda i: i * mw
    u_tm = proj3[:, :, :mw].transpose(1, 0, 2)
    if cache is None:
        h0_re = jnp.zeros((bsz, g, SSM_STATE), F32)
        h0_im = h0_re
    else:
        h0_re, h0_im = cache[4], cache[5]
    y_tm, ht_re, ht_im = _s5_branch(u_tm, h0_re, h0_im, ab_re, ab_im, bbt_re, bbt_im, c_re, c_im, d_skip)
    lhs_a = _glu_gate(y_tm, w_glu, l, b_glu, proj, col(1), bsz, s)
    k_c = proj3[:, :, col(7):col(8)]
    v_c = proj3[:, :, col(8):col(9)]
    v_b = proj3[:, :, col(4):col(5)]
    if cache is None:
        lhs_b, band_k = _band_prompt(proj3, col(2), col(3), col(4), col(5), rel_bias, qn_g, kn_g)
        keep = min(BAND_LEN, s)
        band_v = v_b[:, s - keep:]
        lhs_c = _sb_prompt(proj3, col(6), col(7), col(8), col(9), nh)
    else:
        sb_k, sb_v, bk, bv = cache[:4]
        lhs_b, band_k = _band_sample(proj3, col(2), col(3), col(4), col(5), bk, bv, l, rel_bias, qn_g, kn_g)
        band_v = v_b
        lhs_c = _sb_sample(proj3, col(6), col(9), k_c, v_c, sb_k, sb_v, l, nh)
    mixed = _branch_merge(lhs_a, lhs_b.reshape(m, mw), lhs_c.reshape(m, mw), w_br_a, w_br_b, w_br_c, l,
                          proj, gate_b, col(10))
    y = _matmul_residual(mixed, w_out, l, x2).reshape(bsz, s, d)
    heads = lambda t: t.reshape(bsz, t.shape[1], nh, HEAD_DIM)
    return y, (heads(k_c), heads(v_c), heads(band_k), heads(band_v), ht_re, ht_im)


def kernel(x_prompt, x_sample, cache_sb_k, cache_sb_v, cache_band_k, cache_band_v, state_ssm_re, state_ssm_im, norm_g, w_in, ssm_a_re, ssm_a_im, ssm_log_dt, ssm_b_re, ssm_b_im, ssm_c_re, ssm_c_im, ssm_d, w_glu, b_glu, q_norm_g, k_norm_g, rel_bias, w_br_a, w_br_b, w_br_c, gate_b, w_out):
    depth = w_in.shape[0]
    yp, ys = x_prompt, x_sample
    p_states, s_states = [], []
    w_in, w_glu, w_br_a, w_br_b, w_br_c, w_out = (t.astype(BF16) for t in (w_in, w_glu, w_br_a, w_br_b, w_br_c, w_out))
    caches = (cache_sb_k, cache_sb_v, cache_band_k, cache_band_v)
    for l in range(depth):
        ssm = _discretize(ssm_a_re[l], ssm_a_im[l], ssm_log_dt[l], ssm_b_re[l], ssm_b_im[l])
        w = (norm_g[l], w_in, (*ssm, ssm_c_re[l], ssm_c_im[l], ssm_d[l]), w_glu, b_glu[l], q_norm_g[l], k_norm_g[l],
             rel_bias[l], w_br_a, w_br_b, w_br_c, gate_b[l], w_out)
        yp, sp = _layer(yp, l, w, None)
        ys, ss = _layer(ys, l, w, (*caches, state_ssm_re[l], state_ssm_im[l]))
        p_states.append(sp)
        s_states.append(ss)
    stk = lambda states, i: jnp.stack([st[i] for st in states], axis=0)
    return (yp, ys,
            *(stk(p_states, i) for i in range(6)),
            *(stk(s_states, i) for i in range(6)))
```

```python
import functools
import math

import jax
import jax.numpy as jnp
import numpy as np
from jax import lax
from jax.experimental import pallas as pl
from jax.experimental.pallas import tpu as pltpu

F32 = jnp.float32
BF16 = jnp.bfloat16

CHUNK = 64
HEAD_DIM = 128
SSM_GROUP = 16
SSM_STATE = 64
BAND_CHUNKS = 8
BAND_LEN = BAND_CHUNKS * CHUNK
MAX_REL = 128
N_BRANCH = 3
RMS_EPS = 1e-6
NEG_INF = -1e30

LANES = 128
SUBLANES = 8
VMEM_LIMIT_BYTES = 56 * 1024 * 1024
MM_TILE = 1024
ROW_TILE = 512
SSM_CH_BLOCK = 128
SSM_GROUPS_PER_BLOCK = SSM_CH_BLOCK // SSM_GROUP
SSM_LANES = SSM_GROUPS_PER_BLOCK * SSM_STATE
S5_RING = 3
S5_CHUNK = 128
SB_Q_BLOCK = 2048
SB_K_BLOCK = 128
SB_GROUP = 2 * SB_K_BLOCK
SB_PROMPT_UNROLL = 4
BAND_Q_BLOCK = 4 * BAND_LEN
BAND_Q_SUB = 2 * CHUNK
BAND_WINDOW = BAND_LEN + BAND_Q_SUB


def _params(*sem):
    return pltpu.CompilerParams(dimension_semantics=sem, vmem_limit_bytes=VMEM_LIMIT_BYTES)


def _row_tile(m, target, align=SUBLANES):
    if m <= target:
        return m
    for t in range(target - target % align, 0, -align):
        if m % t == 0:
            return t
    raise ValueError((m, target, align))


def _rmsnorm_kernel(x_ref, g_ref, o_ref):
    x = x_ref[...]
    r = lax.rsqrt(jnp.mean(x * x, axis=-1, keepdims=True) + RMS_EPS)
    o_ref[...] = (x * r * g_ref[...]).astype(o_ref.dtype)


def _rmsnorm(x, g):
    m, d = x.shape
    tm = _row_tile(m, ROW_TILE)
    return pl.pallas_call(
        _rmsnorm_kernel,
        grid=(m // tm,),
        in_specs=[pl.BlockSpec((tm, d), lambda i: (i, 0)),
                  pl.BlockSpec((1, d), lambda i: (0, 0))],
        out_specs=pl.BlockSpec((tm, d), lambda i: (i, 0)),
        out_shape=jax.ShapeDtypeStruct((m, d), BF16),
        compiler_params=_params("parallel"),
        name="rmsnorm",
    )(x, g.reshape(1, d))


def _mm_kernel(x_ref, w_ref, o_ref):
    o_ref[...] = jnp.dot(x_ref[...], w_ref[...], preferred_element_type=F32).astype(o_ref.dtype)


def _matmul(x, w, l, out_dtype=F32):
    m, k = x.shape
    _, _, n = w.shape
    tm = _row_tile(m, MM_TILE)
    tn = _row_tile(n, MM_TILE, LANES)
    return pl.pallas_call(
        _mm_kernel,
        grid=(m // tm, n // tn),
        in_specs=[pl.BlockSpec((tm, k), lambda i, j: (i, 0)),
                  pl.BlockSpec((None, k, tn), lambda i, j: (l, 0, j))],
        out_specs=pl.BlockSpec((tm, tn), lambda i, j: (i, j)),
        out_shape=jax.ShapeDtypeStruct((m, n), out_dtype),
        compiler_params=_params("parallel", "arbitrary"),
        name="in_proj",
    )(x, w)


def _mm_res_kernel(x_ref, w_ref, r_ref, o_ref):
    o_ref[...] = r_ref[...] + jnp.dot(x_ref[...], w_ref[...], preferred_element_type=F32)


def _matmul_residual(x, w, l, res):
    m, k = x.shape
    _, _, n = w.shape
    tm = _row_tile(m, MM_TILE)
    tn = _row_tile(n, MM_TILE, LANES)
    return pl.pallas_call(
        _mm_res_kernel,
        grid=(m // tm, n // tn),
        in_specs=[pl.BlockSpec((tm, k), lambda i, j: (i, 0)),
                  pl.BlockSpec((None, k, tn), lambda i, j: (l, 0, j)),
                  pl.BlockSpec((tm, tn), lambda i, j: (i, j))],
        out_specs=pl.BlockSpec((tm, tn), lambda i, j: (i, j)),
        out_shape=jax.ShapeDtypeStruct((m, n), F32),
        compiler_params=_params("parallel", "arbitrary"),
        name="out_proj",
    )(x, w, res)


def _branch_kernel(la_ref, lb_ref, lc_ref, wa_ref, wb_ref, wc_ref,
                   ga_ref, gb_ref, gc_ref, ba_ref, bb_ref, bc_ref, o_ref):
    def one(l_ref, w_ref, g_ref, b_ref):
        o = jnp.dot(l_ref[...], w_ref[...], preferred_element_type=F32)
        return jax.nn.sigmoid(g_ref[...] + b_ref[...]) * o
    mixed = one(la_ref, wa_ref, ga_ref, ba_ref) + one(lb_ref, wb_ref, gb_ref, bb_ref)
    mixed = mixed + one(lc_ref, wc_ref, gc_ref, bc_ref)
    o_ref[...] = mixed.astype(o_ref.dtype)


def _branch_merge(lhs_a, lhs_b, lhs_c, w_a, w_b, w_c, l, proj, gate_b, g_col0):
    m, kw = lhs_a.shape
    d = w_a.shape[2]
    tm = _row_tile(m, ROW_TILE)
    tn = _row_tile(math.gcd(d, g_col0), MM_TILE, LANES)
    assert g_col0 % tn == 0 and d % tn == 0
    gblk = g_col0 // tn
    nblk = d // tn
    lspec = pl.BlockSpec((tm, kw), lambda j, i: (i, 0))
    wspec = pl.BlockSpec((None, kw, tn), lambda j, i: (l, 0, j))
    gspecs = [pl.BlockSpec((tm, tn), functools.partial(lambda j, i, br: (i, gblk + br * nblk + j), br=br))
              for br in range(N_BRANCH)]
    bspecs = [pl.BlockSpec((1, tn), functools.partial(lambda j, i, br: (0, br * nblk + j), br=br))
              for br in range(N_BRANCH)]
    gb = gate_b.reshape(1, N_BRANCH * d)
    return pl.pallas_call(
        _branch_kernel,
        grid=(d // tn, m // tm),
        in_specs=[lspec, lspec, lspec, wspec, wspec, wspec, *gspecs, *bspecs],
        out_specs=pl.BlockSpec((tm, tn), lambda j, i: (i, j)),
        out_shape=jax.ShapeDtypeStruct((m, d), BF16),
        compiler_params=_params("parallel", "arbitrary"),
        name="branch_merge",
    )(lhs_a, lhs_b, lhs_c, w_a, w_b, w_c, proj, proj, proj, gb, gb, gb)


def _discretize_kernel(are_ref, aim_ref, ldt_ref, bre_ref, bim_ref,
                       abre_ref, abim_ref, bbre_ref, bbim_ref):
    a_re = are_ref[...]
    a_im = aim_ref[...]
    dt = jnp.exp(ldt_ref[...])
    mag = jnp.exp(a_re * dt)
    ab_re = mag * jnp.cos(a_im * dt)
    ab_im = mag * jnp.sin(a_im * dt)
    den = a_re * a_re + a_im * a_im
    nr = ab_re - 1.0
    k_re = (nr * a_re + ab_im * a_im) / den
    k_im = (ab_im * a_re - nr * a_im) / den
    abre_ref[...] = ab_re
    abim_ref[...] = ab_im
    b_re = bre_ref[...]
    b_im = bim_ref[...]
    bbre_ref[...] = k_re[:, None, :] * b_re - k_im[:, None, :] * b_im
    bbim_ref[...] = k_re[:, None, :] * b_im + k_im[:, None, :] * b_re


def _discretize(a_re, a_im, log_dt, b_re, b_im):
    g, n = a_re.shape
    p = b_re.shape[-1]
    gn = jax.ShapeDtypeStruct((g, n), F32)
    gpn = jax.ShapeDtypeStruct((g, p, n), F32)
    return pl.pallas_call(
        _discretize_kernel,
        out_shape=(gn, gn, gpn, gpn),
        name="s5_discretize",
    )(a_re, a_im, log_dt.reshape(g, 1), b_re.transpose(0, 2, 1), b_im.transpose(0, 2, 1))


def _block_diag(x):
    ncb, gl, r, c = x.shape
    eye = jnp.eye(gl, dtype=x.dtype)
    out = x[:, :, :, None, :] * eye[None, :, None, :, None]
    return out.reshape(ncb, gl * r, gl * c)


def _s5_kernel(u_ref, up_ref, h0_ref, a_ref, bblk_ref, cblk_ref, d_ref, y_ref, ht_ref, *scratch,
               t_steps, n_chunks):
    ring, h_ref = scratch[:S5_RING], scratch[S5_RING]
    t_idx = pl.program_id(1)

    @pl.when(t_idx == 0)
    def _():
        h_ref[...] = h0_ref[0]
        for buf in ring:
            buf[...] = jnp.zeros(buf.shape, F32)

    rows = t_steps * SUBLANES

    def stages(in_buf, scan_buf, out_buf):
        u = u_ref[...].reshape(rows, SSM_CH_BLOCK)
        in_buf[...] = jnp.dot(u.astype(BF16), bblk_ref[0], preferred_element_type=F32)

        u_out = up_ref[...].reshape(rows, SSM_CH_BLOCK)
        y = jnp.dot(out_buf[...].astype(BF16), cblk_ref[0], preferred_element_type=F32)
        y = y + d_ref[0] * u_out
        y_ref[...] = jax.nn.gelu(y).reshape(t_steps, SUBLANES, SSM_CH_BLOCK)

        a_re = a_ref[0, :, :SSM_LANES]
        a_im = a_ref[0, :, SSM_LANES:]
        h_re = h_ref[:, :SSM_LANES]
        h_im = h_ref[:, SSM_LANES:]
        s_re, s_im = h_re, h_im
        for t in range(t_steps):
            row = slice(t * SUBLANES, (t + 1) * SUBLANES)
            b_re = scan_buf[row, :SSM_LANES]
            b_im = scan_buf[row, SSM_LANES:]
            s_re, s_im = a_re * s_re - a_im * s_im + b_re, a_re * s_im + a_im * s_re + b_im
            scan_buf[row, :SSM_LANES] = s_re
            scan_buf[row, SSM_LANES:] = s_im
        in_range = (t_idx >= 1) & (t_idx <= n_chunks)
        h_ref[:, :SSM_LANES] = jnp.where(in_range, s_re, h_re)
        h_ref[:, SSM_LANES:] = jnp.where(in_range, s_im, h_im)
        ht_ref[0] = h_ref[...]

    for k in range(S5_RING):
        pl.when(t_idx % S5_RING == k)(functools.partial(
            stages, ring[k], ring[(k + S5_RING - 1) % S5_RING], ring[(k + S5_RING - 2) % S5_RING]))


def _s5_branch(u_tm, h0_re, h0_im, ab_re, ab_im, bbt_re, bbt_im, c_re, c_im, d_skip):
    s, b, w = u_tm.shape
    assert b == SUBLANES, "the scan keeps one stream per sublane"
    g, n = ab_re.shape
    assert n == SSM_STATE and w == g * SSM_GROUP and w % SSM_CH_BLOCK == 0
    ncb = w // SSM_CH_BLOCK
    gl = SSM_GROUPS_PER_BLOCK
    t_steps = _row_tile(s, S5_CHUNK)

    def lanes(x):
        r = x.shape[0]
        return x.reshape(r, ncb, gl * n).transpose(1, 0, 2)

    h0 = jnp.concatenate([lanes(h0_re), lanes(h0_im)], axis=-1)
    a_tab = jnp.concatenate([lanes(jnp.broadcast_to(ab_re[None], (SUBLANES, g, n))),
                             lanes(jnp.broadcast_to(ab_im[None], (SUBLANES, g, n)))], axis=-1)
    bblk = jnp.concatenate([_block_diag(bbt_re.reshape(ncb, gl, SSM_GROUP, n)),
                            _block_diag(bbt_im.reshape(ncb, gl, SSM_GROUP, n))], axis=-1).astype(BF16)
    cblk = jnp.concatenate([_block_diag(c_re.reshape(ncb, gl, SSM_GROUP, n).transpose(0, 1, 3, 2)),
                            _block_diag(-c_im.reshape(ncb, gl, SSM_GROUP, n).transpose(0, 1, 3, 2))],
                           axis=1).astype(BF16)
    d_tab = d_skip.reshape(ncb, 1, SSM_CH_BLOCK)
    st_spec = pl.BlockSpec((1, SUBLANES, 2 * SSM_LANES), lambda c, t: (c, 0, 0))
    n_chunks = s // t_steps
    chunk = (t_steps, SUBLANES, SSM_CH_BLOCK)
    last = n_chunks - 1
    y_tm, ht = pl.pallas_call(
        functools.partial(_s5_kernel, t_steps=t_steps, n_chunks=n_chunks),
        grid=(ncb, n_chunks + S5_RING - 1),
        in_specs=[pl.BlockSpec(chunk, lambda c, t: (jnp.minimum(t, last), 0, c)),
                  pl.BlockSpec(chunk, lambda c, t: (jnp.clip(t - 2, 0, last), 0, c)),
                  st_spec, st_spec,
                  pl.BlockSpec((1, SSM_CH_BLOCK, 2 * SSM_LANES), lambda c, t: (c, 0, 0)),
                  pl.BlockSpec((1, 2 * SSM_LANES, SSM_CH_BLOCK), lambda c, t: (c, 0, 0)),
                  pl.BlockSpec((1, 1, SSM_CH_BLOCK), lambda c, t: (c, 0, 0))],
        out_specs=[pl.BlockSpec(chunk, lambda c, t: (jnp.clip(t - 2, 0, last), 0, c)), st_spec],
        out_shape=(jax.ShapeDtypeStruct((s, b, w), F32),
                   jax.ShapeDtypeStruct((ncb, SUBLANES, 2 * SSM_LANES), F32)),
        scratch_shapes=[pltpu.VMEM((t_steps * SUBLANES, 2 * SSM_LANES), F32)] * S5_RING
                       + [pltpu.VMEM((SUBLANES, 2 * SSM_LANES), F32)],
        compiler_params=_params("parallel", "arbitrary"),
        name="s5_scan",
    )(u_tm, u_tm, h0, a_tab, bblk, cblk, d_tab)

    def unlanes(x):
        return x.transpose(1, 0, 2).reshape(b, g, n)

    return y_tm, unlanes(ht[..., :SSM_LANES]), unlanes(ht[..., SSM_LANES:])


def _glu_kernel(y_ref, w_ref, b_ref, z_ref, o_ref):
    y = y_ref[...]
    gate = jnp.dot(y.astype(BF16), w_ref[...], preferred_element_type=F32) + b_ref[...]
    o_ref[...] = (y * jax.nn.sigmoid(gate) * jax.nn.silu(z_ref[...])).astype(o_ref.dtype)


def _glu_gate(y_tm, w_glu, l, b_glu, proj, z_col0, bsz, s):
    w = w_glu.shape[1]
    tm = _row_tile(s, ROW_TILE)
    nt = s // tm
    zblk = z_col0 // w
    return pl.pallas_call(
        _glu_kernel,
        grid=(bsz, nt),
        in_specs=[pl.BlockSpec((tm, w), lambda b, t: (t, b)),
                  pl.BlockSpec((None, w, w), lambda b, t: (l, 0, 0)),
                  pl.BlockSpec((1, w), lambda b, t: (0, 0)),
                  pl.BlockSpec((tm, w), lambda b, t: (b * nt + t, zblk))],
        out_specs=pl.BlockSpec((tm, w), lambda b, t: (b * nt + t, 0)),
        out_shape=jax.ShapeDtypeStruct((bsz * s, w), BF16),
        compiler_params=_params("parallel", "arbitrary"),
        name="glu_gate",
    )(y_tm.reshape(s, bsz * w), w_glu, b_glu.reshape(1, w), proj)


def _head_rms(x, g):
    r = lax.rsqrt(jnp.mean(x * x, axis=-1, keepdims=True) + RMS_EPS)
    return x * r * g


def _qk(q, k):
    return lax.dot_general(q, k, (((1,), (1,)), ((), ())), preferred_element_type=F32)


def _band_prompt_kernel(q_ref, kp_ref, kc_ref, vp_ref, vc_ref, z_ref, bias_ref, qg_ref, kg_ref,
                        o_ref, kn_ref):
    i = pl.program_id(2)
    scale = 1.0 / math.sqrt(HEAD_DIM)
    qn = _head_rms(q_ref[0], qg_ref[...]).astype(BF16)
    kpn = _head_rms(kp_ref[0], kg_ref[...])
    kcn = _head_rms(kc_ref[0], kg_ref[...])
    tq = q_ref.shape[1]
    kn_ref[0] = kcn[tq - BAND_LEN:]
    k_all = jnp.concatenate([kpn.astype(BF16), kcn.astype(BF16)], axis=0)
    v_all = jnp.concatenate([vp_ref[0].astype(BF16), vc_ref[0].astype(BF16)], axis=0)
    bias = bias_ref[0]
    col = lax.broadcasted_iota(jnp.int32, (BAND_Q_SUB, BAND_WINDOW), 1)
    z = z_ref[0]
    n_sub = tq // BAND_Q_SUB
    rows = [slice(p * BAND_Q_SUB, (p + 1) * BAND_Q_SUB) for p in range(n_sub)]
    wins = [slice(p * BAND_Q_SUB, p * BAND_Q_SUB + BAND_WINDOW) for p in range(n_sub)]
    scores = []
    for p in range(n_sub):
        s = _qk(qn[rows[p]], k_all[wins[p]]) * scale + bias
        if p * BAND_Q_SUB < BAND_LEN:
            n_missing = jnp.where(i == 0, BAND_LEN - p * BAND_Q_SUB, 0)
            s = jnp.where(col < n_missing, NEG_INF, s)
        scores.append(s)
    probs = [jnp.exp(s - jnp.max(s, axis=-1, keepdims=True)) for s in scores]
    outs = [jnp.dot(pr.astype(BF16), v_all[wins[p]], preferred_element_type=F32) for p, pr in enumerate(probs)]
    for p in range(n_sub):
        l = jnp.sum(probs[p], axis=-1, keepdims=True)
        o_ref[0, rows[p], :] = ((outs[p] / l) * jax.nn.silu(z[rows[p]])).astype(o_ref.dtype)


def _static_take(tab, idx):
    nh = tab.shape[0]
    idx = np.asarray(idx)
    pieces = []
    i = 0
    while i < len(idx):
        step = int(idx[i + 1] - idx[i]) if i + 1 < len(idx) else 0
        if step not in (-1, 0, 1):
            step = 0
            j = i + 1
        else:
            j = i + 1
            while j < len(idx) and idx[j] - idx[j - 1] == step:
                j += 1
        n = j - i
        a = int(idx[i])
        if step == 0:
            pieces.append(jnp.broadcast_to(tab[:, a:a + 1], (nh, n)))
        elif step == 1:
            pieces.append(tab[:, a:a + n])
        else:
            pieces.append(tab[:, a - n + 1:a + 1][:, ::-1])
        i = j
    return jnp.concatenate(pieces, axis=1)


def _toeplitz_bias(rel_bias, rows, cols, offset):
    nh = rel_bias.shape[0]
    period = rows + cols
    m = np.arange(period)
    c_minus_r = np.where(m < cols, m, m - period)
    rel = np.clip(offset - c_minus_r, -MAX_REL, MAX_REL) + MAX_REL
    vec = _static_take(rel_bias.astype(F32), rel)
    return jnp.tile(vec, (1, rows))[:, :rows * (period - 1)].reshape(nh, rows, period - 1)[:, :, :cols]


def _band_bias_prompt(rel_bias):
    r = np.arange(BAND_Q_SUB)
    c = np.arange(BAND_WINDOW)
    jb = c[None, :] - (r // CHUNK * CHUNK)[:, None]
    inside = (jb >= 0) & (jb < (BAND_CHUNKS + 1) * CHUNK)
    toep = _toeplitz_bias(rel_bias, BAND_Q_SUB, BAND_WINDOW, BAND_LEN)
    return jnp.where(jnp.asarray(inside)[None], toep, NEG_INF)


def _band_prompt(proj3, q_col0, k_col0, v_col0, z_col0, rel_bias, qn_g, kn_g):
    bsz, s, _ = proj3.shape
    nh = rel_bias.shape[0]
    tq = _row_tile(s, BAND_Q_BLOCK, BAND_LEN)
    assert s % tq == 0 and tq % BAND_LEN == 0
    nq = s // tq
    per = tq // BAND_LEN
    qb, kb, vb, zb = (c // HEAD_DIM for c in (q_col0, k_col0, v_col0, z_col0))
    bias = _band_bias_prompt(rel_bias)
    blk = (1, tq, HEAD_DIM)
    prev = (1, BAND_LEN, HEAD_DIM)
    gspec = pl.BlockSpec((1, HEAD_DIM), lambda h, b, i: (0, 0))
    return pl.pallas_call(
        _band_prompt_kernel,
        grid=(nh, bsz, nq),
        in_specs=[pl.BlockSpec(blk, lambda h, b, i: (b, i, qb + h)),
                  pl.BlockSpec(prev, lambda h, b, i: (b, jnp.maximum(i * per - 1, 0), kb + h)),
                  pl.BlockSpec(blk, lambda h, b, i: (b, i, kb + h)),
                  pl.BlockSpec(prev, lambda h, b, i: (b, jnp.maximum(i * per - 1, 0), vb + h)),
                  pl.BlockSpec(blk, lambda h, b, i: (b, i, vb + h)),
                  pl.BlockSpec(blk, lambda h, b, i: (b, i, zb + h)),
                  pl.BlockSpec((1, BAND_Q_SUB, BAND_WINDOW), lambda h, b, i: (h, 0, 0)),
                  gspec, gspec],
        out_specs=[pl.BlockSpec(blk, lambda h, b, i: (b, i, h)),
                   pl.BlockSpec(prev, lambda h, b, i: (b, 0, h))],
        out_shape=(jax.ShapeDtypeStruct((bsz, s, nh * HEAD_DIM), BF16),
                   jax.ShapeDtypeStruct((bsz, BAND_LEN, nh * HEAD_DIM), F32)),
        compiler_params=_params("parallel", "parallel", "arbitrary"),
        name="band_prompt",
    )(proj3, proj3, proj3, proj3, proj3, proj3, bias, qn_g.reshape(1, HEAD_DIM), kn_g.reshape(1, HEAD_DIM))


def _band_sample_kernel(q_ref, k_ref, v_ref, ck_ref, cv_ref, z_ref, bias_ref, qg_ref, kg_ref,
                        o_ref, kn_ref, *, lc):
    scale = 1.0 / math.sqrt(HEAD_DIM)
    n = q_ref.shape[1]
    nh = ck_ref.shape[1] // lc
    zpad = jnp.zeros((LANES - n, HEAD_DIM), F32)
    for hh in range(nh):
        lanes = slice(hh * HEAD_DIM, (hh + 1) * HEAD_DIM)
        cached = pl.ds(hh, lc, stride=nh)
        qn = _head_rms(q_ref[0, :, lanes], qg_ref[...]).astype(BF16)
        kn = _head_rms(k_ref[0, :, lanes], kg_ref[...])
        kn_ref[0, :, lanes] = kn
        kn = jnp.concatenate([kn, zpad], axis=0)
        v_new = jnp.concatenate([v_ref[0, :, lanes], zpad], axis=0)
        s_p = _qk(qn, ck_ref[0, cached, :].astype(BF16)) * scale + bias_ref[hh, :, :lc]
        s_c = _qk(qn, kn.astype(BF16)) * scale + bias_ref[hh, :, lc:]
        m = jnp.maximum(jnp.max(s_p, axis=-1, keepdims=True), jnp.max(s_c, axis=-1, keepdims=True))
        p_p = jnp.exp(s_p - m)
        p_c = jnp.exp(s_c - m)
        l = jnp.sum(p_p, axis=-1, keepdims=True) + jnp.sum(p_c, axis=-1, keepdims=True)
        o = jnp.dot(p_p.astype(BF16), cv_ref[0, cached, :].astype(BF16), preferred_element_type=F32)
        o = o + jnp.dot(p_c.astype(BF16), v_new.astype(BF16), preferred_element_type=F32)
        o_ref[0, :, lanes] = ((o / l) * jax.nn.silu(z_ref[0, :, lanes])).astype(o_ref.dtype)


def _band_sample(proj3, q_col0, k_col0, v_col0, z_col0, cache_k, cache_v, l, rel_bias, qn_g, kn_g):
    bsz, n, _ = proj3.shape
    lc = cache_k.shape[2]
    nh = rel_bias.shape[0]
    rows = lambda t: t.reshape(*t.shape[:2], lc * nh, HEAD_DIM)
    cache_k, cache_v = rows(cache_k), rows(cache_v)
    w = nh * HEAD_DIM
    assert all(c % w == 0 for c in (q_col0, k_col0, v_col0, z_col0))
    qb, kb, vb, zb = (c // w for c in (q_col0, k_col0, v_col0, z_col0))
    bias = _toeplitz_bias(rel_bias, n, lc + n, lc)
    assert n <= LANES
    bias = jnp.pad(bias, ((0, 0), (0, 0), (0, LANES - n)), constant_values=NEG_INF)
    blk = (1, n, w)
    cblk = (None, 1, lc * nh, HEAD_DIM)
    gspec = pl.BlockSpec((1, HEAD_DIM), lambda b: (0, 0))
    return pl.pallas_call(
        functools.partial(_band_sample_kernel, lc=lc),
        grid=(bsz,),
        in_specs=[pl.BlockSpec(blk, lambda b: (b, 0, qb)),
                  pl.BlockSpec(blk, lambda b: (b, 0, kb)),
                  pl.BlockSpec(blk, lambda b: (b, 0, vb)),
                  pl.BlockSpec(cblk, lambda b: (l, b, 0, 0)),
                  pl.BlockSpec(cblk, lambda b: (l, b, 0, 0)),
                  pl.BlockSpec(blk, lambda b: (b, 0, zb)),
                  pl.BlockSpec((nh, n, lc + LANES), lambda b: (0, 0, 0)),
                  gspec, gspec],
        out_specs=[pl.BlockSpec(blk, lambda b: (b, 0, 0)),
                   pl.BlockSpec(blk, lambda b: (b, 0, 0))],
        out_shape=(jax.ShapeDtypeStruct((bsz, n, w), BF16),
                   jax.ShapeDtypeStruct((bsz, n, w), F32)),
        compiler_params=_params("parallel"),
        name="band_sample",
    )(proj3, proj3, proj3, cache_k, cache_v, proj3, bias, qn_g.reshape(1, HEAD_DIM), kn_g.reshape(1, HEAD_DIM))


def _suffix_matrix():
    kb = SB_K_BLOCK
    j = lax.broadcasted_iota(jnp.int32, (2 * kb, 2 * kb), 0) % kb
    c = lax.broadcasted_iota(jnp.int32, (2 * kb, 2 * kb), 1)
    return jnp.where((c >= kb) | (j >= c), -1.0, 0.0).astype(BF16)


def _sb_groups(q, groups, carry, acc, umat):
    scale = 1.0 / math.sqrt(HEAD_DIM)
    kb = SB_K_BLOCK
    zs = [_qk(q, k.astype(BF16)) * scale for k, _, _ in groups]
    fails = [jnp.maximum(z, 0.0) + jnp.log(1.0 + jnp.exp(-jnp.abs(z))) for z in zs]
    fails = [f if causal is None else jnp.where(causal, f, 0.0) for f, (_, _, causal) in zip(fails, groups)]
    his = [f.astype(BF16) for f in fails]
    los = [(f - hi.astype(F32)).astype(BF16) for f, hi in zip(fails, his)]
    staged = []
    for z, hi, lo, (_, _, causal) in zip(zs, his, los, groups):
        parts = []
        for b in reversed(range(SB_GROUP // kb)):
            cols = slice(b * kb, (b + 1) * kb)
            sums = jnp.dot(jnp.concatenate([hi[:, cols], lo[:, cols]], axis=1), umat, preferred_element_type=F32)
            parts.append((b, z[:, cols] + sums[:, :kb], sums[:, kb:]))
        staged.append((parts, causal))
    ws = []
    for parts, causal in staged:
        log_w = [None] * len(parts)
        for b, part, total in parts:
            log_w[b] = part + carry
            carry = carry + total
        w = jnp.exp(jnp.concatenate(log_w, axis=1))
        if causal is not None:
            w = jnp.where(causal, w, 0.0)
        ws.append(w.astype(BF16))
    vs = [v.astype(BF16) for _, v, _ in groups]
    w_all = ws[0] if len(ws) == 1 else jnp.concatenate(ws, axis=1)
    v_all = vs[0] if len(vs) == 1 else jnp.concatenate(vs, axis=0)
    return carry, acc + jnp.dot(w_all, v_all, preferred_element_type=F32)


def _sb_prompt_kernel(q_ref, k_ref, v_ref, z_ref, o_ref):
    qi = pl.program_id(2)
    tq = q_ref.shape[1]
    q = q_ref[0].astype(BF16)
    umat = _suffix_matrix()
    q0 = qi * tq
    carry = jnp.zeros((tq, SB_K_BLOCK), F32)
    acc = jnp.zeros((tq, HEAD_DIM), F32)

    def load(k0, causal):
        k0 = pl.multiple_of(k0, SB_GROUP)
        return k_ref[0, pl.ds(k0, SB_GROUP), :], v_ref[0, pl.ds(k0, SB_GROUP), :], causal

    n_diag = tq // SB_GROUP
    assert n_diag % SB_PROMPT_UNROLL == 0
    for d in reversed(range(n_diag)):
        r0 = d * SB_GROUP
        k, v, _ = load(q0 + r0, None)
        causal = (lax.broadcasted_iota(jnp.int32, (tq - r0, SB_GROUP), 1)
                  < lax.broadcasted_iota(jnp.int32, (tq - r0, SB_GROUP), 0))
        c_lo, a_lo = _sb_groups(q[r0:], [(k, v, causal)], carry[r0:], acc[r0:], umat)
        carry = c_lo if d == 0 else jnp.concatenate([carry[:r0], c_lo], axis=0)
        acc = a_lo if d == 0 else jnp.concatenate([acc[:r0], a_lo], axis=0)

    def body(j, ca):
        groups = [load(q0 - (j * SB_PROMPT_UNROLL + u + 1) * SB_GROUP, None) for u in range(SB_PROMPT_UNROLL)]
        return _sb_groups(q, groups, ca[0], ca[1], umat)

    carry, acc = lax.fori_loop(0, qi * (n_diag // SB_PROMPT_UNROLL), body, (carry, acc))
    o_ref[0] = (acc * jax.nn.silu(z_ref[0])).astype(o_ref.dtype)


def _sb_prompt(proj3, q_col0, k_col0, v_col0, z_col0, nh):
    bsz, s, _ = proj3.shape
    tq = _row_tile(s, SB_Q_BLOCK, SB_GROUP * SB_PROMPT_UNROLL)
    qb, kb, vb, zb = (c // HEAD_DIM for c in (q_col0, k_col0, v_col0, z_col0))
    blk = (1, tq, HEAD_DIM)
    full = (1, s, HEAD_DIM)
    return pl.pallas_call(
        _sb_prompt_kernel,
        grid=(bsz, nh, s // tq),
        in_specs=[pl.BlockSpec(blk, lambda b, h, i: (b, i, qb + h)),
                  pl.BlockSpec(full, lambda b, h, i: (b, 0, kb + h)),
                  pl.BlockSpec(full, lambda b, h, i: (b, 0, vb + h)),
                  pl.BlockSpec(blk, lambda b, h, i: (b, i, zb + h))],
        out_specs=pl.BlockSpec(blk, lambda b, h, i: (b, i, h)),
        out_shape=jax.ShapeDtypeStruct((bsz, s, nh * HEAD_DIM), BF16),
        compiler_params=_params("parallel", "parallel", "arbitrary"),
        name="sb_prompt",
    )(proj3, proj3, proj3, proj3)


def _sb_sample_kernel(q_ref, k_ref, v_ref, ck_ref, cv_ref, z_ref, o_ref, *, past_len):
    n = q_ref.shape[1]
    nh = ck_ref.shape[1] // past_len
    umat = _suffix_matrix()
    row = lax.broadcasted_iota(jnp.int32, (n, SB_GROUP), 0)
    col = lax.broadcasted_iota(jnp.int32, (n, SB_GROUP), 1)
    for hh in range(nh):
        lanes = slice(hh * HEAD_DIM, (hh + 1) * HEAD_DIM)
        groups = [(k_ref[0, :, lanes], v_ref[0, :, lanes], col < row)]
        for g in range(past_len // SB_GROUP):
            rows = pl.ds((past_len - (g + 1) * SB_GROUP) * nh + hh, SB_GROUP, stride=nh)
            groups.append((ck_ref[0, rows, :], cv_ref[0, rows, :], None))
        _, acc = _sb_groups(q_ref[0, :, lanes].astype(BF16), groups, jnp.zeros((n, SB_K_BLOCK), F32),
                            jnp.zeros((n, HEAD_DIM), F32), umat)
        o_ref[0, :, lanes] = (acc * jax.nn.silu(z_ref[0, :, lanes])).astype(o_ref.dtype)


def _sb_sample(proj3, q_col0, z_col0, k_new, v_new, cache_k, cache_v, l, nh):
    bsz, n, _ = proj3.shape
    past_len = cache_k.shape[2]
    assert past_len % SB_GROUP == 0 and n <= SB_GROUP
    w = nh * HEAD_DIM
    assert q_col0 % w == 0 and z_col0 % w == 0
    qb, zb = q_col0 // w, z_col0 // w
    pad = ((0, 0), (0, SB_GROUP - n), (0, 0))
    k_pad = jnp.pad(k_new, pad)
    v_pad = jnp.pad(v_new, pad)
    blk = (1, n, w)
    own = (1, SB_GROUP, w)
    past = (None, 1, past_len * nh, HEAD_DIM)
    rows = lambda t: t.reshape(*t.shape[:2], past_len * nh, HEAD_DIM)
    cache_k, cache_v = rows(cache_k), rows(cache_v)
    return pl.pallas_call(
        functools.partial(_sb_sample_kernel, past_len=past_len),
        grid=(bsz,),
        in_specs=[pl.BlockSpec(blk, lambda b: (b, 0, qb)),
                  pl.BlockSpec(own, lambda b: (b, 0, 0)),
                  pl.BlockSpec(own, lambda b: (b, 0, 0)),
                  pl.BlockSpec(past, lambda b: (l, b, 0, 0)),
                  pl.BlockSpec(past, lambda b: (l, b, 0, 0)),
                  pl.BlockSpec(blk, lambda b: (b, 0, zb))],
        out_specs=pl.BlockSpec(blk, lambda b: (b, 0, 0)),
        out_shape=jax.ShapeDtypeStruct((bsz, n, w), BF16),
        compiler_params=_params("parallel"),
        name="sb_sample",
    )(proj3, k_pad, v_pad, cache_k, cache_v, proj3)


def _layer(x, l, w, cache):
    (norm_g, w_in, ssm, w_glu, b_glu, qn_g, kn_g, rel_bias, w_br_a, w_br_b, w_br_c, gate_b, w_out) = w
    ab_re, ab_im, bbt_re, bbt_im, c_re, c_im, d_skip = ssm
    bsz, s, d = x.shape
    mw = w_glu.shape[1]
    nh = mw // HEAD_DIM
    g = ab_re.shape[0]
    m = bsz * s
    x2 = x.reshape(m, d)
    hx = _rmsnorm(x2, norm_g)
    proj = _matmul(hx, w_in, l)
    n_in = proj.shape[1]
    proj3 = proj.reshape(bsz, s, n_in)
    col = lambda i: i * mw
    u_tm = proj3[:, :, :mw].transpose(1, 0, 2)
    if cache is None:
        h0_re = jnp.zeros((bsz, g, SSM_STATE), F32)
        h0_im = h0_re
    else:
        h0_re, h0_im = cache[4], cache[5]
    y_tm, ht_re, ht_im = _s5_branch(u_tm, h0_re, h0_im, ab_re, ab_im, bbt_re, bbt_im, c_re, c_im, d_skip)
    lhs_a = _glu_gate(y_tm, w_glu, l, b_glu, proj, col(1), bsz, s)
    k_c = proj3[:, :, col(7):col(8)]
    v_c = proj3[:, :, col(8):col(9)]
    v_b = proj3[:, :, col(4):col(5)]
    if cache is None:
        lhs_b, band_k = _band_prompt(proj3, col(2), col(3), col(4), col(5), rel_bias, qn_g, kn_g)
        keep = min(BAND_LEN, s)
        band_v = v_b[:, s - keep:]
        lhs_c = _sb_prompt(proj3, col(6), col(7), col(8), col(9), nh)
    else:
        sb_k, sb_v, bk, bv = cache[:4]
        lhs_b, band_k = _band_sample(proj3, col(2), col(3), col(4), col(5), bk, bv, l, rel_bias, qn_g, kn_g)
        band_v = v_b
        lhs_c = _sb_sample(proj3, col(6), col(9), k_c, v_c, sb_k, sb_v, l, nh)
    mixed = _branch_merge(lhs_a, lhs_b.reshape(m, mw), lhs_c.reshape(m, mw), w_br_a, w_br_b, w_br_c, l,
                          proj, gate_b, col(10))
    y = _matmul_residual(mixed, w_out, l, x2).reshape(bsz, s, d)
    heads = lambda t: t.reshape(bsz, t.shape[1], nh, HEAD_DIM)
    return y, (heads(k_c), heads(v_c), heads(band_k), heads(band_v), ht_re, ht_im)


def kernel(x_prompt, x_sample, cache_sb_k, cache_sb_v, cache_band_k, cache_band_v, state_ssm_re, state_ssm_im, norm_g, w_in, ssm_a_re, ssm_a_im, ssm_log_dt, ssm_b_re, ssm_b_im, ssm_c_re, ssm_c_im, ssm_d, w_glu, b_glu, q_norm_g, k_norm_g, rel_bias, w_br_a, w_br_b, w_br_c, gate_b, w_out):
    depth = w_in.shape[0]
    yp, ys = x_prompt, x_sample
    p_states, s_states = [], []
    w_in, w_glu, w_br_a, w_br_b, w_br_c, w_out = (t.astype(BF16) for t in (w_in, w_glu, w_br_a, w_br_b, w_br_c, w_out))
    caches = (cache_sb_k, cache_sb_v, cache_band_k, cache_band_v)
    for l in range(depth):
        ssm = _discretize(ssm_a_re[l], ssm_a_im[l], ssm_log_dt[l], ssm_b_re[l], ssm_b_im[l])
        w = (norm_g[l], w_in, (*ssm, ssm_c_re[l], ssm_c_im[l], ssm_d[l]), w_glu, b_glu[l], q_norm_g[l], k_norm_g[l],
             rel_bias[l], w_br_a, w_br_b, w_br_c, gate_b[l], w_out)
        yp, sp = _layer(yp, l, w, None)
        ys, ss = _layer(ys, l, w, (*caches, state_ssm_re[l], state_ssm_im[l]))
        p_states.append(sp)
        s_states.append(ss)
    stk = lambda states, i: jnp.stack([st[i] for st in states], axis=0)
    return (yp, ys,
            *(stk(p_states, i) for i in range(6)),
            *(stk(s_states, i) for i in range(6)))
```

```python
import functools
import math

import jax
import jax.numpy as jnp
import numpy as np
from jax import lax
from jax.experimental import pallas as pl
from jax.experimental.pallas import tpu as pltpu

F32 = jnp.float32
BF16 = jnp.bfloat16

CHUNK = 64
HEAD_DIM = 128
SSM_GROUP = 16
SSM_STATE = 64
BAND_CHUNKS = 8
BAND_LEN = BAND_CHUNKS * CHUNK
MAX_REL = 128
N_BRANCH = 3
RMS_EPS = 1e-6
NEG_INF = -1e30

LANES = 128
SUBLANES = 8
VMEM_LIMIT_BYTES = 56 * 1024 * 1024
MM_TILE = 1024
ROW_TILE = 512
SSM_CH_BLOCK = 128
SSM_GROUPS_PER_BLOCK = SSM_CH_BLOCK // SSM_GROUP
SSM_LANES = SSM_GROUPS_PER_BLOCK * SSM_STATE
S5_RING = 3
S5_CHUNK = 128
SB_Q_BLOCK = 2048
SB_K_BLOCK = 128
SB_GROUP = 2 * SB_K_BLOCK
SB_PROMPT_UNROLL = 4
SB_HEADS_PER_STEP = 2
BAND_Q_BLOCK = 4 * BAND_LEN
BAND_Q_SUB = 2 * CHUNK
BAND_WINDOW = BAND_LEN + BAND_Q_SUB


def _params(*sem):
    return pltpu.CompilerParams(dimension_semantics=sem, vmem_limit_bytes=VMEM_LIMIT_BYTES)


def _row_tile(m, target, align=SUBLANES):
    if m <= target:
        return m
    for t in range(target - target % align, 0, -align):
        if m % t == 0:
            return t
    raise ValueError((m, target, align))


def _rmsnorm_kernel(x_ref, g_ref, o_ref):
    x = x_ref[...]
    r = lax.rsqrt(jnp.mean(x * x, axis=-1, keepdims=True) + RMS_EPS)
    o_ref[...] = (x * r * g_ref[...]).astype(o_ref.dtype)


def _rmsnorm(x, g):
    m, d = x.shape
    tm = _row_tile(m, ROW_TILE)
    return pl.pallas_call(
        _rmsnorm_kernel,
        grid=(m // tm,),
        in_specs=[pl.BlockSpec((tm, d), lambda i: (i, 0)),
                  pl.BlockSpec((1, d), lambda i: (0, 0))],
        out_specs=pl.BlockSpec((tm, d), lambda i: (i, 0)),
        out_shape=jax.ShapeDtypeStruct((m, d), BF16),
        compiler_params=_params("parallel"),
        name="rmsnorm",
    )(x, g.reshape(1, d))


def _mm_kernel(x_ref, w_ref, o_ref):
    o_ref[...] = jnp.dot(x_ref[...], w_ref[...], preferred_element_type=F32).astype(o_ref.dtype)


def _matmul(x, w, l, out_dtype=F32):
    m, k = x.shape
    _, _, n = w.shape
    tm = _row_tile(m, MM_TILE)
    tn = _row_tile(n, MM_TILE, LANES)
    return pl.pallas_call(
        _mm_kernel,
        grid=(m // tm, n // tn),
        in_specs=[pl.BlockSpec((tm, k), lambda i, j: (i, 0)),
                  pl.BlockSpec((None, k, tn), lambda i, j: (l, 0, j))],
        out_specs=pl.BlockSpec((tm, tn), lambda i, j: (i, j)),
        out_shape=jax.ShapeDtypeStruct((m, n), out_dtype),
        compiler_params=_params("parallel", "arbitrary"),
        name="in_proj",
    )(x, w)


def _mm_res_kernel(x_ref, w_ref, r_ref, o_ref):
    o_ref[...] = r_ref[...] + jnp.dot(x_ref[...], w_ref[...], preferred_element_type=F32)


def _matmul_residual(x, w, l, res):
    m, k = x.shape
    _, _, n = w.shape
    tm = _row_tile(m, MM_TILE)
    tn = _row_tile(n, MM_TILE, LANES)
    return pl.pallas_call(
        _mm_res_kernel,
        grid=(m // tm, n // tn),
        in_specs=[pl.BlockSpec((tm, k), lambda i, j: (i, 0)),
                  pl.BlockSpec((None, k, tn), lambda i, j: (l, 0, j)),
                  pl.BlockSpec((tm, tn), lambda i, j: (i, j))],
        out_specs=pl.BlockSpec((tm, tn), lambda i, j: (i, j)),
        out_shape=jax.ShapeDtypeStruct((m, n), F32),
        compiler_params=_params("parallel", "arbitrary"),
        name="out_proj",
    )(x, w, res)


def _branch_kernel(la_ref, lb_ref, lc_ref, wa_ref, wb_ref, wc_ref,
                   ga_ref, gb_ref, gc_ref, ba_ref, bb_ref, bc_ref, o_ref):
    def one(l_ref, w_ref, g_ref, b_ref):
        o = jnp.dot(l_ref[...], w_ref[...], preferred_element_type=F32)
        return jax.nn.sigmoid(g_ref[...] + b_ref[...]) * o
    mixed = one(la_ref, wa_ref, ga_ref, ba_ref) + one(lb_ref, wb_ref, gb_ref, bb_ref)
    mixed = mixed + one(lc_ref, wc_ref, gc_ref, bc_ref)
    o_ref[...] = mixed.astype(o_ref.dtype)


def _branch_merge(lhs_a, lhs_b, lhs_c, w_a, w_b, w_c, l, proj, gate_b, g_col0):
    m, kw = lhs_a.shape
    d = w_a.shape[2]
    tm = _row_tile(m, ROW_TILE)
    tn = _row_tile(math.gcd(d, g_col0), MM_TILE, LANES)
    assert g_col0 % tn == 0 and d % tn == 0
    gblk = g_col0 // tn
    nblk = d // tn
    lspec = pl.BlockSpec((tm, kw), lambda j, i: (i, 0))
    wspec = pl.BlockSpec((None, kw, tn), lambda j, i: (l, 0, j))
    gspecs = [pl.BlockSpec((tm, tn), functools.partial(lambda j, i, br: (i, gblk + br * nblk + j), br=br))
              for br in range(N_BRANCH)]
    bspecs = [pl.BlockSpec((1, tn), functools.partial(lambda j, i, br: (0, br * nblk + j), br=br))
              for br in range(N_BRANCH)]
    gb = gate_b.reshape(1, N_BRANCH * d)
    return pl.pallas_call(
        _branch_kernel,
        grid=(d // tn, m // tm),
        in_specs=[lspec, lspec, lspec, wspec, wspec, wspec, *gspecs, *bspecs],
        out_specs=pl.BlockSpec((tm, tn), lambda j, i: (i, j)),
        out_shape=jax.ShapeDtypeStruct((m, d), BF16),
        compiler_params=_params("parallel", "arbitrary"),
        name="branch_merge",
    )(lhs_a, lhs_b, lhs_c, w_a, w_b, w_c, proj, proj, proj, gb, gb, gb)


def _discretize_kernel(are_ref, aim_ref, ldt_ref, bre_ref, bim_ref,
                       abre_ref, abim_ref, bbre_ref, bbim_ref):
    a_re = are_ref[...]
    a_im = aim_ref[...]
    dt = jnp.exp(ldt_ref[...])
    mag = jnp.exp(a_re * dt)
    ab_re = mag * jnp.cos(a_im * dt)
    ab_im = mag * jnp.sin(a_im * dt)
    den = a_re * a_re + a_im * a_im
    nr = ab_re - 1.0
    k_re = (nr * a_re + ab_im * a_im) / den
    k_im = (ab_im * a_re - nr * a_im) / den
    abre_ref[...] = ab_re
    abim_ref[...] = ab_im
    b_re = bre_ref[...]
    b_im = bim_ref[...]
    bbre_ref[...] = k_re[:, None, :] * b_re - k_im[:, None, :] * b_im
    bbim_ref[...] = k_re[:, None, :] * b_im + k_im[:, None, :] * b_re


def _discretize(a_re, a_im, log_dt, b_re, b_im):
    g, n = a_re.shape
    p = b_re.shape[-1]
    gn = jax.ShapeDtypeStruct((g, n), F32)
    gpn = jax.ShapeDtypeStruct((g, p, n), F32)
    return pl.pallas_call(
        _discretize_kernel,
        out_shape=(gn, gn, gpn, gpn),
        name="s5_discretize",
    )(a_re, a_im, log_dt.reshape(g, 1), b_re.transpose(0, 2, 1), b_im.transpose(0, 2, 1))


def _block_diag(x):
    ncb, gl, r, c = x.shape
    eye = jnp.eye(gl, dtype=x.dtype)
    out = x[:, :, :, None, :] * eye[None, :, None, :, None]
    return out.reshape(ncb, gl * r, gl * c)


def _s5_kernel(u_ref, up_ref, h0_ref, a_ref, bblk_ref, cblk_ref, d_ref, y_ref, ht_ref, *scratch,
               t_steps, n_chunks):
    ring, h_ref = scratch[:S5_RING], scratch[S5_RING]
    t_idx = pl.program_id(1)

    @pl.when(t_idx == 0)
    def _():
        h_ref[...] = h0_ref[0]
        for buf in ring:
            buf[...] = jnp.zeros(buf.shape, F32)

    rows = t_steps * SUBLANES

    def stages(in_buf, scan_buf, out_buf):
        u = u_ref[...].reshape(rows, SSM_CH_BLOCK)
        in_buf[...] = jnp.dot(u.astype(BF16), bblk_ref[0], preferred_element_type=F32)

        u_out = up_ref[...].reshape(rows, SSM_CH_BLOCK)
        y = jnp.dot(out_buf[...].astype(BF16), cblk_ref[0], preferred_element_type=F32)
        y = y + d_ref[0] * u_out
        y_ref[...] = jax.nn.gelu(y).reshape(t_steps, SUBLANES, SSM_CH_BLOCK)

        a_re = a_ref[0, :, :SSM_LANES]
        a_im = a_ref[0, :, SSM_LANES:]
        h_re = h_ref[:, :SSM_LANES]
        h_im = h_ref[:, SSM_LANES:]
        s_re, s_im = h_re, h_im
        for t in range(t_steps):
            row = slice(t * SUBLANES, (t + 1) * SUBLANES)
            b_re = scan_buf[row, :SSM_LANES]
            b_im = scan_buf[row, SSM_LANES:]
            s_re, s_im = a_re * s_re - a_im * s_im + b_re, a_re * s_im + a_im * s_re + b_im
            scan_buf[row, :SSM_LANES] = s_re
            scan_buf[row, SSM_LANES:] = s_im
        in_range = (t_idx >= 1) & (t_idx <= n_chunks)
        h_ref[:, :SSM_LANES] = jnp.where(in_range, s_re, h_re)
        h_ref[:, SSM_LANES:] = jnp.where(in_range, s_im, h_im)
        ht_ref[0] = h_ref[...]

    for k in range(S5_RING):
        pl.when(t_idx % S5_RING == k)(functools.partial(
            stages, ring[k], ring[(k + S5_RING - 1) % S5_RING], ring[(k + S5_RING - 2) % S5_RING]))


def _s5_branch(u_tm, h0_re, h0_im, ab_re, ab_im, bbt_re, bbt_im, c_re, c_im, d_skip):
    s, b, w = u_tm.shape
    assert b == SUBLANES, "the scan keeps one stream per sublane"
    g, n = ab_re.shape
    assert n == SSM_STATE and w == g * SSM_GROUP and w % SSM_CH_BLOCK == 0
    ncb = w // SSM_CH_BLOCK
    gl = SSM_GROUPS_PER_BLOCK
    t_steps = _row_tile(s, S5_CHUNK)

    def lanes(x):
        r = x.shape[0]
        return x.reshape(r, ncb, gl * n).transpose(1, 0, 2)

    h0 = jnp.concatenate([lanes(h0_re), lanes(h0_im)], axis=-1)
    a_tab = jnp.concatenate([lanes(jnp.broadcast_to(ab_re[None], (SUBLANES, g, n))),
                             lanes(jnp.broadcast_to(ab_im[None], (SUBLANES, g, n)))], axis=-1)
    bblk = jnp.concatenate([_block_diag(bbt_re.reshape(ncb, gl, SSM_GROUP, n)),
                            _block_diag(bbt_im.reshape(ncb, gl, SSM_GROUP, n))], axis=-1).astype(BF16)
    cblk = jnp.concatenate([_block_diag(c_re.reshape(ncb, gl, SSM_GROUP, n).transpose(0, 1, 3, 2)),
                            _block_diag(-c_im.reshape(ncb, gl, SSM_GROUP, n).transpose(0, 1, 3, 2))],
                           axis=1).astype(BF16)
    d_tab = d_skip.reshape(ncb, 1, SSM_CH_BLOCK)
    st_spec = pl.BlockSpec((1, SUBLANES, 2 * SSM_LANES), lambda c, t: (c, 0, 0))
    n_chunks = s // t_steps
    chunk = (t_steps, SUBLANES, SSM_CH_BLOCK)
    last = n_chunks - 1
    y_tm, ht = pl.pallas_call(
        functools.partial(_s5_kernel, t_steps=t_steps, n_chunks=n_chunks),
        grid=(ncb, n_chunks + S5_RING - 1),
        in_specs=[pl.BlockSpec(chunk, lambda c, t: (jnp.minimum(t, last), 0, c)),
                  pl.BlockSpec(chunk, lambda c, t: (jnp.clip(t - 2, 0, last), 0, c)),
                  st_spec, st_spec,
                  pl.BlockSpec((1, SSM_CH_BLOCK, 2 * SSM_LANES), lambda c, t: (c, 0, 0)),
                  pl.BlockSpec((1, 2 * SSM_LANES, SSM_CH_BLOCK), lambda c, t: (c, 0, 0)),
                  pl.BlockSpec((1, 1, SSM_CH_BLOCK), lambda c, t: (c, 0, 0))],
        out_specs=[pl.BlockSpec(chunk, lambda c, t: (jnp.clip(t - 2, 0, last), 0, c)), st_spec],
        out_shape=(jax.ShapeDtypeStruct((s, b, w), F32),
                   jax.ShapeDtypeStruct((ncb, SUBLANES, 2 * SSM_LANES), F32)),
        scratch_shapes=[pltpu.VMEM((t_steps * SUBLANES, 2 * SSM_LANES), F32)] * S5_RING
                       + [pltpu.VMEM((SUBLANES, 2 * SSM_LANES), F32)],
        compiler_params=_params("parallel", "arbitrary"),
        name="s5_scan",
    )(u_tm, u_tm, h0, a_tab, bblk, cblk, d_tab)

    def unlanes(x):
        return x.transpose(1, 0, 2).reshape(b, g, n)

    return y_tm, unlanes(ht[..., :SSM_LANES]), unlanes(ht[..., SSM_LANES:])


def _glu_kernel(y_ref, w_ref, b_ref, z_ref, o_ref):
    y = y_ref[...]
    gate = jnp.dot(y.astype(BF16), w_ref[...], preferred_element_type=F32) + b_ref[...]
    o_ref[...] = (y * jax.nn.sigmoid(gate) * jax.nn.silu(z_ref[...])).astype(o_ref.dtype)


def _glu_gate(y_tm, w_glu, l, b_glu, proj, z_col0, bsz, s):
    w = w_glu.shape[1]
    tm = _row_tile(s, ROW_TILE)
    nt = s // tm
    zblk = z_col0 // w
    return pl.pallas_call(
        _glu_kernel,
        grid=(bsz, nt),
        in_specs=[pl.BlockSpec((tm, w), lambda b, t: (t, b)),
                  pl.BlockSpec((None, w, w), lambda b, t: (l, 0, 0)),
                  pl.BlockSpec((1, w), lambda b, t: (0, 0)),
                  pl.BlockSpec((tm, w), lambda b, t: (b * nt + t, zblk))],
        out_specs=pl.BlockSpec((tm, w), lambda b, t: (b * nt + t, 0)),
        out_shape=jax.ShapeDtypeStruct((bsz * s, w), BF16),
        compiler_params=_params("parallel", "arbitrary"),
        name="glu_gate",
    )(y_tm.reshape(s, bsz * w), w_glu, b_glu.reshape(1, w), proj)


def _head_rms(x, g):
    r = lax.rsqrt(jnp.mean(x * x, axis=-1, keepdims=True) + RMS_EPS)
    return x * r * g


def _qk(q, k):
    return lax.dot_general(q, k, (((1,), (1,)), ((), ())), preferred_element_type=F32)


def _band_prompt_kernel(q_ref, kp_ref, kc_ref, vp_ref, vc_ref, z_ref, bias_ref, qg_ref, kg_ref,
                        o_ref, kn_ref):
    i = pl.program_id(2)
    scale = 1.0 / math.sqrt(HEAD_DIM)
    qn = _head_rms(q_ref[0], qg_ref[...]).astype(BF16)
    kpn = _head_rms(kp_ref[0], kg_ref[...])
    kcn = _head_rms(kc_ref[0], kg_ref[...])
    tq = q_ref.shape[1]
    kn_ref[0] = kcn[tq - BAND_LEN:]
    k_all = jnp.concatenate([kpn.astype(BF16), kcn.astype(BF16)], axis=0)
    v_all = jnp.concatenate([vp_ref[0].astype(BF16), vc_ref[0].astype(BF16)], axis=0)
    bias = bias_ref[0]
    col = lax.broadcasted_iota(jnp.int32, (BAND_Q_SUB, BAND_WINDOW), 1)
    z = z_ref[0]
    n_sub = tq // BAND_Q_SUB
    rows = [slice(p * BAND_Q_SUB, (p + 1) * BAND_Q_SUB) for p in range(n_sub)]
    wins = [slice(p * BAND_Q_SUB, p * BAND_Q_SUB + BAND_WINDOW) for p in range(n_sub)]
    scores = []
    for p in range(n_sub):
        s = _qk(qn[rows[p]], k_all[wins[p]]) * scale + bias
        if p * BAND_Q_SUB < BAND_LEN:
            n_missing = jnp.where(i == 0, BAND_LEN - p * BAND_Q_SUB, 0)
            s = jnp.where(col < n_missing, NEG_INF, s)
        scores.append(s)
    probs = [jnp.exp(s - jnp.max(s, axis=-1, keepdims=True)) for s in scores]
    outs = [jnp.dot(pr.astype(BF16), v_all[wins[p]], preferred_element_type=F32) for p, pr in enumerate(probs)]
    for p in range(n_sub):
        l = jnp.sum(probs[p], axis=-1, keepdims=True)
        o_ref[0, rows[p], :] = ((outs[p] / l) * jax.nn.silu(z[rows[p]])).astype(o_ref.dtype)


def _static_take(tab, idx):
    nh = tab.shape[0]
    idx = np.asarray(idx)
    pieces = []
    i = 0
    while i < len(idx):
        step = int(idx[i + 1] - idx[i]) if i + 1 < len(idx) else 0
        if step not in (-1, 0, 1):
            step = 0
            j = i + 1
        else:
            j = i + 1
            while j < len(idx) and idx[j] - idx[j - 1] == step:
                j += 1
        n = j - i
        a = int(idx[i])
        if step == 0:
            pieces.append(jnp.broadcast_to(tab[:, a:a + 1], (nh, n)))
        elif step == 1:
            pieces.append(tab[:, a:a + n])
        else:
            pieces.append(tab[:, a - n + 1:a + 1][:, ::-1])
        i = j
    return jnp.concatenate(pieces, axis=1)


def _toeplitz_bias(rel_bias, rows, cols, offset):
    nh = rel_bias.shape[0]
    period = rows + cols
    m = np.arange(period)
    c_minus_r = np.where(m < cols, m, m - period)
    rel = np.clip(offset - c_minus_r, -MAX_REL, MAX_REL) + MAX_REL
    vec = _static_take(rel_bias.astype(F32), rel)
    return jnp.tile(vec, (1, rows))[:, :rows * (period - 1)].reshape(nh, rows, period - 1)[:, :, :cols]


def _band_bias_prompt(rel_bias):
    r = np.arange(BAND_Q_SUB)
    c = np.arange(BAND_WINDOW)
    jb = c[None, :] - (r // CHUNK * CHUNK)[:, None]
    inside = (jb >= 0) & (jb < (BAND_CHUNKS + 1) * CHUNK)
    toep = _toeplitz_bias(rel_bias, BAND_Q_SUB, BAND_WINDOW, BAND_LEN)
    return jnp.where(jnp.asarray(inside)[None], toep, NEG_INF)


def _band_prompt(proj3, q_col0, k_col0, v_col0, z_col0, rel_bias, qn_g, kn_g):
    bsz, s, _ = proj3.shape
    nh = rel_bias.shape[0]
    tq = _row_tile(s, BAND_Q_BLOCK, BAND_LEN)
    assert s % tq == 0 and tq % BAND_LEN == 0
    nq = s // tq
    per = tq // BAND_LEN
    qb, kb, vb, zb = (c // HEAD_DIM for c in (q_col0, k_col0, v_col0, z_col0))
    bias = _band_bias_prompt(rel_bias)
    blk = (1, tq, HEAD_DIM)
    prev = (1, BAND_LEN, HEAD_DIM)
    gspec = pl.BlockSpec((1, HEAD_DIM), lambda h, b, i: (0, 0))
    return pl.pallas_call(
        _band_prompt_kernel,
        grid=(nh, bsz, nq),
        in_specs=[pl.BlockSpec(blk, lambda h, b, i: (b, i, qb + h)),
                  pl.BlockSpec(prev, lambda h, b, i: (b, jnp.maximum(i * per - 1, 0), kb + h)),
                  pl.BlockSpec(blk, lambda h, b, i: (b, i, kb + h)),
                  pl.BlockSpec(prev, lambda h, b, i: (b, jnp.maximum(i * per - 1, 0), vb + h)),
                  pl.BlockSpec(blk, lambda h, b, i: (b, i, vb + h)),
                  pl.BlockSpec(blk, lambda h, b, i: (b, i, zb + h)),
                  pl.BlockSpec((1, BAND_Q_SUB, BAND_WINDOW), lambda h, b, i: (h, 0, 0)),
                  gspec, gspec],
        out_specs=[pl.BlockSpec(blk, lambda h, b, i: (b, i, h)),
                   pl.BlockSpec(prev, lambda h, b, i: (b, 0, h))],
        out_shape=(jax.ShapeDtypeStruct((bsz, s, nh * HEAD_DIM), BF16),
                   jax.ShapeDtypeStruct((bsz, BAND_LEN, nh * HEAD_DIM), F32)),
        compiler_params=_params("parallel", "parallel", "arbitrary"),
        name="band_prompt",
    )(proj3, proj3, proj3, proj3, proj3, proj3, bias, qn_g.reshape(1, HEAD_DIM), kn_g.reshape(1, HEAD_DIM))


def _band_sample_kernel(q_ref, k_ref, v_ref, ck_ref, cv_ref, z_ref, bias_ref, qg_ref, kg_ref,
                        o_ref, kn_ref, *, lc):
    scale = 1.0 / math.sqrt(HEAD_DIM)
    n = q_ref.shape[1]
    nh = ck_ref.shape[1] // lc
    zpad = jnp.zeros((LANES - n, HEAD_DIM), F32)
    for hh in range(nh):
        lanes = slice(hh * HEAD_DIM, (hh + 1) * HEAD_DIM)
        cached = pl.ds(hh, lc, stride=nh)
        qn = _head_rms(q_ref[0, :, lanes], qg_ref[...]).astype(BF16)
        kn = _head_rms(k_ref[0, :, lanes], kg_ref[...])
        kn_ref[0, :, lanes] = kn
        kn = jnp.concatenate([kn, zpad], axis=0)
        v_new = jnp.concatenate([v_ref[0, :, lanes], zpad], axis=0)
        s_p = _qk(qn, ck_ref[0, cached, :].astype(BF16)) * scale + bias_ref[hh, :, :lc]
        s_c = _qk(qn, kn.astype(BF16)) * scale + bias_ref[hh, :, lc:]
        m = jnp.maximum(jnp.max(s_p, axis=-1, keepdims=True), jnp.max(s_c, axis=-1, keepdims=True))
        p_p = jnp.exp(s_p - m)
        p_c = jnp.exp(s_c - m)
        l = jnp.sum(p_p, axis=-1, keepdims=True) + jnp.sum(p_c, axis=-1, keepdims=True)
        o = jnp.dot(p_p.astype(BF16), cv_ref[0, cached, :].astype(BF16), preferred_element_type=F32)
        o = o + jnp.dot(p_c.astype(BF16), v_new.astype(BF16), preferred_element_type=F32)
        o_ref[0, :, lanes] = ((o / l) * jax.nn.silu(z_ref[0, :, lanes])).astype(o_ref.dtype)


def _band_sample(proj3, q_col0, k_col0, v_col0, z_col0, cache_k, cache_v, l, rel_bias, qn_g, kn_g):
    bsz, n, _ = proj3.shape
    lc = cache_k.shape[2]
    nh = rel_bias.shape[0]
    rows = lambda t: t.reshape(*t.shape[:2], lc * nh, HEAD_DIM)
    cache_k, cache_v = rows(cache_k), rows(cache_v)
    w = nh * HEAD_DIM
    assert all(c % w == 0 for c in (q_col0, k_col0, v_col0, z_col0))
    qb, kb, vb, zb = (c // w for c in (q_col0, k_col0, v_col0, z_col0))
    bias = _toeplitz_bias(rel_bias, n, lc + n, lc)
    assert n <= LANES
    bias = jnp.pad(bias, ((0, 0), (0, 0), (0, LANES - n)), constant_values=NEG_INF)
    blk = (1, n, w)
    cblk = (None, 1, lc * nh, HEAD_DIM)
    gspec = pl.BlockSpec((1, HEAD_DIM), lambda b: (0, 0))
    return pl.pallas_call(
        functools.partial(_band_sample_kernel, lc=lc),
        grid=(bsz,),
        in_specs=[pl.BlockSpec(blk, lambda b: (b, 0, qb)),
                  pl.BlockSpec(blk, lambda b: (b, 0, kb)),
                  pl.BlockSpec(blk, lambda b: (b, 0, vb)),
                  pl.BlockSpec(cblk, lambda b: (l, b, 0, 0)),
                  pl.BlockSpec(cblk, lambda b: (l, b, 0, 0)),
                  pl.BlockSpec(blk, lambda b: (b, 0, zb)),
                  pl.BlockSpec((nh, n, lc + LANES), lambda b: (0, 0, 0)),
                  gspec, gspec],
        out_specs=[pl.BlockSpec(blk, lambda b: (b, 0, 0)),
                   pl.BlockSpec(blk, lambda b: (b, 0, 0))],
        out_shape=(jax.ShapeDtypeStruct((bsz, n, w), BF16),
                   jax.ShapeDtypeStruct((bsz, n, w), F32)),
        compiler_params=_params("parallel"),
        name="band_sample",
    )(proj3, proj3, proj3, cache_k, cache_v, proj3, bias, qn_g.reshape(1, HEAD_DIM), kn_g.reshape(1, HEAD_DIM))


def _suffix_matrix():
    kb = SB_K_BLOCK
    j = lax.broadcasted_iota(jnp.int32, (2 * kb, 2 * kb), 0) % kb
    c = lax.broadcasted_iota(jnp.int32, (2 * kb, 2 * kb), 1)
    return jnp.where((c >= kb) | (j >= c), -1.0, 0.0).astype(BF16)


def _sb_groups(q, groups, carry, acc, umat):
    scale = 1.0 / math.sqrt(HEAD_DIM)
    kb = SB_K_BLOCK
    zs = [_qk(q, k.astype(BF16)) * scale for k, _, _ in groups]
    fails = [jnp.maximum(z, 0.0) + jnp.log(1.0 + jnp.exp(-jnp.abs(z))) for z in zs]
    fails = [f if causal is None else jnp.where(causal, f, 0.0) for f, (_, _, causal) in zip(fails, groups)]
    his = [f.astype(BF16) for f in fails]
    los = [(f - hi.astype(F32)).astype(BF16) for f, hi in zip(fails, his)]
    staged = []
    for z, hi, lo, (_, _, causal) in zip(zs, his, los, groups):
        parts = []
        for b in reversed(range(SB_GROUP // kb)):
            cols = slice(b * kb, (b + 1) * kb)
            sums = jnp.dot(jnp.concatenate([hi[:, cols], lo[:, cols]], axis=1), umat, preferred_element_type=F32)
            parts.append((b, z[:, cols] + sums[:, :kb], sums[:, kb:]))
        staged.append((parts, causal))
    ws = []
    for parts, causal in staged:
        log_w = [None] * len(parts)
        for b, part, total in parts:
            log_w[b] = part + carry
            carry = carry + total
        w = jnp.exp(jnp.concatenate(log_w, axis=1))
        if causal is not None:
            w = jnp.where(causal, w, 0.0)
        ws.append(w.astype(BF16))
    vs = [v.astype(BF16) for _, v, _ in groups]
    w_all = ws[0] if len(ws) == 1 else jnp.concatenate(ws, axis=1)
    v_all = vs[0] if len(vs) == 1 else jnp.concatenate(vs, axis=0)
    return carry, acc + jnp.dot(w_all, v_all, preferred_element_type=F32)


def _sb_prompt_kernel(q_ref, k_ref, v_ref, z_ref, o_ref):
    for hh in range(q_ref.shape[2] // HEAD_DIM):
        _sb_prompt_head(q_ref, k_ref, v_ref, z_ref, o_ref, slice(hh * HEAD_DIM, (hh + 1) * HEAD_DIM))


def _sb_prompt_head(q_ref, k_ref, v_ref, z_ref, o_ref, lanes):
    qi = pl.program_id(2)
    tq = q_ref.shape[1]
    q = q_ref[0, :, lanes].astype(BF16)
    umat = _suffix_matrix()
    q0 = qi * tq
    carry = jnp.zeros((tq, SB_K_BLOCK), F32)
    acc = jnp.zeros((tq, HEAD_DIM), F32)

    def load(k0, causal):
        k0 = pl.multiple_of(k0, SB_GROUP)
        return k_ref[0, pl.ds(k0, SB_GROUP), lanes], v_ref[0, pl.ds(k0, SB_GROUP), lanes], causal

    n_diag = tq // SB_GROUP
    assert n_diag % SB_PROMPT_UNROLL == 0
    for d in reversed(range(n_diag)):
        r0 = d * SB_GROUP
        k, v, _ = load(q0 + r0, None)
        causal = (lax.broadcasted_iota(jnp.int32, (tq - r0, SB_GROUP), 1)
                  < lax.broadcasted_iota(jnp.int32, (tq - r0, SB_GROUP), 0))
        c_lo, a_lo = _sb_groups(q[r0:], [(k, v, causal)], carry[r0:], acc[r0:], umat)
        carry = c_lo if d == 0 else jnp.concatenate([carry[:r0], c_lo], axis=0)
        acc = a_lo if d == 0 else jnp.concatenate([acc[:r0], a_lo], axis=0)

    def body(j, ca):
        groups = [load(q0 - (j * SB_PROMPT_UNROLL + u + 1) * SB_GROUP, None) for u in range(SB_PROMPT_UNROLL)]
        return _sb_groups(q, groups, ca[0], ca[1], umat)

    carry, acc = lax.fori_loop(0, qi * (n_diag // SB_PROMPT_UNROLL), body, (carry, acc))
    o_ref[0, :, lanes] = (acc * jax.nn.silu(z_ref[0, :, lanes])).astype(o_ref.dtype)


def _sb_prompt(proj3, q_col0, k_col0, v_col0, z_col0, nh):
    bsz, s, _ = proj3.shape
    tq = _row_tile(s, SB_Q_BLOCK, SB_GROUP * SB_PROMPT_UNROLL)
    hp = math.gcd(nh, SB_HEADS_PER_STEP)
    wl = hp * HEAD_DIM
    assert all(c % wl == 0 for c in (q_col0, k_col0, v_col0, z_col0))
    qb, kb, vb, zb = (c // wl for c in (q_col0, k_col0, v_col0, z_col0))
    blk = (1, tq, wl)
    full = (1, s, wl)
    return pl.pallas_call(
        _sb_prompt_kernel,
        grid=(bsz, nh // hp, s // tq),
        in_specs=[pl.BlockSpec(blk, lambda b, h, i: (b, i, qb + h)),
                  pl.BlockSpec(full, lambda b, h, i: (b, 0, kb + h)),
                  pl.BlockSpec(full, lambda b, h, i: (b, 0, vb + h)),
                  pl.BlockSpec(blk, lambda b, h, i: (b, i, zb + h))],
        out_specs=pl.BlockSpec(blk, lambda b, h, i: (b, i, h)),
        out_shape=jax.ShapeDtypeStruct((bsz, s, nh * HEAD_DIM), BF16),
        compiler_params=_params("parallel", "parallel", "arbitrary"),
        name="sb_prompt",
    )(proj3, proj3, proj3, proj3)


def _sb_sample_kernel(q_ref, k_ref, v_ref, ck_ref, cv_ref, z_ref, o_ref, *, past_len):
    n = q_ref.shape[1]
    nh = ck_ref.shape[1] // past_len
    umat = _suffix_matrix()
    row = lax.broadcasted_iota(jnp.int32, (n, SB_GROUP), 0)
    col = lax.broadcasted_iota(jnp.int32, (n, SB_GROUP), 1)
    for hh in range(nh):
        lanes = slice(hh * HEAD_DIM, (hh + 1) * HEAD_DIM)
        groups = [(k_ref[0, :, lanes], v_ref[0, :, lanes], col < row)]
        for g in range(past_len // SB_GROUP):
            rows = pl.ds((past_len - (g + 1) * SB_GROUP) * nh + hh, SB_GROUP, stride=nh)
            groups.append((ck_ref[0, rows, :], cv_ref[0, rows, :], None))
        _, acc = _sb_groups(q_ref[0, :, lanes].astype(BF16), groups, jnp.zeros((n, SB_K_BLOCK), F32),
                            jnp.zeros((n, HEAD_DIM), F32), umat)
        o_ref[0, :, lanes] = (acc * jax.nn.silu(z_ref[0, :, lanes])).astype(o_ref.dtype)


def _sb_sample(proj3, q_col0, z_col0, k_new, v_new, cache_k, cache_v, l, nh):
    bsz, n, _ = proj3.shape
    past_len = cache_k.shape[2]
    assert past_len % SB_GROUP == 0 and n <= SB_GROUP
    w = nh * HEAD_DIM
    assert q_col0 % w == 0 and z_col0 % w == 0
    qb, zb = q_col0 // w, z_col0 // w
    pad = ((0, 0), (0, SB_GROUP - n), (0, 0))
    k_pad = jnp.pad(k_new, pad)
    v_pad = jnp.pad(v_new, pad)
    blk = (1, n, w)
    own = (1, SB_GROUP, w)
    past = (None, 1, past_len * nh, HEAD_DIM)
    rows = lambda t: t.reshape(*t.shape[:2], past_len * nh, HEAD_DIM)
    cache_k, cache_v = rows(cache_k), rows(cache_v)
    return pl.pallas_call(
        functools.partial(_sb_sample_kernel, past_len=past_len),
        grid=(bsz,),
        in_specs=[pl.BlockSpec(blk, lambda b: (b, 0, qb)),
                  pl.BlockSpec(own, lambda b: (b, 0, 0)),
                  pl.BlockSpec(own, lambda b: (b, 0, 0)),
                  pl.BlockSpec(past, lambda b: (l, b, 0, 0)),
                  pl.BlockSpec(past, lambda b: (l, b, 0, 0)),
                  pl.BlockSpec(blk, lambda b: (b, 0, zb))],
        out_specs=pl.BlockSpec(blk, lambda b: (b, 0, 0)),
        out_shape=jax.ShapeDtypeStruct((bsz, n, w), BF16),
        compiler_params=_params("parallel"),
        name="sb_sample",
    )(proj3, k_pad, v_pad, cache_k, cache_v, proj3)


def _layer(x, l, w, cache):
    (norm_g, w_in, ssm, w_glu, b_glu, qn_g, kn_g, rel_bias, w_br_a, w_br_b, w_br_c, gate_b, w_out) = w
    ab_re, ab_im, bbt_re, bbt_im, c_re, c_im, d_skip = ssm
    bsz, s, d = x.shape
    mw = w_glu.shape[1]
    nh = mw // HEAD_DIM
    g = ab_re.shape[0]
    m = bsz * s
    x2 = x.reshape(m, d)
    hx = _rmsnorm(x2, norm_g)
    proj = _matmul(hx, w_in, l)
    n_in = proj.shape[1]
    proj3 = proj.reshape(bsz, s, n_in)
    col = lambda i: i * mw
    u_tm = proj3[:, :, :mw].transpose(1, 0, 2)
    if cache is None:
        h0_re = jnp.zeros((bsz, g, SSM_STATE), F32)
        h0_im = h0_re
    else:
        h0_re, h0_im = cache[4], cache[5]
    y_tm, ht_re, ht_im = _s5_branch(u_tm, h0_re, h0_im, ab_re, ab_im, bbt_re, bbt_im, c_re, c_im, d_skip)
    lhs_a = _glu_gate(y_tm, w_glu, l, b_glu, proj, col(1), bsz, s)
    k_c = proj3[:, :, col(7):col(8)]
    v_c = proj3[:, :, col(8):col(9)]
    v_b = proj3[:, :, col(4):col(5)]
    if cache is None:
        lhs_b, band_k = _band_prompt(proj3, col(2), col(3), col(4), col(5), rel_bias, qn_g, kn_g)
        keep = min(BAND_LEN, s)
        band_v = v_b[:, s - keep:]
        lhs_c = _sb_prompt(proj3, col(6), col(7), col(8), col(9), nh)
    else:
        sb_k, sb_v, bk, bv = cache[:4]
        lhs_b, band_k = _band_sample(proj3, col(2), col(3), col(4), col(5), bk, bv, l, rel_bias, qn_g, kn_g)
        band_v = v_b
        lhs_c = _sb_sample(proj3, col(6), col(9), k_c, v_c, sb_k, sb_v, l, nh)
    mixed = _branch_merge(lhs_a, lhs_b.reshape(m, mw), lhs_c.reshape(m, mw), w_br_a, w_br_b, w_br_c, l,
                          proj, gate_b, col(10))
    y = _matmul_residual(mixed, w_out, l, x2).reshape(bsz, s, d)
    heads = lambda t: t.reshape(bsz, t.shape[1], nh, HEAD_DIM)
    return y, (heads(k_c), heads(v_c), heads(band_k), heads(band_v), ht_re, ht_im)


def kernel(x_prompt, x_sample, cache_sb_k, cache_sb_v, cache_band_k, cache_band_v, state_ssm_re, state_ssm_im, norm_g, w_in, ssm_a_re, ssm_a_im, ssm_log_dt, ssm_b_re, ssm_b_im, ssm_c_re, ssm_c_im, ssm_d, w_glu, b_glu, q_norm_g, k_norm_g, rel_bias, w_br_a, w_br_b, w_br_c, gate_b, w_out):
    depth = w_in.shape[0]
    yp, ys = x_prompt, x_sample
    p_states, s_states = [], []
    w_in, w_glu, w_br_a, w_br_b, w_br_c, w_out = (t.astype(BF16) for t in (w_in, w_glu, w_br_a, w_br_b, w_br_c, w_out))
    caches = (cache_sb_k, cache_sb_v, cache_band_k, cache_band_v)
    for l in range(depth):
        ssm = _discretize(ssm_a_re[l], ssm_a_im[l], ssm_log_dt[l], ssm_b_re[l], ssm_b_im[l])
        w = (norm_g[l], w_in, (*ssm, ssm_c_re[l], ssm_c_im[l], ssm_d[l]), w_glu, b_glu[l], q_norm_g[l], k_norm_g[l],
             rel_bias[l], w_br_a, w_br_b, w_br_c, gate_b[l], w_out)
        yp, sp = _layer(yp, l, w, None)
        ys, ss = _layer(ys, l, w, (*caches, state_ssm_re[l], state_ssm_im[l]))
        p_states.append(sp)
        s_states.append(ss)
    stk = lambda states, i: jnp.stack([st[i] for st in states], axis=0)
    return (yp, ys,
            *(stk(p_states, i) for i in range(6)),
            *(stk(s_states, i) for i in range(6)))
```
